```python
import math
import jax
import jax.numpy as jnp
from jax import lax
import numpy as np

D_MODEL = 1024
BATCH = 4
SEQ = 8192
DEPTH = 1
DEC_BATCH = 32
DEC_SEQ = 4
PAST_LEN = 16384
PAGE_SIZE = 128

A_HEADS = 8
A_HEAD_DIM = 64
MOBA_BLOCK = 256
MOBA_TOPK = 3
MOBA_Q_CHUNK = 32
ROPE_THETA = 10000.0
B_HEADS = 4
B_DK = 128
B_DV = 256
B_GATE_RANK = 16
B_GATE_NORM = 16.0
GLA_CHUNK = 64
N_EXPERTS = 32
TOP_K = 4
D_FF = 1024
SWIGLU_ALPHA = 1.702
SWIGLU_LIMIT = 7.0
MOE_BLOCK = 128
NORM_EPS = 1e-6

A_QK = A_HEADS * A_HEAD_DIM
B_K = B_HEADS * B_DK
B_V = B_HEADS * B_DV
IN_SPLITS = (A_QK, A_QK, A_QK, B_K, B_K, B_V, B_V, B_GATE_RANK)
D_IN = sum(IN_SPLITS)
IN_OFFSETS = tuple(int(v) for v in np.cumsum(IN_SPLITS)[:-1])
F32 = jnp.float32

kernel_name = 'moba_gla_moe_hybrid_step'


def rms_norm(x, g):
    xf = x.astype(F32)
    y = xf * lax.rsqrt(jnp.mean(xf * xf, axis=-1, keepdims=True) + NORM_EPS)
    return (y * g.astype(F32)).astype(x.dtype)


def rope(x, pos):
    half = x.shape[-1] // 2
    inv = ROPE_THETA ** (-jnp.arange(half, dtype=F32) / half)
    ang = pos.astype(F32)[:, None] * inv[None, :]
    cos, sin = jnp.cos(ang), jnp.sin(ang)
    xf = x.astype(F32)
    x1, x2 = xf[..., :half], xf[..., half:]
    return jnp.concatenate([x1 * cos - x2 * sin, x2 * cos + x1 * sin], axis=-1).astype(x.dtype)


def split_heads(t, n_heads):
    n, T, _ = t.shape
    return t.reshape(n, T, n_heads, -1).transpose(0, 2, 1, 3)


def merge_heads(t):
    n, H, T, d = t.shape
    return t.transpose(0, 2, 1, 3).reshape(n, T, H * d)


def project_inputs(x, pos, norm1_g, w_in, q_norm_g, k_norm_g, w_gate_up, b_gate_up):
    h = rms_norm(x, norm1_g)
    z = h @ w_in
    qa, ka, va, qb, kb, vb, gb, lr = jnp.split(z, IN_OFFSETS, axis=-1)
    qa = rope(rms_norm(split_heads(qa, A_HEADS), q_norm_g), pos)
    ka = rope(rms_norm(split_heads(ka, A_HEADS), k_norm_g), pos)
    va = split_heads(va, A_HEADS)
    qb = split_heads(qb, B_HEADS).astype(F32) * (B_DK ** -0.5)
    kb = split_heads(kb, B_HEADS).astype(F32)
    vb = split_heads(vb, B_HEADS).astype(F32)
    log_a = jax.nn.log_sigmoid((lr @ w_gate_up + b_gate_up).astype(F32)) / B_GATE_NORM
    log_a = split_heads(log_a, B_HEADS)
    return h, qa, ka, va, qb, kb, vb, log_a, gb


def moba_core(q, pos_q, k_own, v_own, pos_own, k_sel=None, v_sel=None, sel_valid=None):
    qf = q.astype(F32) * (A_HEAD_DIM ** -0.5)
    start = (pos_q // MOBA_BLOCK) * MOBA_BLOCK
    own_ok = (pos_own[None, :] <= pos_q[:, None]) & (pos_own[None, :] >= start[:, None])
    s_own = jnp.where(own_ok, jnp.einsum('nhtd,nhld->nhtl', qf, k_own.astype(F32)), -jnp.inf)
    if k_sel is None:
        p = jax.nn.softmax(s_own, axis=-1)
        o = jnp.einsum('nhtl,nhld->nhtd', p, v_own.astype(F32))
    else:
        n, H, T, S, L, d = k_sel.shape
        s_sel = jnp.einsum('nhtd,nhtsld->nhtsl', qf, k_sel.astype(F32))
        s_sel = jnp.where(sel_valid[..., None], s_sel, -jnp.inf).reshape(n, H, T, S * L)
        p = jax.nn.softmax(jnp.concatenate([s_sel, s_own], axis=-1), axis=-1)
        o = (jnp.einsum('nhtk,nhtkd->nhtd', p[..., :S * L], v_sel.reshape(n, H, T, S * L, d).astype(F32))
             + jnp.einsum('nhtl,nhld->nhtd', p[..., S * L:], v_own.astype(F32)))
    return o.astype(q.dtype)


def moba_prompt(q, k, v):
    n, H, T, d = q.shape
    nb = -(-T // MOBA_BLOCK)
    n_sel = min(MOBA_TOPK, nb - 1)
    pad = nb * MOBA_BLOCK - T
    kp = jnp.pad(k, ((0, 0), (0, 0), (0, pad), (0, 0)))
    vp = jnp.pad(v, ((0, 0), (0, 0), (0, pad), (0, 0)))
    pos = jnp.arange(T)
    kb = vb = idx = valid = None
    if n_sel > 0:
        kb = kp.reshape(n, H, nb, MOBA_BLOCK, d)
        vb = vp.reshape(n, H, nb, MOBA_BLOCK, d)
        k_mean = jnp.mean(kb.astype(F32), axis=3)
        gate = jnp.einsum('nhtd,nhbd->nhtb', q.astype(F32), k_mean)
        cur = pos // MOBA_BLOCK
        gate = jnp.where(jnp.arange(nb)[None, :] < cur[:, None], gate, -jnp.inf)
        _, idx = lax.top_k(gate, n_sel)
        valid = jnp.arange(n_sel)[None, :] < cur[:, None]
    ni = jnp.arange(n)[:, None, None, None]
    hi = jnp.arange(H)[None, :, None, None]

    def chunk(c):
        t0 = c * MOBA_Q_CHUNK
        b0 = (t0 // MOBA_BLOCK) * MOBA_BLOCK
        qc = lax.dynamic_slice_in_dim(q, t0, MOBA_Q_CHUNK, axis=2)
        pos_q = t0 + jnp.arange(MOBA_Q_CHUNK)
        k_own = lax.dynamic_slice_in_dim(kp, b0, MOBA_BLOCK, axis=2)
        v_own = lax.dynamic_slice_in_dim(vp, b0, MOBA_BLOCK, axis=2)
        pos_own = b0 + jnp.arange(MOBA_BLOCK)
        if n_sel == 0:
            return moba_core(qc, pos_q, k_own, v_own, pos_own)
        ic = lax.dynamic_slice_in_dim(idx, t0, MOBA_Q_CHUNK, axis=2)
        vc = lax.dynamic_slice_in_dim(valid, t0, MOBA_Q_CHUNK, axis=0)
        return moba_core(qc, pos_q, k_own, v_own, pos_own, kb[ni, hi, ic], vb[ni, hi, ic], vc)

    o = lax.map(chunk, jnp.arange(T // MOBA_Q_CHUNK))
    return o.transpose(1, 2, 0, 3, 4).reshape(n, H, T, d)


def moba_sample(q, k_new, v_new, cache_k, cache_v, page_table):
    n, H, T, d = q.shape
    ppb = MOBA_BLOCK // PAGE_SIZE
    n_full = PAST_LEN // MOBA_BLOCK
    n_sel = min(MOBA_TOPK, n_full)
    n_own_past = PAST_LEN - n_full * MOBA_BLOCK
    pos_q = PAST_LEN + jnp.arange(T)
    pos_own = n_full * MOBA_BLOCK + jnp.arange(n_own_past + T)
    if n_own_past > 0:
        own_pages = page_table[:, n_full * ppb: n_full * ppb + n_own_past // PAGE_SIZE]
        def own_rows(cache):
            r = cache[own_pages]
            return r.transpose(0, 2, 1, 3, 4).reshape(n, H, n_own_past, d)
        k_own = jnp.concatenate([own_rows(cache_k).astype(k_new.dtype), k_new], axis=2)
        v_own = jnp.concatenate([own_rows(cache_v).astype(v_new.dtype), v_new], axis=2)
    else:
        k_own, v_own = k_new, v_new
    if n_sel == 0:
        return moba_core(q, pos_q, k_own, v_own, pos_own)
    rows = cache_k[page_table[:, :n_full * ppb]]
    k_mean = rows.astype(F32).reshape(n, n_full, ppb, H, PAGE_SIZE, d).mean(axis=(2, 4)).transpose(0, 2, 1, 3)
    gate = jnp.einsum('nhtd,nhbd->nhtb', q.astype(F32), k_mean)
    _, idx = lax.top_k(gate, n_sel)
    lpages = idx[..., None] * ppb + jnp.arange(ppb)
    phys = page_table[jnp.arange(n)[:, None, None, None, None], lpages]
    hi = jnp.arange(H)[None, :, None, None, None]
    k_sel = cache_k[phys, hi].reshape(n, H, T, n_sel, MOBA_BLOCK, d)
    v_sel = cache_v[phys, hi].reshape(n, H, T, n_sel, MOBA_BLOCK, d)
    return moba_core(q, pos_q, k_own, v_own, pos_own, k_sel, v_sel, jnp.ones((T, n_sel), bool))


def gla_chunked(q, k, v, log_a, s0):
    n, H, T, dk = q.shape
    C = math.gcd(GLA_CHUNK, T)
    nc = T // C
    def to_chunks(t):
        return t.reshape(n, H, nc, C, t.shape[-1]).transpose(2, 0, 1, 3, 4)
    causal = jnp.tril(jnp.ones((C, C), bool))

    def step(s, inp):
        qc, kc, vc, ac = inp
        b = jnp.cumsum(ac, axis=-2)
        diff = b[..., :, None, :] - b[..., None, :, :]
        decay = jnp.exp(jnp.where(causal[:, :, None], diff, -jnp.inf))
        attn = jnp.einsum('nhtd,nhsd,nhtsd->nhts', qc, kc, decay)
        o = jnp.einsum('nhts,nhsv->nhtv', attn, vc) + jnp.einsum('nhtd,nhdv->nhtv', qc * jnp.exp(b), s)
        b_last = b[..., -1, :]
        s_new = jnp.exp(b_last)[..., None] * s + jnp.einsum('nhsd,nhsv->nhdv', kc * jnp.exp(b_last[..., None, :] - b), vc)
        return s_new, o

    s_fin, o = lax.scan(step, s0, (to_chunks(q), to_chunks(k), to_chunks(v), to_chunks(log_a)))
    return o.transpose(1, 2, 0, 3, 4).reshape(n, H, T, -1), s_fin


def moe_ffn(h, w_router, b_router, w_up, b_up, w_down, b_down):
    n_tok, d = h.shape
    logits = (h @ w_router + b_router).astype(F32)
    top_val, top_idx = lax.top_k(logits, TOP_K)
    gate = jax.nn.softmax(top_val, axis=-1)
    n_assign = n_tok * TOP_K
    flat_e = top_idx.reshape(-1)
    order = jnp.argsort(flat_e)
    sorted_e = flat_e[order]
    counts = jnp.bincount(flat_e, length=N_EXPERTS)
    start = jnp.cumsum(counts) - counts
    padded = (counts + MOE_BLOCK - 1) // MOE_BLOCK * MOE_BLOCK
    pend = jnp.cumsum(padded)
    pstart = pend - padded
    dest_sorted = (pstart[sorted_e] + jnp.arange(n_assign) - start[sorted_e]).astype(jnp.int32)
    dest = jnp.zeros((n_assign,), jnp.int32).at[order].set(dest_sorted)
    n_blocks = -(-n_assign // MOE_BLOCK) + N_EXPERTS
    buf = jnp.zeros((n_blocks * MOE_BLOCK, d), h.dtype).at[dest].set(jnp.repeat(h, TOP_K, axis=0))
    blk_e = jnp.minimum(jnp.searchsorted(pend, jnp.arange(n_blocks) * MOE_BLOCK, side='right'), N_EXPERTS - 1)

    def expert_block(args):
        xb, e = args
        u = xb @ w_up[e] + b_up[e]
        x_glu = jnp.minimum(u[:, ::2], SWIGLU_LIMIT)
        x_lin = jnp.clip(u[:, 1::2], -SWIGLU_LIMIT, SWIGLU_LIMIT)
        a = x_glu * jax.nn.sigmoid(SWIGLU_ALPHA * x_glu) * (x_lin + 1.0)
        return a @ w_down[e] + b_down[e]

    y = lax.map(expert_block, (buf.reshape(n_blocks, MOE_BLOCK, d), blk_e)).reshape(-1, d)
    y_tk = y[dest].reshape(n_tok, TOP_K, d)
    return jnp.einsum('tkd,tk->td', y_tk, gate.astype(y_tk.dtype))


def finish_layer(x, h, o_a, o_b, gb, gla_norm_g, w_branch_a, w_branch_b, w_merge_gate, w_out,
                 norm2_g, w_router, b_router, w_up, b_up, w_down, b_down):
    o_b = merge_heads(rms_norm(o_b, gla_norm_g)).astype(x.dtype) * jax.nn.silu(gb)
    y_a = merge_heads(o_a) @ w_branch_a
    y_b = o_b @ w_branch_b
    g = jax.nn.sigmoid(h @ w_merge_gate)
    mixed = g[..., :D_MODEL] * y_a + g[..., D_MODEL:] * y_b
    x = x + mixed @ w_out
    n, T, d = x.shape
    h2 = rms_norm(x, norm2_g).reshape(n * T, d)
    return x + moe_ffn(h2, w_router, b_router, w_up, b_up, w_down, b_down).reshape(n, T, d)


def setup_inputs(seed: int = 0) -> dict:
    key = jax.random.key(seed)
    ks = jax.random.split(key, 24)
    n_pages = PAST_LEN // PAGE_SIZE
    n_used = DEC_BATCH * n_pages
    n_pool = n_used + max(1, n_used // 4)
    def nrm(k, shape, scale):
        return jax.random.normal(k, shape, F32) * scale
    page_table = jax.random.permutation(ks[5], n_pool)[:n_used].reshape(DEC_BATCH, n_pages).astype(jnp.int32)
    return {
        'x_prompt': nrm(ks[0], (BATCH, SEQ, D_MODEL), 1.0),
        'x_sample': nrm(ks[1], (DEC_BATCH, DEC_SEQ, D_MODEL), 1.0),
        'cache_k': nrm(ks[2], (DEPTH, n_pool, A_HEADS, PAGE_SIZE, A_HEAD_DIM), 1.0),
        'cache_v': nrm(ks[3], (DEPTH, n_pool, A_HEADS, PAGE_SIZE, A_HEAD_DIM), 1.0),
        'state_gla': nrm(ks[4], (DEPTH, DEC_BATCH, B_HEADS, B_DK, B_DV), B_DK ** -0.5),
        'page_table': page_table,
        'norm1_g': 1.0 + nrm(ks[6], (DEPTH, D_MODEL), 0.02),
        'w_in': nrm(ks[7], (DEPTH, D_MODEL, D_IN), D_MODEL ** -0.5),
        'q_norm_g': 1.0 + nrm(ks[8], (DEPTH, A_HEAD_DIM), 0.02),
        'k_norm_g': 1.0 + nrm(ks[9], (DEPTH, A_HEAD_DIM), 0.02),
        'w_gate_up': nrm(ks[10], (DEPTH, B_GATE_RANK, B_K), B_GATE_RANK ** -0.5),
        'b_gate_up': 1.0 + nrm(ks[11], (DEPTH, B_K), 0.1),
        'gla_norm_g': 1.0 + nrm(ks[12], (DEPTH, B_DV), 0.02),
        'w_branch_a': nrm(ks[13], (DEPTH, A_QK, D_MODEL), A_QK ** -0.5),
        'w_branch_b': nrm(ks[14], (DEPTH, B_V, D_MODEL), B_V ** -0.5),
        'w_merge_gate': nrm(ks[15], (DEPTH, D_MODEL, 2 * D_MODEL), D_MODEL ** -0.5),
        'w_out': nrm(ks[16], (DEPTH, D_MODEL, D_MODEL), D_MODEL ** -0.5),
        'norm2_g': 1.0 + nrm(ks[17], (DEPTH, D_MODEL), 0.02),
        'w_router': nrm(ks[18], (DEPTH, D_MODEL, N_EXPERTS), D_MODEL ** -0.5),
        'b_router': nrm(ks[19], (DEPTH, N_EXPERTS), 0.01),
        'w_up': nrm(ks[20], (DEPTH, N_EXPERTS, D_MODEL, 2 * D_FF), D_MODEL ** -0.5),
        'b_up': nrm(ks[21], (DEPTH, N_EXPERTS, 2 * D_FF), 0.01),
        'w_down': nrm(ks[22], (DEPTH, N_EXPERTS, D_FF, D_MODEL), D_FF ** -0.5),
        'b_down': nrm(ks[23], (DEPTH, N_EXPERTS, D_MODEL), 0.01),
    }


def reference(x_prompt, x_sample, cache_k, cache_v, state_gla, page_table, norm1_g, w_in, q_norm_g,
              k_norm_g, w_gate_up, b_gate_up, gla_norm_g, w_branch_a, w_branch_b, w_merge_gate, w_out,
              norm2_g, w_router, b_router, w_up, b_up, w_down, b_down):
    pos_p = jnp.arange(x_prompt.shape[1])
    pos_s = PAST_LEN + jnp.arange(x_sample.shape[1])
    xp, xs = x_prompt, x_sample
    kp_l, vp_l, ks_l, vs_l, sp_l, ss_l = [], [], [], [], [], []
    for l in range(DEPTH):
        proj = (norm1_g[l], w_in[l], q_norm_g[l], k_norm_g[l], w_gate_up[l], b_gate_up[l])
        rest = (gla_norm_g[l], w_branch_a[l], w_branch_b[l], w_merge_gate[l], w_out[l], norm2_g[l],
                w_router[l], b_router[l], w_up[l], b_up[l], w_down[l], b_down[l])
        h, qa, ka, va, qb, kb, vb, la, gb = project_inputs(xp, pos_p, *proj)
        o_a = moba_prompt(qa, ka, va)
        s0 = jnp.zeros((xp.shape[0], B_HEADS, B_DK, B_DV), F32)
        o_b, s_p = gla_chunked(qb, kb, vb, la, s0)
        xp = finish_layer(xp, h, o_a, o_b, gb, *rest)
        kp_l.append(ka)
        vp_l.append(va)
        sp_l.append(s_p.astype(state_gla.dtype))
        h, qa, ka, va, qb, kb, vb, la, gb = project_inputs(xs, pos_s, *proj)
        o_a = moba_sample(qa, ka, va, cache_k[l], cache_v[l], page_table)
        o_b, s_s = gla_chunked(qb, kb, vb, la, state_gla[l].astype(F32))
        xs = finish_layer(xs, h, o_a, o_b, gb, *rest)
        ks_l.append(ka)
        vs_l.append(va)
        ss_l.append(s_s.astype(state_gla.dtype))
    k_prompt = jnp.stack(kp_l)
    v_prompt = jnp.stack(vp_l)
    k_sample = jnp.stack(ks_l)
    v_sample = jnp.stack(vs_l)
    s_prompt = jnp.stack(sp_l)
    s_sample = jnp.stack(ss_l)
    return (xp, xs, k_prompt, v_prompt, k_sample, v_sample, s_prompt, s_sample)
```

```python
import functools

import jax
import jax.numpy as jnp
from jax import lax
from jax.experimental import pallas as pl
from jax.experimental.pallas import tpu as pltpu

F32 = jnp.float32
BF16 = jnp.bfloat16
HIGHEST = lax.Precision.HIGHEST

MOBA_BLOCK = 256
MOBA_TOPK = 3
ROPE_THETA = 10000.0
GLA_GATE_NORM = 16.0
NORM_EPS = 1e-6
MOE_TOP_K = 4
SWIGLU_ALPHA = 1.702
SWIGLU_LIMIT = 7.0

LANES = 128
VMEM_LIMIT_BYTES = 56 * 1024 * 1024

MASK_VALUE = -1e30
PROJ_ROWS = 512
GLA_CHUNK = 64
GLA_SUB = 16
GLA_STEP_ROWS = 512
MOE_ROWS = 512
SAMPLE_KMEAN_PAGES = 16


def _cparams(*sem):
    return pltpu.CompilerParams(dimension_semantics=sem, vmem_limit_bytes=VMEM_LIMIT_BYTES)


def _rms(x):
    return x * lax.rsqrt(jnp.mean(x * x, axis=-1, keepdims=True) + NORM_EPS)


def _dot(a, b):
    return jnp.dot(a, b, preferred_element_type=F32)


def _dot_nt(a, b, precision=None):
    return lax.dot_general(a, b, (((1,), (1,)), ((), ())), precision=precision,
                           preferred_element_type=F32)


def _dot_tn(a, b):
    return lax.dot_general(a, b, (((0,), (0,)), ((), ())), preferred_element_type=F32)


def _top_mask(g, col, n_pick, n_col):
    sel = jnp.zeros(g.shape, F32)
    picks = []
    for _ in range(n_pick):
        mx = jnp.max(g, axis=1, keepdims=True)
        cand = jnp.where((g == mx) & (mx > -jnp.inf), col, n_col)
        idx = jnp.min(cand, axis=1, keepdims=True)
        pick = col == idx
        sel = jnp.where(pick, 1.0, sel)
        g = jnp.where(pick, -jnp.inf, g)
        picks.append((idx, mx))
    return sel, picks


def _proj_kernel(x_ref, g1_ref, win_ref, qg_ref, kg_ref, wgu_ref, bgu_ref, cos_ref, sin_ref,
                 qa_ref, ka_ref, va_ref, qb_ref, kb_ref, vb_ref, la_ref, gb_ref,
                 *, heads, hd, bk, bv):
    x = x_ref[0]
    hb = (_rms(x) * g1_ref[...]).astype(BF16)

    def mm(lo, hi):
        return _dot(hb, win_ref[:, lo:hi])

    cos = cos_ref[...]
    sin = sin_ref[...]

    def norm_rope(z, g_ref, out_ref):
        for h in range(heads):
            y = _rms(z[:, h * hd:(h + 1) * hd]) * g_ref[...]
            rot = jnp.concatenate([y[:, hd // 2:], y[:, :hd // 2]], axis=-1)
            out_ref[0, h] = y * cos + rot * sin

    aqk = heads * hd
    norm_rope(mm(0, aqk), qg_ref, qa_ref)
    norm_rope(mm(aqk, 2 * aqk), kg_ref, ka_ref)
    zv = mm(2 * aqk, 3 * aqk)
    for h in range(heads):
        va_ref[0, h] = zv[:, h * hd:(h + 1) * hd]
    o = 3 * aqk
    qb_ref[0] = mm(o, o + bk)
    kb_ref[0] = mm(o + bk, o + 2 * bk)
    vb_ref[0] = mm(o + 2 * bk, o + 2 * bk + bv).astype(BF16)
    gb_ref[0] = mm(o + 2 * bk + bv, o + 2 * bk + 2 * bv).astype(BF16)
    lr = mm(o + 2 * bk + 2 * bv, o + 2 * bk + 2 * bv + LANES)
    u = _dot(lr.astype(BF16), wgu_ref[...]) + bgu_ref[...]
    log_sig = jnp.minimum(u, 0.0) - jnp.log(1.0 + jnp.exp(-jnp.abs(u)))
    la_ref[0] = log_sig / GLA_GATE_NORM


def _project(x, cos, sin, norm1_g, w_in_p, q_norm_g, k_norm_g, w_gu_p, b_gu, *, heads, hd, bk, bv):
    n, t, d = x.shape
    tm = min(PROJ_ROWS, t)
    assert t % tm == 0
    grid = (n, t // tm)
    full = lambda a: pl.BlockSpec(a.shape, lambda b, i: (0,) * a.ndim)
    head_out = pl.BlockSpec((1, heads, tm, hd), lambda b, i: (b, 0, i, 0))
    row_out = lambda w: pl.BlockSpec((1, tm, w), lambda b, i: (b, i, 0))
    return pl.pallas_call(
        functools.partial(_proj_kernel, heads=heads, hd=hd, bk=bk, bv=bv),
        grid=grid,
        in_specs=[pl.BlockSpec((1, tm, d), lambda b, i: (b, i, 0)),
                  full(norm1_g), full(w_in_p), full(q_norm_g), full(k_norm_g), full(w_gu_p),
                  full(b_gu),
                  pl.BlockSpec((tm, hd), lambda b, i: (i, 0)),
                  pl.BlockSpec((tm, hd), lambda b, i: (i, 0))],
        out_specs=[head_out, head_out, head_out, row_out(bk), row_out(bk), row_out(bv),
                   row_out(bk), row_out(bv)],
        out_shape=[jax.ShapeDtypeStruct((n, heads, t, hd), F32),
                   jax.ShapeDtypeStruct((n, heads, t, hd), F32),
                   jax.ShapeDtypeStruct((n, heads, t, hd), F32),
                   jax.ShapeDtypeStruct((n, t, bk), F32),
                   jax.ShapeDtypeStruct((n, t, bk), F32),
                   jax.ShapeDtypeStruct((n, t, bv), BF16),
                   jax.ShapeDtypeStruct((n, t, bk), F32),
                   jax.ShapeDtypeStruct((n, t, bv), BF16)],
        compiler_params=_cparams("parallel", "parallel"),
        name="proj",
    )(x, norm1_g, w_in_p, q_norm_g, k_norm_g, w_gu_p, b_gu, cos, sin)


def _moba_kernel(q_ref, k_ref, v_ref, o_ref, kmean_sc, m_sc, l_sc, acc_sc, *, nb, blk, hd, hp):
    c = pl.program_id(2)

    @pl.when(c == 0)
    def _():
        for i in range(hp):
            kmean_sc[i] = jnp.mean(k_ref[0, i].reshape(nb, blk, hd), axis=1)

    scale = hd ** -0.5
    col = lax.broadcasted_iota(jnp.int32, (blk, nb), 1)
    causal = (lax.broadcasted_iota(jnp.int32, (blk, blk), 1)
              <= lax.broadcasted_iota(jnp.int32, (blk, blk), 0))
    start = pl.multiple_of(c * blk, blk)
    n_sel = min(MOBA_TOPK, nb - 1)

    sels, qss = [], []
    for i in range(hp):
        q = q_ref[0, i]
        gate = _dot_nt(q, kmean_sc[i], precision=HIGHEST)
        sel, _ = _top_mask(jnp.where(col < c, gate, -jnp.inf), col, n_sel, nb)
        sels.append(sel)
        qs = (q * scale).astype(BF16)
        qss.append(qs)
        s = _dot_nt(qs, k_ref[0, i, pl.ds(start, blk), :].astype(BF16))
        s = jnp.where(causal, s, MASK_VALUE)
        m = jnp.max(s, axis=1, keepdims=True)
        p = jnp.exp(s - m)
        m_sc[i] = m
        l_sc[i] = jnp.sum(p, axis=1, keepdims=True)
        acc_sc[i] = _dot(p.astype(BF16), v_ref[0, i, pl.ds(start, blk), :].astype(BF16))

    def body(j, carry):
        off = pl.multiple_of(j * blk, blk)
        for i in range(hp):
            picked = jnp.max(jnp.where(col == j, sels[i], 0.0), axis=1, keepdims=True) > 0.0
            s = _dot_nt(qss[i], k_ref[0, i, pl.ds(off, blk), :].astype(BF16))
            s = jnp.where(picked, s, MASK_VALUE)
            m_old = m_sc[i]
            m_new = jnp.maximum(m_old, jnp.max(s, axis=1, keepdims=True))
            alpha = jnp.exp(m_old - m_new)
            p = jnp.exp(s - m_new)
            l_sc[i] = alpha * l_sc[i] + jnp.sum(p, axis=1, keepdims=True)
            acc_sc[i] = alpha * acc_sc[i] + _dot(p.astype(BF16),
                                                 v_ref[0, i, pl.ds(off, blk), :].astype(BF16))
            m_sc[i] = m_new
        return carry

    lax.fori_loop(0, c, body, 0)
    for i in range(hp):
        o_ref[0, :, i * hd:(i + 1) * hd] = (acc_sc[i] / l_sc[i]).astype(BF16)


def _moba_prompt(q, k, v):
    n, heads, t, hd = q.shape
    blk = MOBA_BLOCK
    assert t % blk == 0 and LANES % hd == 0
    hp = LANES // hd
    assert heads % hp == 0
    nb = t // blk
    kv_spec = pl.BlockSpec((1, hp, t, hd), lambda b, g, c: (b, g, 0, 0))
    return pl.pallas_call(
        functools.partial(_moba_kernel, nb=nb, blk=blk, hd=hd, hp=hp),
        grid=(n, heads // hp, nb),
        in_specs=[pl.BlockSpec((1, hp, blk, hd), lambda b, g, c: (b, g, c, 0)), kv_spec, kv_spec],
        out_specs=pl.BlockSpec((1, blk, hp * hd), lambda b, g, c: (b, c, g)),
        out_shape=jax.ShapeDtypeStruct((n, t, heads * hd), BF16),
        scratch_shapes=[pltpu.VMEM((hp, nb, hd), F32), pltpu.VMEM((hp, blk, 1), F32),
                        pltpu.VMEM((hp, blk, 1), F32), pltpu.VMEM((hp, blk, hd), F32)],
        compiler_params=_cparams("parallel", "parallel", "arbitrary"),
        name="moba_prompt",
    )(q, k, v)


def _kmean_pages_kernel(pt_ref, *refs, n_pg, ppb, rows):
    del pt_ref
    out_ref = refs[n_pg]
    for b in range(n_pg // ppb):
        s = jnp.sum(refs[b * ppb][0, 0], axis=1)
        for j in range(1, ppb):
            s = s + jnp.sum(refs[b * ppb + j][0, 0], axis=1)
        out_ref[0, b] = s / rows


def _sample_kmean(cache_k, layer, page_table, n_full, ppb):
    _, _, heads, page, hd = cache_k.shape
    n, n_pages = page_table.shape
    n_pg = SAMPLE_KMEAN_PAGES
    assert (n_full * ppb) % n_pg == 0 and n_pg % ppb == 0
    steps = n_full * ppb // n_pg
    specs = [pl.BlockSpec((1, 1, heads, page, hd),
                          functools.partial(lambda b, g, pt, j: (layer, pt[b * n_pages + g * n_pg + j], 0, 0, 0), j=j))
             for j in range(n_pg)]
    return pl.pallas_call(
        functools.partial(_kmean_pages_kernel, n_pg=n_pg, ppb=ppb, rows=float(ppb * page)),
        grid_spec=pltpu.PrefetchScalarGridSpec(
            num_scalar_prefetch=1, grid=(n, steps), in_specs=specs,
            out_specs=pl.BlockSpec((1, n_pg // ppb, heads, hd), lambda b, g, pt: (b, g, 0, 0))),
        out_shape=jax.ShapeDtypeStruct((n, n_full, heads, hd), F32),
        compiler_params=_cparams("parallel", "arbitrary"),
        name="sample_kmean",
    )(page_table.reshape(-1), *([cache_k] * n_pg))


def _sample_pick_kernel(q_ref, km_ref, idx_ref, *, heads, n_full, n_sel):
    col = lax.broadcasted_iota(jnp.int32, (q_ref.shape[2], n_full), 1)
    for h in range(heads):
        gate = _dot_nt(q_ref[0, h], km_ref[0, h], precision=HIGHEST)
        _, picks = _top_mask(gate, col, n_sel, n_full)
        idx_ref[0, h] = jnp.concatenate([p[0] for p in picks], axis=1)


def _sample_pick(q, kmean, n_sel):
    n, heads, t, hd = q.shape
    n_full = kmean.shape[2]
    return pl.pallas_call(
        functools.partial(_sample_pick_kernel, heads=heads, n_full=n_full, n_sel=n_sel),
        grid=(n,),
        in_specs=[pl.BlockSpec((1, heads, t, hd), lambda b: (b, 0, 0, 0)),
                  pl.BlockSpec((1, heads, n_full, hd), lambda b: (b, 0, 0, 0))],
        out_specs=pl.BlockSpec((1, heads, t, n_sel), lambda b: (b, 0, 0, 0)),
        out_shape=jax.ShapeDtypeStruct((n, heads, t, n_sel), jnp.int32),
        compiler_params=_cparams("parallel"),
        name="sample_pick",
    )(q, kmean)


def _sample_attn_kernel(ph_ref, q_ref, kn_ref, vn_ref, *refs, n_blk):
    del ph_ref
    k_refs, v_refs, o_ref = refs[:n_blk], refs[n_blk:2 * n_blk], refs[2 * n_blk]
    t = pl.program_id(2)
    tq, hd = q_ref.shape[2], q_ref.shape[3]
    qs = (q_ref[0, 0] * hd ** -0.5).astype(BF16)
    ks = jnp.concatenate([r[0, 0, 0] for r in k_refs], axis=0).astype(BF16)
    vs = jnp.concatenate([r[0, 0, 0] for r in v_refs], axis=0).astype(BF16)
    s_sel = _dot_nt(qs, ks)
    s_own = _dot_nt(qs, kn_ref[0, 0].astype(BF16))
    own_ok = (lax.broadcasted_iota(jnp.int32, (tq, tq), 1)
              <= lax.broadcasted_iota(jnp.int32, (tq, tq), 0))
    s_own = jnp.where(own_ok, s_own, MASK_VALUE)
    m = jnp.maximum(jnp.max(s_sel, axis=1, keepdims=True), jnp.max(s_own, axis=1, keepdims=True))
    p_sel = jnp.exp(s_sel - m)
    p_own = jnp.exp(s_own - m)
    denom = jnp.sum(p_sel, axis=1, keepdims=True) + jnp.sum(p_own, axis=1, keepdims=True)
    o = (_dot(p_sel.astype(BF16), vs) + _dot(p_own.astype(BF16), vn_ref[0, 0].astype(BF16))) / denom
    row = lax.broadcasted_iota(jnp.int32, (tq, hd), 0)
    o_ref[0, 0, pl.ds(t, 1), :] = jnp.sum(jnp.where(row == t, o, 0.0), axis=0, keepdims=True)


def _sample_attn(q, k_new, v_new, cache_k, cache_v, layer, phys):
    n, heads, t, hd = q.shape
    page = cache_k.shape[3]
    n_blk = phys.shape[-1]

    def page_spec(j):
        return pl.BlockSpec(
            (1, 1, 1, page, hd),
            functools.partial(lambda b, h, tt, ph, j: (layer, ph[((b * heads + h) * t + tt) * n_blk + j], h, 0, 0), j=j))

    own = pl.BlockSpec((1, 1, t, hd), lambda b, h, tt, ph: (b, h, 0, 0))
    return pl.pallas_call(
        functools.partial(_sample_attn_kernel, n_blk=n_blk),
        grid_spec=pltpu.PrefetchScalarGridSpec(
            num_scalar_prefetch=1, grid=(n, heads, t),
            in_specs=[own, own, own] + [page_spec(j) for j in range(n_blk)] * 2,
            out_specs=own),
        out_shape=jax.ShapeDtypeStruct((n, heads, t, hd), F32),
        compiler_params=_cparams("parallel", "parallel", "arbitrary"),
        name="sample_attn",
    )(phys.reshape(-1), q, k_new, v_new, *([cache_k] * n_blk), *([cache_v] * n_blk))


def _gla_kernel(q_ref, k_ref, v_ref, la_ref, gb_ref, gn_ref, o_ref, sfin_ref, st_sc,
                *, chunk, sub, n_chunk, dk):
    step = pl.program_id(2)

    @pl.when(step == 0)
    def _():
        st_sc[...] = jnp.zeros(st_sc.shape, F32)

    r_i = lax.broadcasted_iota(jnp.int32, (chunk, chunk), 0)
    c_i = lax.broadcasted_iota(jnp.int32, (chunk, chunk), 1)
    tri = r_i >= c_i
    tri_f = tri.astype(F32)
    row_k = lax.broadcasted_iota(jnp.int32, (chunk, dk), 0)
    q_scale = dk ** -0.5

    def one_chunk(ci, carry):
        off = pl.multiple_of(ci * chunk, chunk)
        q = q_ref[0, pl.ds(off, chunk), :] * q_scale
        k = k_ref[0, pl.ds(off, chunk), :]
        v = v_ref[0, pl.ds(off, chunk), :]
        b = jnp.dot(tri_f, la_ref[0, pl.ds(off, chunk), :], precision=HIGHEST,
                    preferred_element_type=F32)
        st = st_sc[...]
        o = _dot_nt((q * jnp.exp(b)).astype(BF16), st.astype(BF16))
        rows = []
        for i in range(chunk // sub):
            lo, hi = i * sub, (i + 1) * sub
            ref = b[lo - 1:lo, :] if i > 0 else jnp.zeros((1, dk), F32)
            qt = (q[lo:hi] * jnp.exp(b[lo:hi] - ref)).astype(BF16)
            kt = (k * jnp.exp(jnp.where(row_k < hi, ref - b, -jnp.inf))).astype(BF16)
            rows.append(_dot_nt(qt, kt))
        attn = jnp.where(tri, jnp.concatenate(rows, axis=0), 0.0)
        o = o + _dot(attn.astype(BF16), v)
        b_last = b[chunk - 1:chunk, :]
        k_dec = (k * jnp.exp(b_last - b)).astype(BF16)
        st_sc[...] = st * jnp.exp(b_last) + _dot_tn(v, k_dec)
        gb = gb_ref[0, pl.ds(off, chunk), :].astype(F32)
        o_ref[0, pl.ds(off, chunk), :] = (_rms(o) * gn_ref[...] * (gb * jax.nn.sigmoid(gb))).astype(BF16)
        return carry

    lax.fori_loop(0, n_chunk, one_chunk, 0)

    @pl.when(step == pl.num_programs(2) - 1)
    def _():
        sfin_ref[0, 0] = st_sc[...]


def _gla_prompt(qb, kb, vb, la, gb, gla_norm_g, *, heads, dk, dv):
    n, t, _ = qb.shape
    rows = min(GLA_STEP_ROWS, t)
    chunk = min(GLA_CHUNK, rows)
    assert t % rows == 0 and rows % chunk == 0 and chunk % GLA_SUB == 0
    kspec = pl.BlockSpec((1, rows, dk), lambda b, h, s: (b, s, h))
    vspec = pl.BlockSpec((1, rows, dv), lambda b, h, s: (b, s, h))
    return pl.pallas_call(
        functools.partial(_gla_kernel, chunk=chunk, sub=GLA_SUB, n_chunk=rows // chunk, dk=dk),
        grid=(n, heads, t // rows),
        in_specs=[kspec, kspec, vspec, kspec, vspec, pl.BlockSpec((1, dv), lambda b, h, s: (0, 0))],
        out_specs=[vspec, pl.BlockSpec((1, 1, dv, dk), lambda b, h, s: (b, h, 0, 0))],
        out_shape=[jax.ShapeDtypeStruct((n, t, heads * dv), BF16),
                   jax.ShapeDtypeStruct((n, heads, dv, dk), F32)],
        scratch_shapes=[pltpu.VMEM((dv, dk), F32)],
        compiler_params=_cparams("parallel", "parallel", "arbitrary"),
        name="gla_prompt",
    )(qb, kb, vb, la, gb, gla_norm_g)


def _gla_sample_kernel(qt_ref, kt_ref, lat_ref, v_ref, gb_ref, gn_ref, s0_ref, o_ref, sfin_ref, *, dk):
    s = s0_ref[0, 0, 0]
    qt = qt_ref[0, 0] * dk ** -0.5
    kt = kt_ref[0, 0]
    at = jnp.exp(lat_ref[0, 0])
    v = v_ref[0].astype(F32)
    outs = []
    for t in range(v.shape[0]):
        s = at[:, t:t + 1] * s + kt[:, t:t + 1] * v[t:t + 1, :]
        outs.append(jnp.sum(qt[:, t:t + 1] * s, axis=0, keepdims=True))
    o = jnp.concatenate(outs, axis=0)
    gb = gb_ref[0].astype(F32)
    o_ref[0] = (_rms(o) * gn_ref[...] * (gb * jax.nn.sigmoid(gb))).astype(BF16)
    sfin_ref[0, 0] = s


def _gla_sample(qt, kt, lat, vb, gb, gla_norm_g, state, layer, *, dk, dv):
    n, heads, _, t = qt.shape
    tspec = pl.BlockSpec((1, 1, dk, t), lambda b, h: (b, h, 0, 0))
    vspec = pl.BlockSpec((1, t, dv), lambda b, h: (b, 0, h))
    return pl.pallas_call(
        functools.partial(_gla_sample_kernel, dk=dk),
        grid=(n, heads),
        in_specs=[tspec, tspec, tspec, vspec, vspec, pl.BlockSpec((1, dv), lambda b, h: (0, 0)),
                  pl.BlockSpec((1, 1, 1, dk, dv), lambda b, h: (layer, b, h, 0, 0))],
        out_specs=[vspec, pl.BlockSpec((1, 1, dk, dv), lambda b, h: (b, h, 0, 0))],
        out_shape=[jax.ShapeDtypeStruct((n, t, heads * dv), BF16),
                   jax.ShapeDtypeStruct((n, heads, dk, dv), F32)],
        compiler_params=_cparams("parallel", "parallel"),
        name="gla_sample",
    )(qt, kt, lat, vb, gb, gla_norm_g, state)


def _merge_kernel(x_ref, oa_ref, ob_ref, g1_ref, wa_ref, wb_ref, wmg_ref, wo_ref, g2_ref, wr_ref,
                  br_ref, x2_ref, h2_ref, idx_ref, gate_ref, *, d, n_exp):
    x = x_ref[...]
    hb = (_rms(x) * g1_ref[...]).astype(BF16)
    y_a = _dot(oa_ref[...], wa_ref[...])
    y_b = _dot(ob_ref[...], wb_ref[...])
    mixed = (jax.nn.sigmoid(_dot(hb, wmg_ref[:, :d])) * y_a
             + jax.nn.sigmoid(_dot(hb, wmg_ref[:, d:])) * y_b)
    x2 = x + _dot(mixed.astype(BF16), wo_ref[...])
    x2_ref[...] = x2
    h2 = _rms(x2) * g2_ref[...]
    h2_ref[...] = h2.astype(BF16)
    logits = jnp.dot(h2, wr_ref[...], precision=HIGHEST, preferred_element_type=F32) + br_ref[...]
    col = lax.broadcasted_iota(jnp.int32, logits.shape, 1)
    _, picks = _top_mask(logits, col, MOE_TOP_K, n_exp)
    top_val = jnp.concatenate([p[1] for p in picks], axis=1)
    e = jnp.exp(top_val - top_val[:, :1])
    idx_ref[...] = jnp.concatenate([p[0] for p in picks], axis=1)
    gate_ref[...] = e / jnp.sum(e, axis=1, keepdims=True)


def _merge(x, o_a, o_b, norm1_g, w_a, w_b, w_mg, w_o, norm2_g, w_router, b_router):
    n_tok, d = x.shape
    n_exp = w_router.shape[1]
    tm = min(PROJ_ROWS, n_tok)
    assert n_tok % tm == 0
    full = lambda a: pl.BlockSpec(a.shape, lambda i: (0,) * a.ndim)
    rows = lambda w: pl.BlockSpec((tm, w), lambda i: (i, 0))
    return pl.pallas_call(
        functools.partial(_merge_kernel, d=d, n_exp=n_exp),
        grid=(n_tok // tm,),
        in_specs=[rows(d), rows(o_a.shape[1]), rows(o_b.shape[1]), full(norm1_g), full(w_a),
                  full(w_b), full(w_mg), full(w_o), full(norm2_g), full(w_router), full(b_router)],
        out_specs=[rows(d), rows(d), rows(MOE_TOP_K), rows(MOE_TOP_K)],
        out_shape=[jax.ShapeDtypeStruct((n_tok, d), F32), jax.ShapeDtypeStruct((n_tok, d), BF16),
                   jax.ShapeDtypeStruct((n_tok, MOE_TOP_K), jnp.int32),
                   jax.ShapeDtypeStruct((n_tok, MOE_TOP_K), F32)],
        compiler_params=_cparams("parallel"),
        name="merge",
    )(x, o_a, o_b, norm1_g, w_a, w_b, w_mg, w_o, norm2_g, w_router, b_router)


def _expert_kernel(te_ref, nt_ref, x_ref, wup_ref, bup_ref, wdn_ref, bdn_ref, y_ref, *, d_ff):
    del te_ref
    i = pl.program_id(0)

    @pl.when(i < nt_ref[0])
    def _():
        u = _dot(x_ref[...], wup_ref[0]) + bup_ref[0]
        x_glu = jnp.minimum(u[:, :d_ff], SWIGLU_LIMIT)
        x_lin = jnp.clip(u[:, d_ff:], -SWIGLU_LIMIT, SWIGLU_LIMIT)
        a = x_glu * jax.nn.sigmoid(SWIGLU_ALPHA * x_glu) * (x_lin + 1.0)
        y_ref[...] = _dot(a.astype(BF16), wdn_ref[0]) + bdn_ref[0]

    @pl.when(i >= nt_ref[0])
    def _():
        y_ref[...] = jnp.zeros(y_ref.shape, F32)


def _experts(buf, tile_e, n_used, w_up_p, b_up_p, w_dn, b_dn):
    r, d = buf.shape
    n_exp, _, two_ff = w_up_p.shape
    d_ff = two_ff // 2
    return pl.pallas_call(
        functools.partial(_expert_kernel, d_ff=d_ff),
        grid_spec=pltpu.PrefetchScalarGridSpec(
            num_scalar_prefetch=2, grid=(r // MOE_ROWS,),
            in_specs=[pl.BlockSpec((MOE_ROWS, d), lambda i, te, nt: (i, 0)),
                      pl.BlockSpec((1, d, two_ff), lambda i, te, nt: (te[i], 0, 0)),
                      pl.BlockSpec((1, 1, two_ff), lambda i, te, nt: (te[i], 0, 0)),
                      pl.BlockSpec((1, d_ff, d), lambda i, te, nt: (te[i], 0, 0)),
                      pl.BlockSpec((1, 1, d), lambda i, te, nt: (te[i], 0, 0))],
            out_specs=pl.BlockSpec((MOE_ROWS, d), lambda i, te, nt: (i, 0))),
        out_shape=jax.ShapeDtypeStruct((r, d), F32),
        compiler_params=_cparams("arbitrary"),
        name="experts",
    )(tile_e, n_used, buf, w_up_p, b_up_p, w_dn, b_dn)


def _moe(h2, top_idx, gate, w_up_p, b_up_p, w_dn, b_dn):
    n_tok, d = h2.shape
    n_exp = w_up_p.shape[0]
    n_assign = n_tok * MOE_TOP_K
    flat_e = top_idx.reshape(-1)
    onehot = (flat_e[:, None] == jnp.arange(n_exp, dtype=jnp.int32)[None, :]).astype(jnp.int32)
    csum = jnp.cumsum(onehot, axis=0)
    rank = jnp.sum(csum * onehot, axis=1) - 1
    counts = csum[-1]
    padded = (counts + MOE_ROWS - 1) // MOE_ROWS * MOE_ROWS
    pend = jnp.cumsum(padded)
    dest = ((pend - padded)[flat_e] + rank).astype(jnp.int32)
    n_tiles = -(-n_assign // MOE_ROWS) + n_exp
    tile_e = jnp.minimum(jnp.searchsorted(pend, jnp.arange(n_tiles, dtype=jnp.int32) * MOE_ROWS,
                                          side='right'), n_exp - 1).astype(jnp.int32)
    n_used = (pend[-1:] // MOE_ROWS).astype(jnp.int32)
    src = jnp.zeros((n_tiles * MOE_ROWS,), jnp.int32).at[dest].set(
        jnp.arange(n_assign, dtype=jnp.int32) // MOE_TOP_K)
    y = _experts(h2[src], tile_e, n_used, w_up_p, b_up_p, w_dn, b_dn)
    y_tk = y[dest].reshape(n_tok, MOE_TOP_K, d)
    return jnp.einsum('tkd,tk->td', y_tk, gate)


def _rope_tables(pos, hd):
    half = hd // 2
    inv = ROPE_THETA ** (-jnp.arange(half, dtype=F32) / half)
    ang = pos.astype(F32)[:, None] * inv[None, :]
    cos, sin = jnp.cos(ang), jnp.sin(ang)
    return jnp.concatenate([cos, cos], axis=-1), jnp.concatenate([-sin, sin], axis=-1)


def kernel(x_prompt, x_sample, cache_k, cache_v, state_gla, page_table, norm1_g, w_in, q_norm_g,
           k_norm_g, w_gate_up, b_gate_up, gla_norm_g, w_branch_a, w_branch_b, w_merge_gate, w_out,
           norm2_g, w_router, b_router, w_up, b_up, w_down, b_down):
    depth = norm1_g.shape[0]
    assert depth == 1, "sample/prompt streams are only chained through one layer here"
    layer = 0
    n_p, t_p, d = x_prompt.shape
    n_s, t_s, _ = x_sample.shape
    heads, page, hd = cache_k.shape[2:]
    b_heads, dk, dv = state_gla.shape[2:]
    bk, bv = b_heads * dk, b_heads * dv
    rank = w_gate_up.shape[1]
    past_len = page_table.shape[1] * page
    ppb = MOBA_BLOCK // page
    n_full = past_len // MOBA_BLOCK
    assert past_len % MOBA_BLOCK == 0 and n_full >= MOBA_TOPK and rank <= LANES
    d_in = w_in.shape[2]
    assert d_in == 3 * heads * hd + 2 * bk + 2 * bv + rank

    w_in_p = jnp.pad(w_in[layer], ((0, 0), (0, LANES - rank))).astype(BF16)
    w_gu_p = jnp.pad(w_gate_up[layer], ((0, LANES - rank), (0, 0))).astype(BF16)
    g1 = norm1_g[layer][None, :]
    g2 = norm2_g[layer][None, :]
    qg = q_norm_g[layer][None, :]
    kg = k_norm_g[layer][None, :]
    bgu = b_gate_up[layer][None, :]
    gn = gla_norm_g[layer][None, :]
    w_a = w_branch_a[layer].astype(BF16)
    w_b = w_branch_b[layer].astype(BF16)
    w_mg = w_merge_gate[layer].astype(BF16)
    w_o = w_out[layer].astype(BF16)
    w_r = w_router[layer]
    b_r = b_router[layer][None, :]
    w_up_p = jnp.concatenate([w_up[layer][:, :, 0::2], w_up[layer][:, :, 1::2]], axis=-1).astype(BF16)
    b_up_p = jnp.concatenate([b_up[layer][:, 0::2], b_up[layer][:, 1::2]], axis=-1)[:, None, :]
    w_dn = w_down[layer].astype(BF16)
    b_dn = b_down[layer][:, None, :]
    proj = functools.partial(_project, norm1_g=g1, w_in_p=w_in_p, q_norm_g=qg, k_norm_g=kg,
                             w_gu_p=w_gu_p, b_gu=bgu, heads=heads, hd=hd, bk=bk, bv=bv)
    merge = functools.partial(_merge, norm1_g=g1, w_a=w_a, w_b=w_b, w_mg=w_mg, w_o=w_o,
                              norm2_g=g2, w_router=w_r, b_router=b_r)

    cos_p, sin_p = _rope_tables(jnp.arange(t_p), hd)
    qa, ka, va, qb, kb, vb, la, gb = proj(x_prompt, cos_p, sin_p)
    o_a = _moba_prompt(qa, ka, va)
    o_b, st_p = _gla_prompt(qb, kb, vb, la, gb, gn, heads=b_heads, dk=dk, dv=dv)
    s_prompt = jnp.swapaxes(st_p, 2, 3)
    x2_p, h2_p, idx_p, gate_p = merge(x_prompt.reshape(n_p * t_p, d), o_a.reshape(n_p * t_p, -1),
                                      o_b.reshape(n_p * t_p, -1))

    n_tok_s = n_s * t_s
    cos_s, sin_s = _rope_tables(past_len + jnp.arange(t_s), hd)
    qa_s, ka_s, va_s, qb_s, kb_s, vb_s, la_s, gb_s = proj(
        x_sample.reshape(1, n_tok_s, d), jnp.tile(cos_s, (n_s, 1)), jnp.tile(sin_s, (n_s, 1)))
    per_seq = lambda a: a.reshape(heads, n_s, t_s, hd).transpose(1, 0, 2, 3)
    qa_s, ka_s, va_s = per_seq(qa_s), per_seq(ka_s), per_seq(va_s)
    kmean = _sample_kmean(cache_k, layer, page_table, n_full, ppb).transpose(0, 2, 1, 3)
    blk_idx = _sample_pick(qa_s, kmean, MOBA_TOPK)
    lpages = blk_idx[..., None] * ppb + jnp.arange(ppb, dtype=jnp.int32)
    phys = jnp.take_along_axis(page_table[:, None, None, :],
                               lpages.reshape(n_s, heads, t_s, MOBA_TOPK * ppb), axis=-1)
    o_a_s = _sample_attn(qa_s, ka_s, va_s, cache_k, cache_v, layer, phys)
    o_a_s = o_a_s.transpose(0, 2, 1, 3).reshape(n_tok_s, heads * hd).astype(BF16)
    tr = lambda a: a.reshape(n_s, t_s, b_heads, dk).transpose(0, 2, 3, 1)
    o_b_s, s_sample = _gla_sample(tr(qb_s), tr(kb_s), tr(la_s), vb_s.reshape(n_s, t_s, bv),
                                  gb_s.reshape(n_s, t_s, bv), gn, state_gla, layer, dk=dk, dv=dv)
    x2_s, h2_s, idx_s, gate_s = merge(x_sample.reshape(n_tok_s, d), o_a_s, o_b_s.reshape(n_tok_s, bv))

    moe = _moe(jnp.concatenate([h2_p, h2_s]), jnp.concatenate([idx_p, idx_s]),
               jnp.concatenate([gate_p, gate_s]), w_up_p, b_up_p, w_dn, b_dn)
    n_tok_p = n_p * t_p
    y_prompt = (x2_p + moe[:n_tok_p]).reshape(n_p, t_p, d)
    y_sample = (x2_s + moe[n_tok_p:]).reshape(n_s, t_s, d)
    return (y_prompt, y_sample, ka[None], va[None], ka_s[None], va_s[None],
            s_prompt[None], s_sample[None])
```

```python
import functools

import jax
import jax.numpy as jnp
from jax import lax
from jax.experimental import pallas as pl
from jax.experimental.pallas import tpu as pltpu

F32 = jnp.float32
BF16 = jnp.bfloat16
HIGHEST = lax.Precision.HIGHEST

MOBA_BLOCK = 256
MOBA_TOPK = 3
ROPE_THETA = 10000.0
GLA_GATE_NORM = 16.0
NORM_EPS = 1e-6
MOE_TOP_K = 4
SWIGLU_ALPHA = 1.702
SWIGLU_LIMIT = 7.0

LANES = 128
VMEM_LIMIT_BYTES = 56 * 1024 * 1024

MASK_VALUE = -1e30
PROJ_ROWS = 512
GLA_CHUNK = 64
GLA_SUB = 16
GLA_STEP_ROWS = 512
MOE_ROWS = 512
MOBA_GROUP = 4
SAMPLE_KMEAN_PAGES = 16


def _cparams(*sem):
    return pltpu.CompilerParams(dimension_semantics=sem, vmem_limit_bytes=VMEM_LIMIT_BYTES)


def _rms(x):
    return x * lax.rsqrt(jnp.mean(x * x, axis=-1, keepdims=True) + NORM_EPS)


def _dot(a, b):
    return jnp.dot(a, b, preferred_element_type=F32)


def _dot_nt(a, b, precision=None):
    return lax.dot_general(a, b, (((1,), (1,)), ((), ())), precision=precision,
                           preferred_element_type=F32)


def _dot_tn(a, b):
    return lax.dot_general(a, b, (((0,), (0,)), ((), ())), preferred_element_type=F32)


def _top_mask(g, pos, n_pick, n_pos, axis):
    sel = jnp.zeros(g.shape, F32)
    picks = []
    for _ in range(n_pick):
        mx = jnp.max(g, axis=axis, keepdims=True)
        cand = jnp.where((g == mx) & (mx > -jnp.inf), pos, n_pos)
        idx = jnp.min(cand, axis=axis, keepdims=True)
        pick = pos == idx
        sel = jnp.where(pick, 1.0, sel)
        g = jnp.where(pick, -jnp.inf, g)
        picks.append((idx, mx))
    return sel, picks


def _proj_kernel(x_ref, g1_ref, wqkv_ref, wrest_ref, qg_ref, kg_ref, wgu_ref, bgu_ref, cos_ref,
                 sin_ref, qt_ref, kt_ref, vt_ref, qb_ref, kb_ref, vb_ref, la_ref, gb_ref,
                 *, heads, hd, bk, bv):
    x = x_ref[0]
    hb = (_rms(x) * g1_ref[...]).astype(BF16)
    cos = cos_ref[...]
    sin = sin_ref[...]
    aqk = heads * hd

    def mm_t(lo, hi):
        return _dot_nt(wqkv_ref[lo:hi, :], hb)

    def norm_rope_t(z, g_ref, out_ref):
        for h in range(heads):
            y = z[h * hd:(h + 1) * hd, :]
            y = y * lax.rsqrt(jnp.mean(y * y, axis=0, keepdims=True) + NORM_EPS) * g_ref[...]
            rot = jnp.concatenate([y[hd // 2:], y[:hd // 2]], axis=0)
            out_ref[0, h] = y * cos + rot * sin

    norm_rope_t(mm_t(0, aqk), qg_ref, qt_ref)
    norm_rope_t(mm_t(aqk, 2 * aqk), kg_ref, kt_ref)
    zv = mm_t(2 * aqk, 3 * aqk)
    for h in range(heads):
        vt_ref[0, h] = zv[h * hd:(h + 1) * hd, :]

    def mm(lo, hi):
        return _dot(hb, wrest_ref[:, lo:hi])

    qb_ref[0] = mm(0, bk)
    kb_ref[0] = mm(bk, 2 * bk)
    vb_ref[0] = mm(2 * bk, 2 * bk + bv).astype(BF16)
    gb_ref[0] = mm(2 * bk + bv, 2 * bk + 2 * bv).astype(BF16)
    lr = mm(2 * bk + 2 * bv, 2 * bk + 2 * bv + LANES)
    u = _dot(lr.astype(BF16), wgu_ref[...]) + bgu_ref[...]
    log_sig = jnp.minimum(u, 0.0) - jnp.log(1.0 + jnp.exp(-jnp.abs(u)))
    la_ref[0] = log_sig / GLA_GATE_NORM


def _project(x, cos_t, sin_t, norm1_g, w_qkv_t, w_rest, q_norm_g, k_norm_g, w_gu_p, b_gu,
             *, heads, hd, bk, bv):
    n, t, d = x.shape
    tm = min(PROJ_ROWS, t)
    assert t % tm == 0
    full = lambda a: pl.BlockSpec(a.shape, lambda b, i: (0,) * a.ndim)
    head_out = pl.BlockSpec((1, heads, hd, tm), lambda b, i: (b, 0, 0, i))
    row_out = lambda w: pl.BlockSpec((1, tm, w), lambda b, i: (b, i, 0))
    table = pl.BlockSpec((hd, tm), lambda b, i: (0, i))
    head_shape = jax.ShapeDtypeStruct((n, heads, hd, t), F32)
    return pl.pallas_call(
        functools.partial(_proj_kernel, heads=heads, hd=hd, bk=bk, bv=bv),
        grid=(n, t // tm),
        in_specs=[pl.BlockSpec((1, tm, d), lambda b, i: (b, i, 0)),
                  full(norm1_g), full(w_qkv_t), full(w_rest), full(q_norm_g), full(k_norm_g),
                  full(w_gu_p), full(b_gu), table, table],
        out_specs=[head_out, head_out, head_out, row_out(bk), row_out(bk), row_out(bv),
                   row_out(bk), row_out(bv)],
        out_shape=[head_shape, head_shape, head_shape,
                   jax.ShapeDtypeStruct((n, t, bk), F32),
                   jax.ShapeDtypeStruct((n, t, bk), F32),
                   jax.ShapeDtypeStruct((n, t, bv), BF16),
                   jax.ShapeDtypeStruct((n, t, bk), F32),
                   jax.ShapeDtypeStruct((n, t, bv), BF16)],
        compiler_params=_cparams("parallel", "parallel"),
        name="proj",
    )(x, norm1_g, w_qkv_t, w_rest, q_norm_g, k_norm_g, w_gu_p, b_gu, cos_t, sin_t)


def _moba_kernel(qt_ref, kt_ref, vt_ref, o_ref, k_sc, vt_sc, kmean_sc, qs_sc, sel_sc, m_sc, l_sc,
                 acc_sc, *, nb, blk, hd, hp, grp):
    c = pl.program_id(2)
    w = hp * hd
    nq = hp * blk

    @pl.when(c == 0)
    def _():
        def stage(j, carry):
            off = pl.multiple_of(j * blk, blk)
            k_nat = kt_ref[0, :, :, pl.ds(off, blk)].reshape(w, blk).T
            kmean_sc[pl.ds(j, 1), :] = jnp.mean(k_nat, axis=0, keepdims=True)
            k_sc[pl.ds(off, blk), :] = k_nat.astype(BF16)
            vt_sc[:, pl.ds(off, blk)] = vt_ref[0, :, :, pl.ds(off, blk)].reshape(w, blk).astype(BF16)
            return carry

        lax.fori_loop(0, nb, stage, 0)

    zero = jnp.zeros((hd, blk), F32)
    qbd = jnp.concatenate(
        [jnp.concatenate([qt_ref[0, i] if j == i else zero for j in range(hp)], axis=1)
         for i in range(hp)], axis=0)
    blk_row = lax.broadcasted_iota(jnp.int32, (nb, nq), 0)
    gate = jnp.dot(kmean_sc[...], qbd, precision=HIGHEST, preferred_element_type=F32)
    sel, _ = _top_mask(jnp.where(blk_row < c, gate, -jnp.inf), blk_row, min(MOBA_TOPK, nb - 1),
                       nb, axis=0)
    sel_sc[...] = sel
    qs_sc[...] = (qbd * hd ** -0.5).astype(BF16)

    def attend(off, n_blk, keeps, first):
        s = _dot(k_sc[pl.ds(off, n_blk * blk), :], qs_sc[...])
        s = jnp.concatenate([jnp.where(keep, s[r * blk:(r + 1) * blk], MASK_VALUE)
                             for r, keep in enumerate(keeps)], axis=0)
        m_blk = jnp.max(s, axis=0, keepdims=True)
        m_new = m_blk if first else jnp.maximum(m_sc[...], m_blk)
        p = jnp.exp(s - m_new)
        p_sum = jnp.sum(p, axis=0, keepdims=True)
        pb = p.astype(BF16)
        if not first:
            alpha = jnp.exp(m_sc[...] - m_new)
            p_sum = alpha * l_sc[...] + p_sum
        l_sc[...] = p_sum
        m_sc[...] = m_new
        for i in range(hp):
            cols = slice(i * blk, (i + 1) * blk)
            pv = _dot(vt_sc[i * hd:(i + 1) * hd, pl.ds(off, n_blk * blk)], pb[:, cols])
            acc_sc[:, cols] = pv if first else alpha[:, cols] * acc_sc[:, cols] + pv

    key_i = lax.broadcasted_iota(jnp.int32, (blk, nq), 0)
    qry_i = lax.broadcasted_iota(jnp.int32, (blk, nq), 1) % blk
    attend(pl.multiple_of(c * blk, blk), 1, [key_i <= qry_i], True)

    def body(g, carry):
        keeps = [sel_sc[pl.ds(g * grp + r, 1), :] > 0.0 for r in range(grp)]
        attend(pl.multiple_of(g * (grp * blk), grp * blk), grp, keeps, False)
        return carry

    lax.fori_loop(0, (c + grp - 1) // grp, body, 0)
    out_t = acc_sc[...] / l_sc[...]
    out_t = jnp.concatenate([out_t[:, i * blk:(i + 1) * blk] for i in range(hp)], axis=0)
    o_ref[0] = out_t.T.astype(BF16)


def _moba_prompt(qt, kt, vt):
    n, heads, hd, t = qt.shape
    blk = MOBA_BLOCK
    assert t % blk == 0 and LANES % hd == 0
    hp = LANES // hd
    assert heads % hp == 0
    nb = t // blk
    grp = MOBA_GROUP if nb % MOBA_GROUP == 0 else 1
    kv_spec = pl.BlockSpec((1, hp, hd, t), lambda b, g, c: (b, g, 0, 0))
    return pl.pallas_call(
        functools.partial(_moba_kernel, nb=nb, blk=blk, hd=hd, hp=hp, grp=grp),
        grid=(n, heads // hp, nb),
        in_specs=[pl.BlockSpec((1, hp, hd, blk), lambda b, g, c: (b, g, 0, c)), kv_spec, kv_spec],
        out_specs=pl.BlockSpec((1, blk, hp * hd), lambda b, g, c: (b, c, g)),
        out_shape=jax.ShapeDtypeStruct((n, t, heads * hd), BF16),
        scratch_shapes=[pltpu.VMEM((t, hp * hd), BF16), pltpu.VMEM((hp * hd, t), BF16),
                        pltpu.VMEM((nb, hp * hd), F32), pltpu.VMEM((hp * hd, hp * blk), BF16),
                        pltpu.VMEM((nb, hp * blk), F32), pltpu.VMEM((1, hp * blk), F32),
                        pltpu.VMEM((1, hp * blk), F32), pltpu.VMEM((hd, hp * blk), F32)],
        compiler_params=_cparams("parallel", "parallel", "arbitrary"),
        name="moba_prompt",
    )(qt, kt, vt)


def _kmean_pages_kernel(pt_ref, *refs, n_pg, ppb, rows):
    del pt_ref
    out_ref = refs[n_pg]
    for b in range(n_pg // ppb):
        acc = refs[b * ppb][0, 0]
        for j in range(1, ppb):
            acc = acc + refs[b * ppb + j][0, 0]
        out_ref[0, b] = jnp.sum(acc, axis=-1) / rows


def _sample_kmean(cache_kt, layer, page_table, n_full, ppb):
    _, _, heads, hd, page = cache_kt.shape
    n, n_pages = page_table.shape
    n_pg = SAMPLE_KMEAN_PAGES
    assert (n_full * ppb) % n_pg == 0 and n_pg % ppb == 0

    def page_map(b, g, pt, j):
        return (layer, pt[b * n_pages + g * n_pg + j], 0, 0, 0)

    specs = [pl.BlockSpec((1, 1, heads, hd, page), functools.partial(page_map, j=j))
             for j in range(n_pg)]
    return pl.pallas_call(
        functools.partial(_kmean_pages_kernel, n_pg=n_pg, ppb=ppb, rows=float(ppb * page)),
        grid_spec=pltpu.PrefetchScalarGridSpec(
            num_scalar_prefetch=1, grid=(n, n_full * ppb // n_pg), in_specs=specs,
            out_specs=pl.BlockSpec((1, n_pg // ppb, heads, hd), lambda b, g, pt: (b, g, 0, 0))),
        out_shape=jax.ShapeDtypeStruct((n, n_full, heads, hd), F32),
        compiler_params=_cparams("parallel", "arbitrary"),
        name="sample_kmean",
    )(page_table.reshape(-1), *([cache_kt] * n_pg))


def _sample_pick_kernel(q_ref, km_ref, idx_ref, *, heads, n_full, n_sel):
    col = lax.broadcasted_iota(jnp.int32, (q_ref.shape[2], n_full), 1)
    for h in range(heads):
        gate = _dot_nt(q_ref[0, h], km_ref[0, h], precision=HIGHEST)
        _, picks = _top_mask(gate, col, n_sel, n_full, axis=1)
        idx_ref[0, h] = jnp.concatenate([p[0] for p in picks], axis=1)


def _sample_pick(q, kmean, n_sel):
    n, heads, t, hd = q.shape
    n_full = kmean.shape[2]
    return pl.pallas_call(
        functools.partial(_sample_pick_kernel, heads=heads, n_full=n_full, n_sel=n_sel),
        grid=(n,),
        in_specs=[pl.BlockSpec((1, heads, t, hd), lambda b: (b, 0, 0, 0)),
                  pl.BlockSpec((1, heads, n_full, hd), lambda b: (b, 0, 0, 0))],
        out_specs=pl.BlockSpec((1, heads, t, n_sel), lambda b: (b, 0, 0, 0)),
        out_shape=jax.ShapeDtypeStruct((n, heads, t, n_sel), jnp.int32),
        compiler_params=_cparams("parallel"),
        name="sample_pick",
    )(q, kmean)


def _sample_attn_kernel(ph_ref, q_ref, kn_ref, vn_ref, *refs, n_blk):
    del ph_ref
    k_refs, v_refs, o_ref = refs[:n_blk], refs[n_blk:2 * n_blk], refs[2 * n_blk]
    t = pl.program_id(2)
    tq, hd = q_ref.shape[2], q_ref.shape[3]
    qs = (q_ref[0, 0] * hd ** -0.5).astype(BF16)
    kt = jnp.concatenate([r[0, 0, 0] for r in k_refs], axis=1).astype(BF16)
    vt = jnp.concatenate([r[0, 0, 0] for r in v_refs], axis=1).astype(BF16)
    s_sel = _dot(qs, kt)
    s_own = _dot_nt(qs, kn_ref[0, 0].astype(BF16))
    own_ok = (lax.broadcasted_iota(jnp.int32, (tq, tq), 1)
              <= lax.broadcasted_iota(jnp.int32, (tq, tq), 0))
    s_own = jnp.where(own_ok, s_own, MASK_VALUE)
    m = jnp.maximum(jnp.max(s_sel, axis=1, keepdims=True), jnp.max(s_own, axis=1, keepdims=True))
    p_sel = jnp.exp(s_sel - m)
    p_own = jnp.exp(s_own - m)
    denom = jnp.sum(p_sel, axis=1, keepdims=True) + jnp.sum(p_own, axis=1, keepdims=True)
    o = (_dot_nt(p_sel.astype(BF16), vt) + _dot(p_own.astype(BF16), vn_ref[0, 0].astype(BF16))) / denom
    row = lax.broadcasted_iota(jnp.int32, (tq, hd), 0)
    o_ref[0, 0, pl.ds(t, 1), :] = jnp.sum(jnp.where(row == t, o, 0.0), axis=0, keepdims=True)


def _sample_attn(q, k_new, v_new, cache_kt, cache_vt, layer, phys):
    n, heads, t, hd = q.shape
    page = cache_kt.shape[4]
    n_blk = phys.shape[-1]

    def page_map(b, h, tt, ph, j):
        return (layer, ph[((b * heads + h) * t + tt) * n_blk + j], h, 0, 0)

    pages = [pl.BlockSpec((1, 1, 1, hd, page), functools.partial(page_map, j=j))
             for j in range(n_blk)]
    own = pl.BlockSpec((1, 1, t, hd), lambda b, h, tt, ph: (b, h, 0, 0))
    return pl.pallas_call(
        functools.partial(_sample_attn_kernel, n_blk=n_blk),
        grid_spec=pltpu.PrefetchScalarGridSpec(
            num_scalar_prefetch=1, grid=(n, heads, t),
            in_specs=[own, own, own] + pages * 2, out_specs=own),
        out_shape=jax.ShapeDtypeStruct((n, heads, t, hd), F32),
        compiler_params=_cparams("parallel", "parallel", "arbitrary"),
        name="sample_attn",
    )(phys.reshape(-1), q, k_new, v_new, *([cache_kt] * n_blk), *([cache_vt] * n_blk))


def _gla_kernel(q_ref, k_ref, v_ref, la_ref, gb_ref, gn_ref, o_ref, sfin_ref, st_sc,
                *, chunk, sub, n_chunk, dk):
    step = pl.program_id(2)

    @pl.when(step == 0)
    def _():
        st_sc[...] = jnp.zeros(st_sc.shape, F32)

    r_i = lax.broadcasted_iota(jnp.int32, (chunk, chunk), 0)
    c_i = lax.broadcasted_iota(jnp.int32, (chunk, chunk), 1)
    tri = r_i >= c_i
    tri_f = tri.astype(F32)
    row_k = lax.broadcasted_iota(jnp.int32, (chunk, dk), 0)
    q_scale = dk ** -0.5

    def one_chunk(ci, carry):
        off = pl.multiple_of(ci * chunk, chunk)
        q = q_ref[0, pl.ds(off, chunk), :] * q_scale
        k = k_ref[0, pl.ds(off, chunk), :]
        v = v_ref[0, pl.ds(off, chunk), :]
        b = jnp.dot(tri_f, la_ref[0, pl.ds(off, chunk), :], precision=HIGHEST,
                    preferred_element_type=F32)
        st = st_sc[...]
        o = _dot_nt((q * jnp.exp(b)).astype(BF16), st.astype(BF16))
        rows = []
        for i in range(chunk // sub):
            lo, hi = i * sub, (i + 1) * sub
            ref = b[lo - 1:lo, :] if i > 0 else jnp.zeros((1, dk), F32)
            qt = (q[lo:hi] * jnp.exp(b[lo:hi] - ref)).astype(BF16)
            kt = (k * jnp.exp(jnp.where(row_k < hi, ref - b, -jnp.inf))).astype(BF16)
            rows.append(_dot_nt(qt, kt))
        attn = jnp.where(tri, jnp.concatenate(rows, axis=0), 0.0)
        o = o + _dot(attn.astype(BF16), v)
        b_last = b[chunk - 1:chunk, :]
        k_dec = (k * jnp.exp(b_last - b)).astype(BF16)
        st_sc[...] = st * jnp.exp(b_last) + _dot_tn(v, k_dec)
        gb = gb_ref[0, pl.ds(off, chunk), :].astype(F32)
        o_ref[0, pl.ds(off, chunk), :] = (_rms(o) * gn_ref[...] * (gb * jax.nn.sigmoid(gb))).astype(BF16)
        return carry

    lax.fori_loop(0, n_chunk, one_chunk, 0)

    @pl.when(step == pl.num_programs(2) - 1)
    def _():
        sfin_ref[0, 0] = st_sc[...]


def _gla_prompt(qb, kb, vb, la, gb, gla_norm_g, *, heads, dk, dv):
    n, t, _ = qb.shape
    rows = min(GLA_STEP_ROWS, t)
    chunk = min(GLA_CHUNK, rows)
    assert t % rows == 0 and rows % chunk == 0 and chunk % GLA_SUB == 0
    kspec = pl.BlockSpec((1, rows, dk), lambda b, h, s: (b, s, h))
    vspec = pl.BlockSpec((1, rows, dv), lambda b, h, s: (b, s, h))
    return pl.pallas_call(
        functools.partial(_gla_kernel, chunk=chunk, sub=GLA_SUB, n_chunk=rows // chunk, dk=dk),
        grid=(n, heads, t // rows),
        in_specs=[kspec, kspec, vspec, kspec, vspec, pl.BlockSpec((1, dv), lambda b, h, s: (0, 0))],
        out_specs=[vspec, pl.BlockSpec((1, 1, dv, dk), lambda b, h, s: (b, h, 0, 0))],
        out_shape=[jax.ShapeDtypeStruct((n, t, heads * dv), BF16),
                   jax.ShapeDtypeStruct((n, heads, dv, dk), F32)],
        scratch_shapes=[pltpu.VMEM((dv, dk), F32)],
        compiler_params=_cparams("parallel", "parallel", "arbitrary"),
        name="gla_prompt",
    )(qb, kb, vb, la, gb, gla_norm_g)


def _gla_sample_kernel(qt_ref, kt_ref, lat_ref, v_ref, gb_ref, gn_ref, s0_ref, o_ref, sfin_ref, *, dk):
    s = s0_ref[0, 0, 0]
    qt = qt_ref[0, 0] * dk ** -0.5
    kt = kt_ref[0, 0]
    at = jnp.exp(lat_ref[0, 0])
    v = v_ref[0].astype(F32)
    outs = []
    for t in range(v.shape[0]):
        s = at[:, t:t + 1] * s + kt[:, t:t + 1] * v[t:t + 1, :]
        outs.append(jnp.sum(qt[:, t:t + 1] * s, axis=0, keepdims=True))
    o = jnp.concatenate(outs, axis=0)
    gb = gb_ref[0].astype(F32)
    o_ref[0] = (_rms(o) * gn_ref[...] * (gb * jax.nn.sigmoid(gb))).astype(BF16)
    sfin_ref[0, 0] = s


def _gla_sample(qt, kt, lat, vb, gb, gla_norm_g, state, layer, *, dk, dv):
    n, heads, _, t = qt.shape
    tspec = pl.BlockSpec((1, 1, dk, t), lambda b, h: (b, h, 0, 0))
    vspec = pl.BlockSpec((1, t, dv), lambda b, h: (b, 0, h))
    return pl.pallas_call(
        functools.partial(_gla_sample_kernel, dk=dk),
        grid=(n, heads),
        in_specs=[tspec, tspec, tspec, vspec, vspec, pl.BlockSpec((1, dv), lambda b, h: (0, 0)),
                  pl.BlockSpec((1, 1, 1, dk, dv), lambda b, h: (layer, b, h, 0, 0))],
        out_specs=[vspec, pl.BlockSpec((1, 1, dk, dv), lambda b, h: (b, h, 0, 0))],
        out_shape=[jax.ShapeDtypeStruct((n, t, heads * dv), BF16),
                   jax.ShapeDtypeStruct((n, heads, dk, dv), F32)],
        compiler_params=_cparams("parallel", "parallel"),
        name="gla_sample",
    )(qt, kt, lat, vb, gb, gla_norm_g, state)


def _merge_kernel(x_ref, oa_ref, ob_ref, g1_ref, wa_ref, wb_ref, wmg_ref, wo_ref, g2_ref, wr_ref,
                  br_ref, x2_ref, h2_ref, idx_ref, gate_ref, *, d, n_exp):
    x = x_ref[...]
    hb = (_rms(x) * g1_ref[...]).astype(BF16)
    y_a = _dot(oa_ref[...], wa_ref[...])
    y_b = _dot(ob_ref[...], wb_ref[...])
    mixed = (jax.nn.sigmoid(_dot(hb, wmg_ref[:, :d])) * y_a
             + jax.nn.sigmoid(_dot(hb, wmg_ref[:, d:])) * y_b)
    x2 = x + _dot(mixed.astype(BF16), wo_ref[...])
    x2_ref[...] = x2
    h2 = _rms(x2) * g2_ref[...]
    h2_ref[...] = h2.astype(BF16)
    logits = jnp.dot(h2, wr_ref[...], precision=HIGHEST, preferred_element_type=F32) + br_ref[...]
    col = lax.broadcasted_iota(jnp.int32, logits.shape, 1)
    _, picks = _top_mask(logits, col, MOE_TOP_K, n_exp, axis=1)
    top_val = jnp.concatenate([p[1] for p in picks], axis=1)
    e = jnp.exp(top_val - top_val[:, :1])
    idx_ref[...] = jnp.concatenate([p[0] for p in picks], axis=1)
    gate_ref[...] = e / jnp.sum(e, axis=1, keepdims=True)


def _merge(x, o_a, o_b, norm1_g, w_a, w_b, w_mg, w_o, norm2_g, w_router, b_router):
    n_tok, d = x.shape
    n_exp = w_router.shape[1]
    tm = min(PROJ_ROWS, n_tok)
    assert n_tok % tm == 0
    full = lambda a: pl.BlockSpec(a.shape, lambda i: (0,) * a.ndim)
    rows = lambda w: pl.BlockSpec((tm, w), lambda i: (i, 0))
    return pl.pallas_call(
        functools.partial(_merge_kernel, d=d, n_exp=n_exp),
        grid=(n_tok // tm,),
        in_specs=[rows(d), rows(o_a.shape[1]), rows(o_b.shape[1]), full(norm1_g), full(w_a),
                  full(w_b), full(w_mg), full(w_o), full(norm2_g), full(w_router), full(b_router)],
        out_specs=[rows(d), rows(d), rows(MOE_TOP_K), rows(MOE_TOP_K)],
        out_shape=[jax.ShapeDtypeStruct((n_tok, d), F32), jax.ShapeDtypeStruct((n_tok, d), BF16),
                   jax.ShapeDtypeStruct((n_tok, MOE_TOP_K), jnp.int32),
                   jax.ShapeDtypeStruct((n_tok, MOE_TOP_K), F32)],
        compiler_params=_cparams("parallel"),
        name="merge",
    )(x, o_a, o_b, norm1_g, w_a, w_b, w_mg, w_o, norm2_g, w_router, b_router)


def _expert_kernel(te_ref, nt_ref, x_ref, wg_ref, wl_ref, bg_ref, bl_ref, wdn_ref, bdn_ref, y_ref):
    del te_ref
    i = pl.program_id(0)

    @pl.when(i < nt_ref[0])
    def _():
        x = x_ref[...]
        x_glu = jnp.minimum(_dot_nt(x, wg_ref[0]) + bg_ref[0], SWIGLU_LIMIT)
        x_lin = jnp.clip(_dot_nt(x, wl_ref[0]) + bl_ref[0], -SWIGLU_LIMIT, SWIGLU_LIMIT)
        a = x_glu * jax.nn.sigmoid(SWIGLU_ALPHA * x_glu) * (x_lin + 1.0)
        y_ref[...] = _dot(a.astype(BF16), wdn_ref[0]) + bdn_ref[0]

    @pl.when(i >= nt_ref[0])
    def _():
        y_ref[...] = jnp.zeros(y_ref.shape, F32)


def _experts(buf, tile_e, n_used, w_glu_t, w_lin_t, b_glu, b_lin, w_dn, b_dn):
    r, d = buf.shape
    d_ff = w_glu_t.shape[1]
    by_expert = lambda *blk: pl.BlockSpec((1,) + blk, lambda i, te, nt: (te[i], 0, 0))
    return pl.pallas_call(
        _expert_kernel,
        grid_spec=pltpu.PrefetchScalarGridSpec(
            num_scalar_prefetch=2, grid=(r // MOE_ROWS,),
            in_specs=[pl.BlockSpec((MOE_ROWS, d), lambda i, te, nt: (i, 0)),
                      by_expert(d_ff, d), by_expert(d_ff, d), by_expert(1, d_ff), by_expert(1, d_ff),
                      by_expert(d_ff, d), by_expert(1, d)],
            out_specs=pl.BlockSpec((MOE_ROWS, d), lambda i, te, nt: (i, 0))),
        out_shape=jax.ShapeDtypeStruct((r, d), F32),
        compiler_params=_cparams("arbitrary"),
        name="experts",
    )(tile_e, n_used, buf, w_glu_t, w_lin_t, b_glu, b_lin, w_dn, b_dn)


def _moe(h2, top_idx, gate, expert_weights):
    n_tok, d = h2.shape
    n_exp = expert_weights[0].shape[0]
    n_assign = n_tok * MOE_TOP_K
    flat_e = top_idx.reshape(-1)
    onehot = (flat_e[:, None] == jnp.arange(n_exp, dtype=jnp.int32)[None, :]).astype(jnp.int32)
    csum = jnp.cumsum(onehot, axis=0)
    rank = jnp.sum(csum * onehot, axis=1) - 1
    counts = csum[-1]
    padded = (counts + MOE_ROWS - 1) // MOE_ROWS * MOE_ROWS
    pend = jnp.cumsum(padded)
    dest = ((pend - padded)[flat_e] + rank).astype(jnp.int32)
    n_tiles = -(-n_assign // MOE_ROWS) + n_exp
    tile_e = jnp.minimum(jnp.searchsorted(pend, jnp.arange(n_tiles, dtype=jnp.int32) * MOE_ROWS,
                                          side='right'), n_exp - 1).astype(jnp.int32)
    n_used = (pend[-1:] // MOE_ROWS).astype(jnp.int32)
    src = jnp.zeros((n_tiles * MOE_ROWS,), jnp.int32).at[dest].set(
        jnp.arange(n_assign, dtype=jnp.int32) // MOE_TOP_K)
    y = _experts(h2[src], tile_e, n_used, *expert_weights)
    y_tk = y[dest].reshape(n_tok, MOE_TOP_K, d)
    return jnp.einsum('tkd,tk->td', y_tk, gate)


def _rope_tables_t(pos, hd):
    half = hd // 2
    inv = ROPE_THETA ** (-jnp.arange(half, dtype=F32) / half)
    ang = inv[:, None] * pos.astype(F32)[None, :]
    cos, sin = jnp.cos(ang), jnp.sin(ang)
    return jnp.concatenate([cos, cos], axis=0), jnp.concatenate([-sin, sin], axis=0)


def kernel(x_prompt, x_sample, cache_k, cache_v, state_gla, page_table, norm1_g, w_in, q_norm_g,
           k_norm_g, w_gate_up, b_gate_up, gla_norm_g, w_branch_a, w_branch_b, w_merge_gate, w_out,
           norm2_g, w_router, b_router, w_up, b_up, w_down, b_down):
    depth = norm1_g.shape[0]
    assert depth == 1, "sample/prompt streams are only chained through one layer here"
    layer = 0
    n_p, t_p, d = x_prompt.shape
    n_s, t_s, _ = x_sample.shape
    heads, page, hd = cache_k.shape[2:]
    b_heads, dk, dv = state_gla.shape[2:]
    bk, bv = b_heads * dk, b_heads * dv
    aqk = heads * hd
    rank = w_gate_up.shape[1]
    past_len = page_table.shape[1] * page
    ppb = MOBA_BLOCK // page
    n_full = past_len // MOBA_BLOCK
    assert past_len % MOBA_BLOCK == 0 and n_full >= MOBA_TOPK and rank <= LANES
    assert w_in.shape[2] == 3 * aqk + 2 * bk + 2 * bv + rank

    w_qkv_t = w_in[layer][:, :3 * aqk].T.astype(BF16)
    w_rest = jnp.pad(w_in[layer][:, 3 * aqk:], ((0, 0), (0, LANES - rank))).astype(BF16)
    w_gu_p = jnp.pad(w_gate_up[layer], ((0, LANES - rank), (0, 0))).astype(BF16)
    g1 = norm1_g[layer][None, :]
    g2 = norm2_g[layer][None, :]
    qg = q_norm_g[layer][:, None]
    kg = k_norm_g[layer][:, None]
    bgu = b_gate_up[layer][None, :]
    gn = gla_norm_g[layer][None, :]
    w_a = w_branch_a[layer].astype(BF16)
    w_b = w_branch_b[layer].astype(BF16)
    w_mg = w_merge_gate[layer].astype(BF16)
    w_o = w_out[layer].astype(BF16)
    w_r = w_router[layer]
    b_r = b_router[layer][None, :]
    w_up_t = jnp.swapaxes(w_up[layer], 1, 2)
    expert_weights = (w_up_t[:, 0::2, :].astype(BF16), w_up_t[:, 1::2, :].astype(BF16),
                      b_up[layer][:, None, 0::2], b_up[layer][:, None, 1::2],
                      w_down[layer].astype(BF16), b_down[layer][:, None, :])
    proj = functools.partial(_project, norm1_g=g1, w_qkv_t=w_qkv_t, w_rest=w_rest, q_norm_g=qg,
                             k_norm_g=kg, w_gu_p=w_gu_p, b_gu=bgu, heads=heads, hd=hd, bk=bk, bv=bv)
    merge = functools.partial(_merge, norm1_g=g1, w_a=w_a, w_b=w_b, w_mg=w_mg, w_o=w_o,
                              norm2_g=g2, w_router=w_r, b_router=b_r)

    cos_p, sin_p = _rope_tables_t(jnp.arange(t_p), hd)
    qt, kt, vt, qb, kb, vb, la, gb = proj(x_prompt, cos_p, sin_p)
    o_a = _moba_prompt(qt, kt, vt)
    o_b, st_p = _gla_prompt(qb, kb, vb, la, gb, gn, heads=b_heads, dk=dk, dv=dv)
    s_prompt = jnp.swapaxes(st_p, 2, 3)
    x2_p, h2_p, idx_p, gate_p = merge(x_prompt.reshape(n_p * t_p, d), o_a.reshape(n_p * t_p, -1),
                                      o_b.reshape(n_p * t_p, -1))

    n_tok_s = n_s * t_s
    cos_s, sin_s = _rope_tables_t(past_len + jnp.arange(t_s), hd)
    qt_s, kt_s, vt_s, qb_s, kb_s, vb_s, la_s, gb_s = proj(
        x_sample.reshape(1, n_tok_s, d), jnp.tile(cos_s, (1, n_s)), jnp.tile(sin_s, (1, n_s)))
    per_seq = lambda a: a.reshape(heads, hd, n_s, t_s).transpose(2, 0, 3, 1)
    qa_s, ka_s, va_s = per_seq(qt_s), per_seq(kt_s), per_seq(vt_s)
    cache_kt = jnp.swapaxes(cache_k, 3, 4)
    cache_vt = jnp.swapaxes(cache_v, 3, 4)
    kmean = _sample_kmean(cache_kt, layer, page_table, n_full, ppb).transpose(0, 2, 1, 3)
    blk_idx = _sample_pick(qa_s, kmean, MOBA_TOPK)
    lpages = blk_idx[..., None] * ppb + jnp.arange(ppb, dtype=jnp.int32)
    phys = jnp.take_along_axis(page_table[:, None, None, :],
                               lpages.reshape(n_s, heads, t_s, MOBA_TOPK * ppb), axis=-1)
    o_a_s = _sample_attn(qa_s, ka_s, va_s, cache_kt, cache_vt, layer, phys)
    o_a_s = o_a_s.transpose(0, 2, 1, 3).reshape(n_tok_s, aqk).astype(BF16)
    tr = lambda a: a.reshape(n_s, t_s, b_heads, dk).transpose(0, 2, 3, 1)
    o_b_s, s_sample = _gla_sample(tr(qb_s), tr(kb_s), tr(la_s), vb_s.reshape(n_s, t_s, bv),
                                  gb_s.reshape(n_s, t_s, bv), gn, state_gla, layer, dk=dk, dv=dv)
    x2_s, h2_s, idx_s, gate_s = merge(x_sample.reshape(n_tok_s, d), o_a_s, o_b_s.reshape(n_tok_s, bv))

    moe = _moe(jnp.concatenate([h2_p, h2_s]), jnp.concatenate([idx_p, idx_s]),
               jnp.concatenate([gate_p, gate_s]), expert_weights)
    n_tok_p = n_p * t_p
    y_prompt = (x2_p + moe[:n_tok_p]).reshape(n_p, t_p, d)
    y_sample = (x2_s + moe[n_tok_p:]).reshape(n_s, t_s, d)
    k_prompt = jnp.swapaxes(kt, 2, 3)
    v_prompt = jnp.swapaxes(vt, 2, 3)
    return (y_prompt, y_sample, k_prompt[None], v_prompt[None], ka_s[None], va_s[None],
            s_prompt[None], s_sample[None])
```

```python
import functools

import jax
import jax.numpy as jnp
from jax import lax
from jax.experimental import pallas as pl
from jax.experimental.pallas import tpu as pltpu

F32 = jnp.float32
BF16 = jnp.bfloat16
HIGHEST = lax.Precision.HIGHEST

MOBA_BLOCK = 256
MOBA_TOPK = 3
ROPE_THETA = 10000.0
GLA_GATE_NORM = 16.0
NORM_EPS = 1e-6
MOE_TOP_K = 4
SWIGLU_ALPHA = 1.702
SWIGLU_LIMIT = 7.0

LANES = 128
VMEM_LIMIT_BYTES = 56 * 1024 * 1024

MASK_VALUE = -1e30
LOG2_E = 1.4426950408889634
PROJ_ROWS = 512
GLA_CHUNK = 64
GLA_SUB = 16
GLA_STEP_ROWS = 512
MOE_ROWS = 512
MOBA_GROUP = 4
SAMPLE_KMEAN_PAGES = 16


def _cparams(*sem):
    return pltpu.CompilerParams(dimension_semantics=sem, vmem_limit_bytes=VMEM_LIMIT_BYTES)


def _rms(x):
    return x * lax.rsqrt(jnp.mean(x * x, axis=-1, keepdims=True) + NORM_EPS)


def _dot(a, b):
    return jnp.dot(a, b, preferred_element_type=F32)


def _dot_nt(a, b, precision=None):
    return lax.dot_general(a, b, (((1,), (1,)), ((), ())), precision=precision,
                           preferred_element_type=F32)


def _dot_tn(a, b):
    return lax.dot_general(a, b, (((0,), (0,)), ((), ())), preferred_element_type=F32)


def _top_mask(g, pos, n_pick, n_pos, axis):
    sel = jnp.zeros(g.shape, F32)
    picks = []
    for _ in range(n_pick):
        mx = jnp.max(g, axis=axis, keepdims=True)
        cand = jnp.where((g == mx) & (mx > -jnp.inf), pos, n_pos)
        idx = jnp.min(cand, axis=axis, keepdims=True)
        pick = pos == idx
        sel = jnp.where(pick, 1.0, sel)
        g = jnp.where(pick, -jnp.inf, g)
        picks.append((idx, mx))
    return sel, picks


def _proj_kernel(x_ref, g1_ref, wqkv_ref, wrest_ref, qg_ref, kg_ref, wgu_ref, bgu_ref, cos_ref,
                 sin_ref, qt_ref, kt_ref, vt_ref, qb_ref, kb_ref, vb_ref, la_ref, gb_ref,
                 *, heads, hd, bk, bv):
    x = x_ref[0]
    hb = (_rms(x) * g1_ref[...]).astype(BF16)
    cos = cos_ref[...]
    sin = sin_ref[...]
    aqk = heads * hd

    def mm_t(lo, hi):
        return _dot_nt(wqkv_ref[lo:hi, :], hb)

    def norm_rope_t(z, g_ref, out_ref):
        for h in range(heads):
            y = z[h * hd:(h + 1) * hd, :]
            y = y * lax.rsqrt(jnp.mean(y * y, axis=0, keepdims=True) + NORM_EPS) * g_ref[...]
            rot = jnp.concatenate([y[hd // 2:], y[:hd // 2]], axis=0)
            out_ref[0, h] = y * cos + rot * sin

    norm_rope_t(mm_t(0, aqk), qg_ref, qt_ref)
    norm_rope_t(mm_t(aqk, 2 * aqk), kg_ref, kt_ref)
    zv = mm_t(2 * aqk, 3 * aqk)
    for h in range(heads):
        vt_ref[0, h] = zv[h * hd:(h + 1) * hd, :]

    def mm(lo, hi):
        return _dot(hb, wrest_ref[:, lo:hi])

    qb_ref[0] = mm(0, bk)
    kb_ref[0] = mm(bk, 2 * bk)
    vb_ref[0] = mm(2 * bk, 2 * bk + bv).astype(BF16)
    gb_ref[0] = mm(2 * bk + bv, 2 * bk + 2 * bv).astype(BF16)
    lr = mm(2 * bk + 2 * bv, 2 * bk + 2 * bv + LANES)
    u = _dot(lr.astype(BF16), wgu_ref[...]) + bgu_ref[...]
    log_sig = jnp.minimum(u, 0.0) - jnp.log(1.0 + jnp.exp(-jnp.abs(u)))
    la_ref[0] = log_sig / GLA_GATE_NORM


def _project(x, cos_t, sin_t, norm1_g, w_qkv_t, w_rest, q_norm_g, k_norm_g, w_gu_p, b_gu,
             *, heads, hd, bk, bv):
    n, t, d = x.shape
    tm = min(PROJ_ROWS, t)
    assert t % tm == 0
    full = lambda a: pl.BlockSpec(a.shape, lambda b, i: (0,) * a.ndim)
    head_out = pl.BlockSpec((1, heads, hd, tm), lambda b, i: (b, 0, 0, i))
    row_out = lambda w: pl.BlockSpec((1, tm, w), lambda b, i: (b, i, 0))
    table = pl.BlockSpec((hd, tm), lambda b, i: (0, i))
    head_shape = jax.ShapeDtypeStruct((n, heads, hd, t), F32)
    return pl.pallas_call(
        functools.partial(_proj_kernel, heads=heads, hd=hd, bk=bk, bv=bv),
        grid=(n, t // tm),
        in_specs=[pl.BlockSpec((1, tm, d), lambda b, i: (b, i, 0)),
                  full(norm1_g), full(w_qkv_t), full(w_rest), full(q_norm_g), full(k_norm_g),
                  full(w_gu_p), full(b_gu), table, table],
        out_specs=[head_out, head_out, head_out, row_out(bk), row_out(bk), row_out(bv),
                   row_out(bk), row_out(bv)],
        out_shape=[head_shape, head_shape, head_shape,
                   jax.ShapeDtypeStruct((n, t, bk), F32),
                   jax.ShapeDtypeStruct((n, t, bk), F32),
                   jax.ShapeDtypeStruct((n, t, bv), BF16),
                   jax.ShapeDtypeStruct((n, t, bk), F32),
                   jax.ShapeDtypeStruct((n, t, bv), BF16)],
        compiler_params=_cparams("parallel", "parallel"),
        name="proj",
    )(x, norm1_g, w_qkv_t, w_rest, q_norm_g, k_norm_g, w_gu_p, b_gu, cos_t, sin_t)


def _moba_kernel(qt_ref, kt_ref, vt_ref, o_ref, k_sc, vt_sc, kmean_sc, qs_sc, sel_sc, m_sc, l_sc,
                 acc_sc, *, nb, blk, hd, hp, grp):
    c = pl.program_id(2)
    w = hp * hd
    nq = hp * blk

    @pl.when(c == 0)
    def _():
        def stage(j, carry):
            off = pl.multiple_of(j * blk, blk)
            k_nat = kt_ref[0, :, :, pl.ds(off, blk)].reshape(w, blk).T
            kmean_sc[pl.ds(j, 1), :] = jnp.mean(k_nat, axis=0, keepdims=True)
            k_sc[pl.ds(off, blk), :] = k_nat.astype(BF16)
            vt_sc[:, pl.ds(off, blk)] = vt_ref[0, :, :, pl.ds(off, blk)].reshape(w, blk).astype(BF16)
            return carry

        lax.fori_loop(0, nb, stage, 0)

    zero = jnp.zeros((hd, blk), F32)
    qbd = jnp.concatenate(
        [jnp.concatenate([qt_ref[0, i] if j == i else zero for j in range(hp)], axis=1)
         for i in range(hp)], axis=0)
    blk_row = lax.broadcasted_iota(jnp.int32, (nb, nq), 0)
    gate = jnp.dot(kmean_sc[...], qbd, precision=HIGHEST, preferred_element_type=F32)
    sel, _ = _top_mask(jnp.where(blk_row < c, gate, -jnp.inf), blk_row, min(MOBA_TOPK, nb - 1),
                       nb, axis=0)
    sel_sc[...] = sel
    qs_sc[...] = (qbd * (hd ** -0.5 * LOG2_E)).astype(BF16)

    def attend(off, n_blk, keeps, first):
        s = _dot(k_sc[pl.ds(off, n_blk * blk), :], qs_sc[...])
        s = jnp.concatenate([jnp.where(keep, s[r * blk:(r + 1) * blk], MASK_VALUE)
                             for r, keep in enumerate(keeps)], axis=0)
        m_blk = jnp.max(s, axis=0, keepdims=True)
        m_new = m_blk if first else jnp.maximum(m_sc[...], m_blk)
        p = jnp.exp2(s - m_new)
        p_sum = jnp.sum(p, axis=0, keepdims=True)
        pb = p.astype(BF16)
        if not first:
            alpha = jnp.exp2(m_sc[...] - m_new)
            p_sum = alpha * l_sc[...] + p_sum
        l_sc[...] = p_sum
        m_sc[...] = m_new
        for i in range(hp):
            cols = slice(i * blk, (i + 1) * blk)
            pv = _dot(vt_sc[i * hd:(i + 1) * hd, pl.ds(off, n_blk * blk)], pb[:, cols])
            acc_sc[:, cols] = pv if first else alpha[:, cols] * acc_sc[:, cols] + pv

    key_i = lax.broadcasted_iota(jnp.int32, (blk, nq), 0)
    qry_i = lax.broadcasted_iota(jnp.int32, (blk, nq), 1) % blk
    attend(pl.multiple_of(c * blk, blk), 1, [key_i <= qry_i], True)

    def body(g, carry):
        keeps = [sel_sc[pl.ds(g * grp + r, 1), :] > 0.0 for r in range(grp)]
        attend(pl.multiple_of(g * (grp * blk), grp * blk), grp, keeps, False)
        return carry

    lax.fori_loop(0, (c + grp - 1) // grp, body, 0)
    out_t = acc_sc[...] / l_sc[...]
    out_t = jnp.concatenate([out_t[:, i * blk:(i + 1) * blk] for i in range(hp)], axis=0)
    o_ref[0] = out_t.T.astype(BF16)


def _moba_prompt(qt, kt, vt):
    n, heads, hd, t = qt.shape
    blk = MOBA_BLOCK
    assert t % blk == 0 and LANES % hd == 0
    hp = LANES // hd
    assert heads % hp == 0
    nb = t // blk
    grp = MOBA_GROUP if nb % MOBA_GROUP == 0 else 1
    kv_spec = pl.BlockSpec((1, hp, hd, t), lambda b, g, c: (b, g, 0, 0))
    return pl.pallas_call(
        functools.partial(_moba_kernel, nb=nb, blk=blk, hd=hd, hp=hp, grp=grp),
        grid=(n, heads // hp, nb),
        in_specs=[pl.BlockSpec((1, hp, hd, blk), lambda b, g, c: (b, g, 0, c)), kv_spec, kv_spec],
        out_specs=pl.BlockSpec((1, blk, hp * hd), lambda b, g, c: (b, c, g)),
        out_shape=jax.ShapeDtypeStruct((n, t, heads * hd), BF16),
        scratch_shapes=[pltpu.VMEM((t, hp * hd), BF16), pltpu.VMEM((hp * hd, t), BF16),
                        pltpu.VMEM((nb, hp * hd), F32), pltpu.VMEM((hp * hd, hp * blk), BF16),
                        pltpu.VMEM((nb, hp * blk), F32), pltpu.VMEM((1, hp * blk), F32),
                        pltpu.VMEM((1, hp * blk), F32), pltpu.VMEM((hd, hp * blk), F32)],
        compiler_params=_cparams("parallel", "parallel", "arbitrary"),
        name="moba_prompt",
    )(qt, kt, vt)


def _kmean_pages_kernel(pt_ref, *refs, n_pg, ppb, rows):
    del pt_ref
    out_ref = refs[n_pg]
    for b in range(n_pg // ppb):
        acc = refs[b * ppb][0, 0]
        for j in range(1, ppb):
            acc = acc + refs[b * ppb + j][0, 0]
        out_ref[0, b] = jnp.sum(acc, axis=-1) / rows


def _sample_kmean(cache_kt, layer, page_table, n_full, ppb):
    _, _, heads, hd, page = cache_kt.shape
    n, n_pages = page_table.shape
    n_pg = SAMPLE_KMEAN_PAGES
    assert (n_full * ppb) % n_pg == 0 and n_pg % ppb == 0

    def page_map(b, g, pt, j):
        return (layer, pt[b * n_pages + g * n_pg + j], 0, 0, 0)

    specs = [pl.BlockSpec((1, 1, heads, hd, page), functools.partial(page_map, j=j))
             for j in range(n_pg)]
    return pl.pallas_call(
        functools.partial(_kmean_pages_kernel, n_pg=n_pg, ppb=ppb, rows=float(ppb * page)),
        grid_spec=pltpu.PrefetchScalarGridSpec(
            num_scalar_prefetch=1, grid=(n, n_full * ppb // n_pg), in_specs=specs,
            out_specs=pl.BlockSpec((1, n_pg // ppb, heads, hd), lambda b, g, pt: (b, g, 0, 0))),
        out_shape=jax.ShapeDtypeStruct((n, n_full, heads, hd), F32),
        compiler_params=_cparams("parallel", "arbitrary"),
        name="sample_kmean",
    )(page_table.reshape(-1), *([cache_kt] * n_pg))


def _sample_pick_kernel(q_ref, km_ref, idx_ref, *, heads, n_full, n_sel):
    col = lax.broadcasted_iota(jnp.int32, (q_ref.shape[2], n_full), 1)
    for h in range(heads):
        gate = _dot_nt(q_ref[0, h], km_ref[0, h], precision=HIGHEST)
        _, picks = _top_mask(gate, col, n_sel, n_full, axis=1)
        idx_ref[0, h] = jnp.concatenate([p[0] for p in picks], axis=1)


def _sample_pick(q, kmean, n_sel):
    n, heads, t, hd = q.shape
    n_full = kmean.shape[2]
    return pl.pallas_call(
        functools.partial(_sample_pick_kernel, heads=heads, n_full=n_full, n_sel=n_sel),
        grid=(n,),
        in_specs=[pl.BlockSpec((1, heads, t, hd), lambda b: (b, 0, 0, 0)),
                  pl.BlockSpec((1, heads, n_full, hd), lambda b: (b, 0, 0, 0))],
        out_specs=pl.BlockSpec((1, heads, t, n_sel), lambda b: (b, 0, 0, 0)),
        out_shape=jax.ShapeDtypeStruct((n, heads, t, n_sel), jnp.int32),
        compiler_params=_cparams("parallel"),
        name="sample_pick",
    )(q, kmean)


def _sample_attn_kernel(ph_ref, q_ref, kn_ref, vn_ref, *refs, n_blk):
    del ph_ref
    k_refs, v_refs, o_ref = refs[:n_blk], refs[n_blk:2 * n_blk], refs[2 * n_blk]
    t = pl.program_id(2)
    tq, hd = q_ref.shape[2], q_ref.shape[3]
    qs = (q_ref[0, 0] * hd ** -0.5).astype(BF16)
    kt = jnp.concatenate([r[0, 0, 0] for r in k_refs], axis=1).astype(BF16)
    vt = jnp.concatenate([r[0, 0, 0] for r in v_refs], axis=1).astype(BF16)
    s_sel = _dot(qs, kt)
    s_own = _dot_nt(qs, kn_ref[0, 0].astype(BF16))
    own_ok = (lax.broadcasted_iota(jnp.int32, (tq, tq), 1)
              <= lax.broadcasted_iota(jnp.int32, (tq, tq), 0))
    s_own = jnp.where(own_ok, s_own, MASK_VALUE)
    m = jnp.maximum(jnp.max(s_sel, axis=1, keepdims=True), jnp.max(s_own, axis=1, keepdims=True))
    p_sel = jnp.exp(s_sel - m)
    p_own = jnp.exp(s_own - m)
    denom = jnp.sum(p_sel, axis=1, keepdims=True) + jnp.sum(p_own, axis=1, keepdims=True)
    o = (_dot_nt(p_sel.astype(BF16), vt) + _dot(p_own.astype(BF16), vn_ref[0, 0].astype(BF16))) / denom
    row = lax.broadcasted_iota(jnp.int32, (tq, hd), 0)
    o_ref[0, 0, pl.ds(t, 1), :] = jnp.sum(jnp.where(row == t, o, 0.0), axis=0, keepdims=True)


def _sample_attn(q, k_new, v_new, cache_kt, cache_vt, layer, phys):
    n, heads, t, hd = q.shape
    page = cache_kt.shape[4]
    n_blk = phys.shape[-1]

    def page_map(b, h, tt, ph, j):
        return (layer, ph[((b * heads + h) * t + tt) * n_blk + j], h, 0, 0)

    pages = [pl.BlockSpec((1, 1, 1, hd, page), functools.partial(page_map, j=j))
             for j in range(n_blk)]
    own = pl.BlockSpec((1, 1, t, hd), lambda b, h, tt, ph: (b, h, 0, 0))
    return pl.pallas_call(
        functools.partial(_sample_attn_kernel, n_blk=n_blk),
        grid_spec=pltpu.PrefetchScalarGridSpec(
            num_scalar_prefetch=1, grid=(n, heads, t),
            in_specs=[own, own, own] + pages * 2, out_specs=own),
        out_shape=jax.ShapeDtypeStruct((n, heads, t, hd), F32),
        compiler_params=_cparams("parallel", "parallel", "arbitrary"),
        name="sample_attn",
    )(phys.reshape(-1), q, k_new, v_new, *([cache_kt] * n_blk), *([cache_vt] * n_blk))


def _gla_kernel(q_ref, k_ref, v_ref, la_ref, gb_ref, gn_ref, o_ref, sfin_ref, st_sc,
                *, chunk, sub, n_chunk, heads, dk, dv):
    step = pl.program_id(1)

    @pl.when(step == 0)
    def _():
        st_sc[...] = jnp.zeros(st_sc.shape, F32)

    r_i = lax.broadcasted_iota(jnp.int32, (chunk, chunk), 0)
    c_i = lax.broadcasted_iota(jnp.int32, (chunk, chunk), 1)
    tri = r_i >= c_i
    tri_f = tri.astype(F32)
    row_k = lax.broadcasted_iota(jnp.int32, (chunk, dk), 0)
    q_scale = dk ** -0.5

    def one_chunk(ci, carry):
        off = pl.multiple_of(ci * chunk, chunk)
        rows_ds = pl.ds(off, chunk)
        b_all = jnp.dot(tri_f, la_ref[0, rows_ds, :], precision=HIGHEST,
                        preferred_element_type=F32)
        for h in range(heads):
            kc = slice(h * dk, (h + 1) * dk)
            vc = slice(h * dv, (h + 1) * dv)
            q = q_ref[0, rows_ds, kc] * q_scale
            k = k_ref[0, rows_ds, kc]
            v = v_ref[0, rows_ds, vc]
            b = b_all[:, kc]
            st = st_sc[h]
            o = _dot_nt((q * jnp.exp(b)).astype(BF16), st.astype(BF16))
            rows = []
            for i in range(chunk // sub):
                lo, hi = i * sub, (i + 1) * sub
                ref = b[lo - 1:lo, :] if i > 0 else jnp.zeros((1, dk), F32)
                qt = (q[lo:hi] * jnp.exp(b[lo:hi] - ref)).astype(BF16)
                kt = (k * jnp.exp(jnp.where(row_k < hi, ref - b, -jnp.inf))).astype(BF16)
                rows.append(_dot_nt(qt, kt))
            attn = jnp.where(tri, jnp.concatenate(rows, axis=0), 0.0)
            o = o + _dot(attn.astype(BF16), v)
            b_last = b[chunk - 1:chunk, :]
            k_dec = (k * jnp.exp(b_last - b)).astype(BF16)
            st_sc[h] = st * jnp.exp(b_last) + _dot_tn(v, k_dec)
            gb = gb_ref[0, rows_ds, vc].astype(F32)
            o_ref[0, rows_ds, vc] = (_rms(o) * gn_ref[...] * (gb * jax.nn.sigmoid(gb))).astype(BF16)
        return carry

    lax.fori_loop(0, n_chunk, one_chunk, 0)

    @pl.when(step == pl.num_programs(1) - 1)
    def _():
        sfin_ref[0] = st_sc[...]


def _gla_prompt(qb, kb, vb, la, gb, gla_norm_g, *, heads, dk, dv):
    n, t, _ = qb.shape
    rows = min(GLA_STEP_ROWS, t)
    chunk = min(GLA_CHUNK, rows)
    assert t % rows == 0 and rows % chunk == 0 and chunk % GLA_SUB == 0
    kspec = pl.BlockSpec((1, rows, heads * dk), lambda b, s: (b, s, 0))
    vspec = pl.BlockSpec((1, rows, heads * dv), lambda b, s: (b, s, 0))
    return pl.pallas_call(
        functools.partial(_gla_kernel, chunk=chunk, sub=GLA_SUB, n_chunk=rows // chunk,
                          heads=heads, dk=dk, dv=dv),
        grid=(n, t // rows),
        in_specs=[kspec, kspec, vspec, kspec, vspec, pl.BlockSpec((1, dv), lambda b, s: (0, 0))],
        out_specs=[vspec, pl.BlockSpec((1, heads, dv, dk), lambda b, s: (b, 0, 0, 0))],
        out_shape=[jax.ShapeDtypeStruct((n, t, heads * dv), BF16),
                   jax.ShapeDtypeStruct((n, heads, dv, dk), F32)],
        scratch_shapes=[pltpu.VMEM((heads, dv, dk), F32)],
        compiler_params=_cparams("parallel", "arbitrary"),
        name="gla_prompt",
    )(qb, kb, vb, la, gb, gla_norm_g)


def _gla_sample_kernel(qt_ref, kt_ref, lat_ref, v_ref, gb_ref, gn_ref, s0_ref, o_ref, sfin_ref, *, dk):
    s = s0_ref[0, 0, 0]
    qt = qt_ref[0, 0] * dk ** -0.5
    kt = kt_ref[0, 0]
    at = jnp.exp(lat_ref[0, 0])
    v = v_ref[0].astype(F32)
    outs = []
    for t in range(v.shape[0]):
        s = at[:, t:t + 1] * s + kt[:, t:t + 1] * v[t:t + 1, :]
        outs.append(jnp.sum(qt[:, t:t + 1] * s, axis=0, keepdims=True))
    o = jnp.concatenate(outs, axis=0)
    gb = gb_ref[0].astype(F32)
    o_ref[0] = (_rms(o) * gn_ref[...] * (gb * jax.nn.sigmoid(gb))).astype(BF16)
    sfin_ref[0, 0] = s


def _gla_sample(qt, kt, lat, vb, gb, gla_norm_g, state, layer, *, dk, dv):
    n, heads, _, t = qt.shape
    tspec = pl.BlockSpec((1, 1, dk, t), lambda b, h: (b, h, 0, 0))
    vspec = pl.BlockSpec((1, t, dv), lambda b, h: (b, 0, h))
    return pl.pallas_call(
        functools.partial(_gla_sample_kernel, dk=dk),
        grid=(n, heads),
        in_specs=[tspec, tspec, tspec, vspec, vspec, pl.BlockSpec((1, dv), lambda b, h: (0, 0)),
                  pl.BlockSpec((1, 1, 1, dk, dv), lambda b, h: (layer, b, h, 0, 0))],
        out_specs=[vspec, pl.BlockSpec((1, 1, dk, dv), lambda b, h: (b, h, 0, 0))],
        out_shape=[jax.ShapeDtypeStruct((n, t, heads * dv), BF16),
                   jax.ShapeDtypeStruct((n, heads, dk, dv), F32)],
        compiler_params=_cparams("parallel", "parallel"),
        name="gla_sample",
    )(qt, kt, lat, vb, gb, gla_norm_g, state)


def _merge_kernel(x_ref, oa_ref, ob_ref, g1_ref, wa_ref, wb_ref, wmg_ref, wo_ref, g2_ref, wr_ref,
                  br_ref, x2_ref, h2_ref, idx_ref, gate_ref, *, d, n_exp):
    x = x_ref[...]
    hb = (_rms(x) * g1_ref[...]).astype(BF16)
    y_a = _dot(oa_ref[...], wa_ref[...])
    y_b = _dot(ob_ref[...], wb_ref[...])
    mixed = (jax.nn.sigmoid(_dot(hb, wmg_ref[:, :d])) * y_a
             + jax.nn.sigmoid(_dot(hb, wmg_ref[:, d:])) * y_b)
    x2 = x + _dot(mixed.astype(BF16), wo_ref[...])
    x2_ref[...] = x2
    h2 = _rms(x2) * g2_ref[...]
    h2_ref[...] = h2.astype(BF16)
    logits = jnp.dot(h2, wr_ref[...], precision=HIGHEST, preferred_element_type=F32) + br_ref[...]
    col = lax.broadcasted_iota(jnp.int32, logits.shape, 1)
    _, picks = _top_mask(logits, col, MOE_TOP_K, n_exp, axis=1)
    top_val = jnp.concatenate([p[1] for p in picks], axis=1)
    e = jnp.exp(top_val - top_val[:, :1])
    idx_ref[...] = jnp.concatenate([p[0] for p in picks], axis=1)
    gate_ref[...] = e / jnp.sum(e, axis=1, keepdims=True)


def _merge(x, o_a, o_b, norm1_g, w_a, w_b, w_mg, w_o, norm2_g, w_router, b_router):
    n_tok, d = x.shape
    n_exp = w_router.shape[1]
    tm = min(PROJ_ROWS, n_tok)
    assert n_tok % tm == 0
    full = lambda a: pl.BlockSpec(a.shape, lambda i: (0,) * a.ndim)
    rows = lambda w: pl.BlockSpec((tm, w), lambda i: (i, 0))
    return pl.pallas_call(
        functools.partial(_merge_kernel, d=d, n_exp=n_exp),
        grid=(n_tok // tm,),
        in_specs=[rows(d), rows(o_a.shape[1]), rows(o_b.shape[1]), full(norm1_g), full(w_a),
                  full(w_b), full(w_mg), full(w_o), full(norm2_g), full(w_router), full(b_router)],
        out_specs=[rows(d), rows(d), rows(MOE_TOP_K), rows(MOE_TOP_K)],
        out_shape=[jax.ShapeDtypeStruct((n_tok, d), F32), jax.ShapeDtypeStruct((n_tok, d), BF16),
                   jax.ShapeDtypeStruct((n_tok, MOE_TOP_K), jnp.int32),
                   jax.ShapeDtypeStruct((n_tok, MOE_TOP_K), F32)],
        compiler_params=_cparams("parallel"),
        name="merge",
    )(x, o_a, o_b, norm1_g, w_a, w_b, w_mg, w_o, norm2_g, w_router, b_router)


def _expert_kernel(te_ref, nt_ref, x_ref, wup_ref, bg_ref, bl_ref, wdn_ref, bdn_ref, y_ref,
                   wg_sc, wl_sc, *, split_rows):
    i = pl.program_id(0)

    @pl.when((i == 0) | (te_ref[i] != te_ref[jnp.maximum(i - 1, 0)]))
    def _():
        for r in range(0, wg_sc.shape[0], split_rows):
            w32 = pltpu.bitcast(wup_ref[0, 2 * r:2 * (r + split_rows), :], jnp.uint32)
            wg_sc[r:r + split_rows, :] = pltpu.bitcast(w32 << 16, F32).astype(BF16)
            wl_sc[r:r + split_rows, :] = pltpu.bitcast(w32 & jnp.uint32(0xFFFF0000), F32).astype(BF16)

    @pl.when(i < nt_ref[0])
    def _():
        x = x_ref[...]
        x_glu = jnp.minimum(_dot_nt(x, wg_sc[...]) + bg_ref[0], SWIGLU_LIMIT)
        x_lin = jnp.clip(_dot_nt(x, wl_sc[...]) + bl_ref[0], -SWIGLU_LIMIT, SWIGLU_LIMIT)
        a = x_glu * jax.nn.sigmoid(SWIGLU_ALPHA * x_glu) * (x_lin + 1.0)
        y_ref[...] = (_dot(a.astype(BF16), wdn_ref[0]) + bdn_ref[0]).astype(y_ref.dtype)

    @pl.when(i >= nt_ref[0])
    def _():
        y_ref[...] = jnp.zeros(y_ref.shape, y_ref.dtype)


def _experts(buf, tile_e, n_used, w_up_t, b_glu, b_lin, w_dn, b_dn):
    r, d = buf.shape
    d_ff = w_dn.shape[1]
    by_expert = lambda *blk: pl.BlockSpec((1,) + blk, lambda i, te, nt: (te[i], 0, 0))
    return pl.pallas_call(
        functools.partial(_expert_kernel, split_rows=min(256, d_ff)),
        grid_spec=pltpu.PrefetchScalarGridSpec(
            num_scalar_prefetch=2, grid=(r // MOE_ROWS,),
            in_specs=[pl.BlockSpec((MOE_ROWS, d), lambda i, te, nt: (i, 0)),
                      by_expert(2 * d_ff, d), by_expert(1, d_ff), by_expert(1, d_ff),
                      by_expert(d_ff, d), by_expert(1, d)],
            out_specs=pl.BlockSpec((MOE_ROWS, d), lambda i, te, nt: (i, 0)),
            scratch_shapes=[pltpu.VMEM((d_ff, d), BF16), pltpu.VMEM((d_ff, d), BF16)]),
        out_shape=jax.ShapeDtypeStruct((r, d), BF16),
        compiler_params=_cparams("arbitrary"),
        name="experts",
    )(tile_e, n_used, buf, w_up_t, b_glu, b_lin, w_dn, b_dn)


def _moe(h2, top_idx, gate, expert_weights):
    n_tok, d = h2.shape
    n_exp = expert_weights[0].shape[0]
    n_assign = n_tok * MOE_TOP_K
    flat_e = top_idx.reshape(-1)
    onehot = (flat_e[:, None] == jnp.arange(n_exp, dtype=jnp.int32)[None, :]).astype(jnp.int32)
    csum = jnp.cumsum(onehot, axis=0)
    rank = jnp.sum(csum * onehot, axis=1) - 1
    counts = csum[-1]
    padded = (counts + MOE_ROWS - 1) // MOE_ROWS * MOE_ROWS
    pend = jnp.cumsum(padded)
    dest = ((pend - padded)[flat_e] + rank).astype(jnp.int32)
    n_tiles = -(-n_assign // MOE_ROWS) + n_exp
    tile_e = jnp.minimum(jnp.searchsorted(pend, jnp.arange(n_tiles, dtype=jnp.int32) * MOE_ROWS,
                                          side='right'), n_exp - 1).astype(jnp.int32)
    n_used = (pend[-1:] // MOE_ROWS).astype(jnp.int32)
    src = jnp.zeros((n_tiles * MOE_ROWS,), jnp.int32).at[dest].set(
        jnp.arange(n_assign, dtype=jnp.int32) // MOE_TOP_K)
    y = _experts(h2[src], tile_e, n_used, *expert_weights)
    y_kt = y[dest.reshape(n_tok, MOE_TOP_K).T]
    return jnp.sum(y_kt.astype(F32) * gate.T[:, :, None], axis=0)


def _rope_tables_t(pos, hd):
    half = hd // 2
    inv = ROPE_THETA ** (-jnp.arange(half, dtype=F32) / half)
    ang = inv[:, None] * pos.astype(F32)[None, :]
    cos, sin = jnp.cos(ang), jnp.sin(ang)
    return jnp.concatenate([cos, cos], axis=0), jnp.concatenate([-sin, sin], axis=0)


def kernel(x_prompt, x_sample, cache_k, cache_v, state_gla, page_table, norm1_g, w_in, q_norm_g,
           k_norm_g, w_gate_up, b_gate_up, gla_norm_g, w_branch_a, w_branch_b, w_merge_gate, w_out,
           norm2_g, w_router, b_router, w_up, b_up, w_down, b_down):
    depth = norm1_g.shape[0]
    assert depth == 1, "sample/prompt streams are only chained through one layer here"
    layer = 0
    n_p, t_p, d = x_prompt.shape
    n_s, t_s, _ = x_sample.shape
    heads, page, hd = cache_k.shape[2:]
    b_heads, dk, dv = state_gla.shape[2:]
    bk, bv = b_heads * dk, b_heads * dv
    aqk = heads * hd
    rank = w_gate_up.shape[1]
    past_len = page_table.shape[1] * page
    ppb = MOBA_BLOCK // page
    n_full = past_len // MOBA_BLOCK
    assert past_len % MOBA_BLOCK == 0 and n_full >= MOBA_TOPK and rank <= LANES
    assert w_in.shape[2] == 3 * aqk + 2 * bk + 2 * bv + rank

    w_qkv_t = w_in[layer][:, :3 * aqk].T.astype(BF16)
    w_rest = jnp.pad(w_in[layer][:, 3 * aqk:], ((0, 0), (0, LANES - rank))).astype(BF16)
    w_gu_p = jnp.pad(w_gate_up[layer], ((0, LANES - rank), (0, 0))).astype(BF16)
    g1 = norm1_g[layer][None, :]
    g2 = norm2_g[layer][None, :]
    qg = q_norm_g[layer][:, None]
    kg = k_norm_g[layer][:, None]
    bgu = b_gate_up[layer][None, :]
    gn = gla_norm_g[layer][None, :]
    w_a = w_branch_a[layer].astype(BF16)
    w_b = w_branch_b[layer].astype(BF16)
    w_mg = w_merge_gate[layer].astype(BF16)
    w_o = w_out[layer].astype(BF16)
    w_r = w_router[layer]
    b_r = b_router[layer][None, :]
    expert_weights = (jnp.swapaxes(w_up[layer], 1, 2).astype(BF16),
                      b_up[layer][:, None, 0::2], b_up[layer][:, None, 1::2],
                      w_down[layer].astype(BF16), b_down[layer][:, None, :])
    proj = functools.partial(_project, norm1_g=g1, w_qkv_t=w_qkv_t, w_rest=w_rest, q_norm_g=qg,
                             k_norm_g=kg, w_gu_p=w_gu_p, b_gu=bgu, heads=heads, hd=hd, bk=bk, bv=bv)
    merge = functools.partial(_merge, norm1_g=g1, w_a=w_a, w_b=w_b, w_mg=w_mg, w_o=w_o,
                              norm2_g=g2, w_router=w_r, b_router=b_r)

    cos_p, sin_p = _rope_tables_t(jnp.arange(t_p), hd)
    qt, kt, vt, qb, kb, vb, la, gb = proj(x_prompt, cos_p, sin_p)
    o_a = _moba_prompt(qt, kt, vt)
    o_b, st_p = _gla_prompt(qb, kb, vb, la, gb, gn, heads=b_heads, dk=dk, dv=dv)
    s_prompt = jnp.swapaxes(st_p, 2, 3)
    x2_p, h2_p, idx_p, gate_p = merge(x_prompt.reshape(n_p * t_p, d), o_a.reshape(n_p * t_p, -1),
                                      o_b.reshape(n_p * t_p, -1))

    n_tok_s = n_s * t_s
    cos_s, sin_s = _rope_tables_t(past_len + jnp.arange(t_s), hd)
    qt_s, kt_s, vt_s, qb_s, kb_s, vb_s, la_s, gb_s = proj(
        x_sample.reshape(1, n_tok_s, d), jnp.tile(cos_s, (1, n_s)), jnp.tile(sin_s, (1, n_s)))
    per_seq = lambda a: a.reshape(heads, hd, n_s, t_s).transpose(2, 0, 3, 1)
    qa_s, ka_s, va_s = per_seq(qt_s), per_seq(kt_s), per_seq(vt_s)
    cache_kt = jnp.swapaxes(cache_k, 3, 4)
    cache_vt = jnp.swapaxes(cache_v, 3, 4)
    kmean = _sample_kmean(cache_kt, layer, page_table, n_full, ppb).transpose(0, 2, 1, 3)
    blk_idx = _sample_pick(qa_s, kmean, MOBA_TOPK)
    lpages = blk_idx[..., None] * ppb + jnp.arange(ppb, dtype=jnp.int32)
    phys = jnp.take_along_axis(page_table[:, None, None, :],
                               lpages.reshape(n_s, heads, t_s, MOBA_TOPK * ppb), axis=-1)
    o_a_s = _sample_attn(qa_s, ka_s, va_s, cache_kt, cache_vt, layer, phys)
    o_a_s = o_a_s.transpose(0, 2, 1, 3).reshape(n_tok_s, aqk).astype(BF16)
    tr = lambda a: a.reshape(n_s, t_s, b_heads, dk).transpose(0, 2, 3, 1)
    o_b_s, s_sample = _gla_sample(tr(qb_s), tr(kb_s), tr(la_s), vb_s.reshape(n_s, t_s, bv),
                                  gb_s.reshape(n_s, t_s, bv), gn, state_gla, layer, dk=dk, dv=dv)
    x2_s, h2_s, idx_s, gate_s = merge(x_sample.reshape(n_tok_s, d), o_a_s, o_b_s.reshape(n_tok_s, bv))

    moe = _moe(jnp.concatenate([h2_p, h2_s]), jnp.concatenate([idx_p, idx_s]),
               jnp.concatenate([gate_p, gate_s]), expert_weights)
    n_tok_p = n_p * t_p
    y_prompt = (x2_p + moe[:n_tok_p]).reshape(n_p, t_p, d)
    y_sample = (x2_s + moe[n_tok_p:]).reshape(n_s, t_s, d)
    k_prompt = jnp.swapaxes(kt, 2, 3)
    v_prompt = jnp.swapaxes(vt, 2, 3)
    return (y_prompt, y_sample, k_prompt[None], v_prompt[None], ka_s[None], va_s[None],
            s_prompt[None], s_sample[None])
```

```python
import functools

import jax
import jax.numpy as jnp
from jax import lax
from jax.experimental import pallas as pl
from jax.experimental.pallas import tpu as pltpu

F32 = jnp.float32
BF16 = jnp.bfloat16
HIGHEST = lax.Precision.HIGHEST

MOBA_BLOCK = 256
MOBA_TOPK = 3
ROPE_THETA = 10000.0
GLA_GATE_NORM = 16.0
NORM_EPS = 1e-6
MOE_TOP_K = 4
SWIGLU_ALPHA = 1.702
SWIGLU_LIMIT = 7.0

LANES = 128
PAD_ROWS = 16
VMEM_LIMIT_BYTES = 56 * 1024 * 1024

MASK_VALUE = -1e30
LOG2_E = 1.4426950408889634
PROJ_ROWS = 512
GLA_CHUNK = 64
GLA_SUB = 16
GLA_STEP_ROWS = 512
MOE_ROWS = 512
MOBA_GROUP = 4
SAMPLE_KMEAN_PAGES = 16


def _cparams(*sem):
    return pltpu.CompilerParams(dimension_semantics=sem, vmem_limit_bytes=VMEM_LIMIT_BYTES)


def _rms(x):
    return x * lax.rsqrt(jnp.mean(x * x, axis=-1, keepdims=True) + NORM_EPS)


def _dot(a, b):
    return jnp.dot(a, b, preferred_element_type=F32)


def _dot_nt(a, b, precision=None):
    return lax.dot_general(a, b, (((1,), (1,)), ((), ())), precision=precision,
                           preferred_element_type=F32)


def _dot_tn(a, b):
    return lax.dot_general(a, b, (((0,), (0,)), ((), ())), preferred_element_type=F32)


def _top_mask(g, pos, n_pick, n_pos, axis):
    sel = jnp.zeros(g.shape, F32)
    picks = []
    for _ in range(n_pick):
        mx = jnp.max(g, axis=axis, keepdims=True)
        cand = jnp.where((g == mx) & (mx > -jnp.inf), pos, n_pos)
        idx = jnp.min(cand, axis=axis, keepdims=True)
        pick = pos == idx
        sel = jnp.where(pick, 1.0, sel)
        g = jnp.where(pick, -jnp.inf, g)
        picks.append((idx, mx))
    return sel, picks


def _proj_kernel(x_ref, g1_ref, wqkv_ref, wrest_ref, qg_ref, kg_ref, wgu_ref, bgu_ref, cos_ref,
                 sin_ref, qt_ref, kt_ref, vt_ref, qb_ref, kb_ref, vb_ref, la_ref, gb_ref,
                 *, heads, hd, bk, bv):
    x = x_ref[0]
    hb = (_rms(x) * g1_ref[...]).astype(BF16)
    cos = cos_ref[...]
    sin = sin_ref[...]
    aqk = heads * hd

    def mm_t(lo, hi):
        return _dot_nt(wqkv_ref[lo:hi, :], hb)

    def norm_rope_t(z, g_ref, out_ref):
        for h in range(heads):
            y = z[h * hd:(h + 1) * hd, :]
            y = y * lax.rsqrt(jnp.mean(y * y, axis=0, keepdims=True) + NORM_EPS) * g_ref[...]
            rot = jnp.concatenate([y[hd // 2:], y[:hd // 2]], axis=0)
            out_ref[0, h] = y * cos + rot * sin

    norm_rope_t(mm_t(0, aqk), qg_ref, qt_ref)
    norm_rope_t(mm_t(aqk, 2 * aqk), kg_ref, kt_ref)
    zv = mm_t(2 * aqk, 3 * aqk)
    for h in range(heads):
        vt_ref[0, h] = zv[h * hd:(h + 1) * hd, :]

    def mm(lo, hi):
        return _dot(hb, wrest_ref[:, lo:hi])

    qb_ref[0] = mm(0, bk)
    kb_ref[0] = mm(bk, 2 * bk)
    vb_ref[0] = mm(2 * bk, 2 * bk + bv).astype(BF16)
    gb_ref[0] = mm(2 * bk + bv, 2 * bk + 2 * bv).astype(BF16)
    lr = mm(2 * bk + 2 * bv, 2 * bk + 2 * bv + LANES)
    u = _dot(lr.astype(BF16), wgu_ref[...]) + bgu_ref[...]
    log_sig = jnp.minimum(u, 0.0) - jnp.log(1.0 + jnp.exp(-jnp.abs(u)))
    la_ref[0] = log_sig / GLA_GATE_NORM


def _project(x, cos_t, sin_t, norm1_g, w_qkv_t, w_rest, q_norm_g, k_norm_g, w_gu_p, b_gu,
             *, heads, hd, bk, bv):
    n, t, d = x.shape
    tm = min(PROJ_ROWS, t)
    assert t % tm == 0
    full = lambda a: pl.BlockSpec(a.shape, lambda b, i: (0,) * a.ndim)
    head_out = pl.BlockSpec((1, heads, hd, tm), lambda b, i: (b, 0, 0, i))
    row_out = lambda w: pl.BlockSpec((1, tm, w), lambda b, i: (b, i, 0))
    table = pl.BlockSpec((hd, tm), lambda b, i: (0, i))
    head_shape = jax.ShapeDtypeStruct((n, heads, hd, t), F32)
    return pl.pallas_call(
        functools.partial(_proj_kernel, heads=heads, hd=hd, bk=bk, bv=bv),
        grid=(n, t // tm),
        in_specs=[pl.BlockSpec((1, tm, d), lambda b, i: (b, i, 0)),
                  full(norm1_g), full(w_qkv_t), full(w_rest), full(q_norm_g), full(k_norm_g),
                  full(w_gu_p), full(b_gu), table, table],
        out_specs=[head_out, head_out, head_out, row_out(bk), row_out(bk), row_out(bv),
                   row_out(bk), row_out(bv)],
        out_shape=[head_shape, head_shape, head_shape,
                   jax.ShapeDtypeStruct((n, t, bk), F32),
                   jax.ShapeDtypeStruct((n, t, bk), F32),
                   jax.ShapeDtypeStruct((n, t, bv), BF16),
                   jax.ShapeDtypeStruct((n, t, bk), F32),
                   jax.ShapeDtypeStruct((n, t, bv), BF16)],
        compiler_params=_cparams("parallel", "parallel"),
        name="proj",
    )(x, norm1_g, w_qkv_t, w_rest, q_norm_g, k_norm_g, w_gu_p, b_gu, cos_t, sin_t)


def _moba_kernel(qt_ref, kt_ref, vt_ref, o_ref, k_sc, vt_sc, kmean_sc, qs_sc, bias_sc, m_sc, acc_sc,
                 sa_sc, sb_sc, ma_sc, mb_sc, *, nb, blk, hd, hp, grp):
    c = pl.program_id(2)
    nq = hp * blk
    vr = hd + PAD_ROWS
    n_groups = nb // grp
    cols = [slice(i * blk, (i + 1) * blk) for i in range(hp)]

    @pl.when(c == 0)
    def _():
        row = lax.broadcasted_iota(jnp.int32, (LANES - hd, blk), 0)
        ones_row = (lax.broadcasted_iota(jnp.int32, (PAD_ROWS, blk), 0) == 0).astype(BF16)

        def stage_kv(j, carry):
            off = pl.multiple_of(j * blk, blk)
            place = (row == j % grp).astype(F32)
            for i in range(hp):
                k_nat = jnp.concatenate([kt_ref[0, i, :, pl.ds(off, blk)], place], axis=0).T
                k_sc[i, pl.ds(off, blk), :] = k_nat.astype(BF16)
                kmean_sc[i, pl.ds(j, 1), :] = jnp.mean(k_nat, axis=0, keepdims=True)
                vt_sc[i * vr:i * vr + hd, pl.ds(off, blk)] = vt_ref[0, i, :, pl.ds(off, blk)].astype(BF16)
                vt_sc[i * vr + hd:(i + 1) * vr, pl.ds(off, blk)] = ones_row
            return carry

        lax.fori_loop(0, nb, stage_kv, 0)

    zero_rows = jnp.zeros((LANES - hd, blk), F32)
    q_pad = [jnp.concatenate([qt_ref[0, i], zero_rows], axis=0) for i in range(hp)]
    gate = jnp.concatenate(
        [jnp.dot(kmean_sc[i], q_pad[i], precision=HIGHEST, preferred_element_type=F32)
         for i in range(hp)], axis=1)
    blk_row = lax.broadcasted_iota(jnp.int32, (nb, nq), 0)
    sel, _ = _top_mask(jnp.where(blk_row < c, gate, -jnp.inf), blk_row, min(MOBA_TOPK, nb - 1),
                       nb, axis=0)
    bias = (sel - 1.0) * -MASK_VALUE
    pad = jnp.zeros((PAD_ROWS - grp, blk), F32)
    for g in range(n_groups):
        for i in range(hp):
            bias_sc[g, i] = jnp.concatenate([bias[g * grp:(g + 1) * grp, cols[i]], pad],
                                            axis=0).astype(BF16)
    for i in range(hp):
        qs_sc[i] = (q_pad[i] * (hd ** -0.5 * LOG2_E)).astype(BF16)

    def scores(off, n_keys):
        return jnp.concatenate([_dot(k_sc[i, pl.ds(off, n_keys), :], qs_sc[i]) for i in range(hp)],
                               axis=1)

    def update(s, m_blk, off, n_keys, first):
        m_new = m_blk if first else jnp.maximum(m_sc[...], m_blk)
        pb = jnp.exp2(s - m_new).astype(BF16)
        if not first:
            alpha = jnp.exp2(m_sc[...] - m_new)
        m_sc[...] = m_new
        for i in range(hp):
            pv = _dot(vt_sc[i * vr:(i + 1) * vr, pl.ds(off, n_keys)], pb[:, cols[i]])
            acc_sc[:, cols[i]] = pv if first else alpha[:, cols[i]] * acc_sc[:, cols[i]] + pv

    own_off = pl.multiple_of(c * blk, blk)
    causal = (lax.broadcasted_iota(jnp.int32, (blk, nq), 0)
              <= lax.broadcasted_iota(jnp.int32, (blk, nq), 1) % blk)
    s_own = jnp.where(causal, scores(own_off, blk), MASK_VALUE)
    update(s_own, jnp.max(s_own, axis=0, keepdims=True), own_off, blk, True)

    n_grp = (c + grp - 1) // grp

    def stage(g, s_buf, mx_buf):
        g = jnp.minimum(g, n_groups - 1)
        for i in range(hp):
            qs_sc[i, hd:hd + PAD_ROWS, :] = bias_sc[g, i]
        s = scores(pl.multiple_of(g * (grp * blk), grp * blk), grp * blk)
        s_buf[...] = s
        mx_buf[...] = jnp.max(s, axis=0, keepdims=True)

    def consume(g, s_buf, mx_buf):
        update(s_buf[...], mx_buf[...], pl.multiple_of(g * (grp * blk), grp * blk), grp * blk, False)

    @pl.when(n_grp > 0)
    def _():
        stage(0, sa_sc, ma_sc)

    def body(u, carry):
        stage(2 * u + 1, sb_sc, mb_sc)
        consume(2 * u, sa_sc, ma_sc)
        stage(2 * u + 2, sa_sc, ma_sc)
        consume(2 * u + 1, sb_sc, mb_sc)
        return carry

    lax.fori_loop(0, n_grp // 2, body, 0)

    @pl.when(n_grp % 2 == 1)
    def _():
        consume(n_grp - 1, sa_sc, ma_sc)

    out_t = acc_sc[:hd, :] / acc_sc[hd:hd + 1, :]
    out_t = jnp.concatenate([out_t[:, cols[i]] for i in range(hp)], axis=0)
    o_ref[0] = out_t.T.astype(BF16)


def _moba_prompt(qt, kt, vt):
    n, heads, hd, t = qt.shape
    blk = MOBA_BLOCK
    assert t % blk == 0 and LANES % hd == 0
    hp = LANES // hd
    assert heads % hp == 0
    nb = t // blk
    grp = MOBA_GROUP if nb % MOBA_GROUP == 0 else 1
    nq = hp * blk
    kv_spec = pl.BlockSpec((1, hp, hd, t), lambda b, g, c: (b, g, 0, 0))
    return pl.pallas_call(
        functools.partial(_moba_kernel, nb=nb, blk=blk, hd=hd, hp=hp, grp=grp),
        grid=(n, heads // hp, nb),
        in_specs=[pl.BlockSpec((1, hp, hd, blk), lambda b, g, c: (b, g, 0, c)), kv_spec, kv_spec],
        out_specs=pl.BlockSpec((1, blk, hp * hd), lambda b, g, c: (b, c, g)),
        out_shape=jax.ShapeDtypeStruct((n, t, heads * hd), BF16),
        scratch_shapes=[pltpu.VMEM((hp, t, LANES), BF16),
                        pltpu.VMEM((hp * (hd + PAD_ROWS), t), BF16),
                        pltpu.VMEM((hp, nb, LANES), F32),
                        pltpu.VMEM((hp, LANES, blk), BF16),
                        pltpu.VMEM((nb // grp, hp, PAD_ROWS, blk), BF16),
                        pltpu.VMEM((1, nq), F32),
                        pltpu.VMEM((hd + PAD_ROWS, nq), F32),
                        pltpu.VMEM((grp * blk, nq), F32), pltpu.VMEM((grp * blk, nq), F32),
                        pltpu.VMEM((1, nq), F32), pltpu.VMEM((1, nq), F32)],
        compiler_params=_cparams("parallel", "parallel", "arbitrary"),
        name="moba_prompt",
    )(qt, kt, vt)


def _kmean_pages_kernel(pt_ref, *refs, n_pg, ppb, rows):
    del pt_ref
    out_ref = refs[n_pg]
    for b in range(n_pg // ppb):
        acc = refs[b * ppb][0, 0]
        for j in range(1, ppb):
            acc = acc + refs[b * ppb + j][0, 0]
        out_ref[0, b] = jnp.sum(acc, axis=-1) / rows


def _sample_kmean(cache_kt, layer, page_table, n_full, ppb):
    _, _, heads, hd, page = cache_kt.shape
    n, n_pages = page_table.shape
    n_pg = SAMPLE_KMEAN_PAGES
    assert (n_full * ppb) % n_pg == 0 and n_pg % ppb == 0

    def page_map(b, g, pt, j):
        return (layer, pt[b * n_pages + g * n_pg + j], 0, 0, 0)

    specs = [pl.BlockSpec((1, 1, heads, hd, page), functools.partial(page_map, j=j))
             for j in range(n_pg)]
    return pl.pallas_call(
        functools.partial(_kmean_pages_kernel, n_pg=n_pg, ppb=ppb, rows=float(ppb * page)),
        grid_spec=pltpu.PrefetchScalarGridSpec(
            num_scalar_prefetch=1, grid=(n, n_full * ppb // n_pg), in_specs=specs,
            out_specs=pl.BlockSpec((1, n_pg // ppb, heads, hd), lambda b, g, pt: (b, g, 0, 0))),
        out_shape=jax.ShapeDtypeStruct((n, n_full, heads, hd), F32),
        compiler_params=_cparams("parallel", "arbitrary"),
        name="sample_kmean",
    )(page_table.reshape(-1), *([cache_kt] * n_pg))


def _sample_pick_kernel(q_ref, km_ref, idx_ref, *, heads, n_full, n_sel):
    col = lax.broadcasted_iota(jnp.int32, (q_ref.shape[2], n_full), 1)
    for h in range(heads):
        gate = _dot_nt(q_ref[0, h], km_ref[0, h], precision=HIGHEST)
        _, picks = _top_mask(gate, col, n_sel, n_full, axis=1)
        idx_ref[0, h] = jnp.concatenate([p[0] for p in picks], axis=1)


def _sample_pick(q, kmean, n_sel):
    n, heads, t, hd = q.shape
    n_full = kmean.shape[2]
    return pl.pallas_call(
        functools.partial(_sample_pick_kernel, heads=heads, n_full=n_full, n_sel=n_sel),
        grid=(n,),
        in_specs=[pl.BlockSpec((1, heads, t, hd), lambda b: (b, 0, 0, 0)),
                  pl.BlockSpec((1, heads, n_full, hd), lambda b: (b, 0, 0, 0))],
        out_specs=pl.BlockSpec((1, heads, t, n_sel), lambda b: (b, 0, 0, 0)),
        out_shape=jax.ShapeDtypeStruct((n, heads, t, n_sel), jnp.int32),
        compiler_params=_cparams("parallel"),
        name="sample_pick",
    )(q, kmean)


def _sample_attn_kernel(ph_ref, q_ref, kn_ref, vn_ref, *refs, n_pg):
    del ph_ref
    tq, hd = q_ref.shape[2], q_ref.shape[3]
    k_refs, v_refs, o_ref = refs[:tq * n_pg], refs[tq * n_pg:2 * tq * n_pg], refs[2 * tq * n_pg]
    qs = (q_ref[0, 0] * hd ** -0.5).astype(BF16)
    own_ok = (lax.broadcasted_iota(jnp.int32, (tq, tq), 1)
              <= lax.broadcasted_iota(jnp.int32, (tq, tq), 0))
    s_own = jnp.where(own_ok, _dot_nt(qs, kn_ref[0, 0].astype(BF16)), MASK_VALUE)
    vn = vn_ref[0, 0].astype(BF16)
    for t in range(tq):
        pages = slice(t * n_pg, (t + 1) * n_pg)
        kt = jnp.concatenate([r[0, 0, 0] for r in k_refs[pages]], axis=1).astype(BF16)
        vt = jnp.concatenate([r[0, 0, 0] for r in v_refs[pages]], axis=1).astype(BF16)
        s_sel = _dot(qs, kt)
        m = jnp.maximum(jnp.max(s_sel, axis=1, keepdims=True), jnp.max(s_own, axis=1, keepdims=True))
        p_sel = jnp.exp(s_sel - m)
        p_own = jnp.exp(s_own - m)
        denom = jnp.sum(p_sel, axis=1, keepdims=True) + jnp.sum(p_own, axis=1, keepdims=True)
        o = (_dot_nt(p_sel.astype(BF16), vt) + _dot(p_own.astype(BF16), vn)) / denom
        o_ref[0, 0, t:t + 1, :] = o[t:t + 1]


def _sample_attn(q, k_new, v_new, cache_kt, cache_vt, layer, phys):
    n, heads, t, hd = q.shape
    page = cache_kt.shape[4]
    n_pg = phys.shape[-1]

    def page_map(b, h, ph, j):
        return (layer, ph[(b * heads + h) * (t * n_pg) + j], h, 0, 0)

    pages = [pl.BlockSpec((1, 1, 1, hd, page), functools.partial(page_map, j=j))
             for j in range(t * n_pg)]
    own = pl.BlockSpec((1, 1, t, hd), lambda b, h, ph: (b, h, 0, 0))
    return pl.pallas_call(
        functools.partial(_sample_attn_kernel, n_pg=n_pg),
        grid_spec=pltpu.PrefetchScalarGridSpec(
            num_scalar_prefetch=1, grid=(n, heads),
            in_specs=[own, own, own] + pages * 2, out_specs=own),
        out_shape=jax.ShapeDtypeStruct((n, heads, t, hd), F32),
        compiler_params=_cparams("parallel", "parallel"),
        name="sample_attn",
    )(phys.reshape(-1), q, k_new, v_new, *([cache_kt] * (t * n_pg)), *([cache_vt] * (t * n_pg)))


def _gla_kernel(q_ref, k_ref, v_ref, la_ref, gb_ref, gn_ref, o_ref, sfin_ref, st_sc,
                *, chunk, sub, n_chunk, heads, dk, dv):
    step = pl.program_id(1)

    @pl.when(step == 0)
    def _():
        st_sc[...] = jnp.zeros(st_sc.shape, F32)

    r_i = lax.broadcasted_iota(jnp.int32, (chunk, chunk), 0)
    c_i = lax.broadcasted_iota(jnp.int32, (chunk, chunk), 1)
    tri = r_i >= c_i
    tri_b = tri.astype(BF16)
    row_k = lax.broadcasted_iota(jnp.int32, (chunk, dk), 0)
    q_scale = dk ** -0.5

    def one_chunk(ci, carry):
        off = pl.multiple_of(ci * chunk, chunk)
        rows_ds = pl.ds(off, chunk)
        la = la_ref[0, rows_ds, :]
        la_hi = la.astype(BF16)
        la_mid = (la - la_hi.astype(F32)).astype(BF16)
        la_lo = (la - la_hi.astype(F32) - la_mid.astype(F32)).astype(BF16)
        b_all = _dot(tri_b, la_hi) + _dot(tri_b, la_mid) + _dot(tri_b, la_lo)
        for h in range(heads):
            kc = slice(h * dk, (h + 1) * dk)
            vc = slice(h * dv, (h + 1) * dv)
            q = q_ref[0, rows_ds, kc] * q_scale
            k = k_ref[0, rows_ds, kc]
            v = v_ref[0, rows_ds, vc]
            b = b_all[:, kc]
            st = st_sc[h]
            o = _dot_nt((q * jnp.exp(b)).astype(BF16), st.astype(BF16))
            rows = []
            for i in range(chunk // sub):
                lo, hi = i * sub, (i + 1) * sub
                ref = b[lo - 1:lo, :] if i > 0 else jnp.zeros((1, dk), F32)
                qt = (q[lo:hi] * jnp.exp(b[lo:hi] - ref)).astype(BF16)
                kt = (k * jnp.exp(jnp.where(row_k < hi, ref - b, -jnp.inf))).astype(BF16)
                rows.append(_dot_nt(qt, kt))
            attn = jnp.where(tri, jnp.concatenate(rows, axis=0), 0.0)
            o = o + _dot(attn.astype(BF16), v)
            b_last = b[chunk - 1:chunk, :]
            k_dec = (k * jnp.exp(b_last - b)).astype(BF16)
            st_sc[h] = st * jnp.exp(b_last) + _dot_tn(v, k_dec)
            gb = gb_ref[0, rows_ds, vc].astype(F32)
            o_ref[0, rows_ds, vc] = (_rms(o) * gn_ref[...] * (gb * jax.nn.sigmoid(gb))).astype(BF16)
        return carry

    lax.fori_loop(0, n_chunk, one_chunk, 0)

    @pl.when(step == pl.num_programs(1) - 1)
    def _():
        sfin_ref[0] = st_sc[...]


def _gla_prompt(qb, kb, vb, la, gb, gla_norm_g, *, heads, dk, dv):
    n, t, _ = qb.shape
    rows = min(GLA_STEP_ROWS, t)
    chunk = min(GLA_CHUNK, rows)
    assert t % rows == 0 and rows % chunk == 0 and chunk % GLA_SUB == 0
    kspec = pl.BlockSpec((1, rows, heads * dk), lambda b, s: (b, s, 0))
    vspec = pl.BlockSpec((1, rows, heads * dv), lambda b, s: (b, s, 0))
    return pl.pallas_call(
        functools.partial(_gla_kernel, chunk=chunk, sub=GLA_SUB, n_chunk=rows // chunk,
                          heads=heads, dk=dk, dv=dv),
        grid=(n, t // rows),
        in_specs=[kspec, kspec, vspec, kspec, vspec, pl.BlockSpec((1, dv), lambda b, s: (0, 0))],
        out_specs=[vspec, pl.BlockSpec((1, heads, dv, dk), lambda b, s: (b, 0, 0, 0))],
        out_shape=[jax.ShapeDtypeStruct((n, t, heads * dv), BF16),
                   jax.ShapeDtypeStruct((n, heads, dv, dk), F32)],
        scratch_shapes=[pltpu.VMEM((heads, dv, dk), F32)],
        compiler_params=_cparams("parallel", "arbitrary"),
        name="gla_prompt",
    )(qb, kb, vb, la, gb, gla_norm_g)


def _gla_sample_kernel(qt_ref, kt_ref, lat_ref, v_ref, gb_ref, gn_ref, s0_ref, o_ref, sfin_ref, *, dk):
    s = s0_ref[0, 0, 0]
    qt = qt_ref[0, 0] * dk ** -0.5
    kt = kt_ref[0, 0]
    at = jnp.exp(lat_ref[0, 0])
    v = v_ref[0].astype(F32)
    outs = []
    for t in range(v.shape[0]):
        s = at[:, t:t + 1] * s + kt[:, t:t + 1] * v[t:t + 1, :]
        outs.append(jnp.sum(qt[:, t:t + 1] * s, axis=0, keepdims=True))
    o = jnp.concatenate(outs, axis=0)
    gb = gb_ref[0].astype(F32)
    o_ref[0] = (_rms(o) * gn_ref[...] * (gb * jax.nn.sigmoid(gb))).astype(BF16)
    sfin_ref[0, 0] = s


def _gla_sample(qt, kt, lat, vb, gb, gla_norm_g, state, layer, *, dk, dv):
    n, heads, _, t = qt.shape
    tspec = pl.BlockSpec((1, 1, dk, t), lambda b, h: (b, h, 0, 0))
    vspec = pl.BlockSpec((1, t, dv), lambda b, h: (b, 0, h))
    return pl.pallas_call(
        functools.partial(_gla_sample_kernel, dk=dk),
        grid=(n, heads),
        in_specs=[tspec, tspec, tspec, vspec, vspec, pl.BlockSpec((1, dv), lambda b, h: (0, 0)),
                  pl.BlockSpec((1, 1, 1, dk, dv), lambda b, h: (layer, b, h, 0, 0))],
        out_specs=[vspec, pl.BlockSpec((1, 1, dk, dv), lambda b, h: (b, h, 0, 0))],
        out_shape=[jax.ShapeDtypeStruct((n, t, heads * dv), BF16),
                   jax.ShapeDtypeStruct((n, heads, dk, dv), F32)],
        compiler_params=_cparams("parallel", "parallel"),
        name="gla_sample",
    )(qt, kt, lat, vb, gb, gla_norm_g, state)


def _merge_kernel(x_ref, oa_ref, ob_ref, g1_ref, wa_ref, wb_ref, wmg_ref, wo_ref, g2_ref, wrh_ref,
                  wrl_ref, br_ref, x2_ref, h2_ref, idx_ref, gate_ref, *, d, n_exp):
    x = x_ref[...]
    hb = (_rms(x) * g1_ref[...]).astype(BF16)
    y_a = _dot(oa_ref[...], wa_ref[...])
    y_b = _dot(ob_ref[...], wb_ref[...])
    mixed = (jax.nn.sigmoid(_dot(hb, wmg_ref[:, :d])) * y_a
             + jax.nn.sigmoid(_dot(hb, wmg_ref[:, d:])) * y_b)
    x2 = x + _dot(mixed.astype(BF16), wo_ref[...])
    x2_ref[...] = x2
    h2 = _rms(x2) * g2_ref[...]
    h2_hi = h2.astype(BF16)
    h2_ref[...] = h2_hi
    h2_lo = (h2 - h2_hi.astype(F32)).astype(BF16)
    logits = (_dot(h2_hi, wrh_ref[...]) + _dot(h2_hi, wrl_ref[...]) + _dot(h2_lo, wrh_ref[...])
              + br_ref[...])
    col = lax.broadcasted_iota(jnp.int32, logits.shape, 1)
    _, picks = _top_mask(logits, col, MOE_TOP_K, n_exp, axis=1)
    top_val = jnp.concatenate([p[1] for p in picks], axis=1)
    e = jnp.exp(top_val - top_val[:, :1])
    idx_ref[...] = jnp.concatenate([p[0] for p in picks], axis=1)
    gate_ref[...] = e / jnp.sum(e, axis=1, keepdims=True)


def _merge(x, o_a, o_b, norm1_g, w_a, w_b, w_mg, w_o, norm2_g, w_router, b_router):
    n_tok, d = x.shape
    n_exp = w_router.shape[1]
    w_router_hi = w_router.astype(BF16)
    w_router_lo = (w_router - w_router_hi.astype(F32)).astype(BF16)
    tm = min(PROJ_ROWS, n_tok)
    assert n_tok % tm == 0
    full = lambda a: pl.BlockSpec(a.shape, lambda i: (0,) * a.ndim)
    rows = lambda w: pl.BlockSpec((tm, w), lambda i: (i, 0))
    return pl.pallas_call(
        functools.partial(_merge_kernel, d=d, n_exp=n_exp),
        grid=(n_tok // tm,),
        in_specs=[rows(d), rows(o_a.shape[1]), rows(o_b.shape[1]), full(norm1_g), full(w_a),
                  full(w_b), full(w_mg), full(w_o), full(norm2_g), full(w_router_hi),
                  full(w_router_lo), full(b_router)],
        out_specs=[rows(d), rows(d), rows(MOE_TOP_K), rows(MOE_TOP_K)],
        out_shape=[jax.ShapeDtypeStruct((n_tok, d), F32), jax.ShapeDtypeStruct((n_tok, d), BF16),
                   jax.ShapeDtypeStruct((n_tok, MOE_TOP_K), jnp.int32),
                   jax.ShapeDtypeStruct((n_tok, MOE_TOP_K), F32)],
        compiler_params=_cparams("parallel"),
        name="merge",
    )(x, o_a, o_b, norm1_g, w_a, w_b, w_mg, w_o, norm2_g, w_router_hi, w_router_lo, b_router)


def _expert_kernel(te_ref, nt_ref, x_ref, wup_ref, bg_ref, bl_ref, wdn_ref, bdn_ref, y_ref,
                   wg_sc, wl_sc, *, split_rows):
    i = pl.program_id(0)

    @pl.when((i == 0) | (te_ref[i] != te_ref[jnp.maximum(i - 1, 0)]))
    def _():
        for r in range(0, wg_sc.shape[0], split_rows):
            w32 = pltpu.bitcast(wup_ref[0, 2 * r:2 * (r + split_rows), :], jnp.uint32)
            wg_sc[r:r + split_rows, :] = pltpu.bitcast(w32 << 16, F32).astype(BF16)
            wl_sc[r:r + split_rows, :] = pltpu.bitcast(w32 & jnp.uint32(0xFFFF0000), F32).astype(BF16)

    @pl.when(i < nt_ref[0])
    def _():
        x = x_ref[...]
        x_glu = jnp.minimum(_dot_nt(x, wg_sc[...]) + bg_ref[0], SWIGLU_LIMIT)
        x_lin = jnp.clip(_dot_nt(x, wl_sc[...]) + bl_ref[0], -SWIGLU_LIMIT, SWIGLU_LIMIT)
        a = x_glu * jax.nn.sigmoid(SWIGLU_ALPHA * x_glu) * (x_lin + 1.0)
        y_ref[...] = (_dot(a.astype(BF16), wdn_ref[0]) + bdn_ref[0]).astype(y_ref.dtype)

    @pl.when(i >= nt_ref[0])
    def _():
        y_ref[...] = jnp.zeros(y_ref.shape, y_ref.dtype)


def _experts(buf, tile_e, n_used, w_up_t, b_glu, b_lin, w_dn, b_dn):
    r, d = buf.shape
    d_ff = w_dn.shape[1]
    by_expert = lambda *blk: pl.BlockSpec((1,) + blk, lambda i, te, nt: (te[i], 0, 0))
    return pl.pallas_call(
        functools.partial(_expert_kernel, split_rows=min(256, d_ff)),
        grid_spec=pltpu.PrefetchScalarGridSpec(
            num_scalar_prefetch=2, grid=(r // MOE_ROWS,),
            in_specs=[pl.BlockSpec((MOE_ROWS, d), lambda i, te, nt: (i, 0)),
                      by_expert(2 * d_ff, d), by_expert(1, d_ff), by_expert(1, d_ff),
                      by_expert(d_ff, d), by_expert(1, d)],
            out_specs=pl.BlockSpec((MOE_ROWS, d), lambda i, te, nt: (i, 0)),
            scratch_shapes=[pltpu.VMEM((d_ff, d), BF16), pltpu.VMEM((d_ff, d), BF16)]),
        out_shape=jax.ShapeDtypeStruct((r, d), BF16),
        compiler_params=_cparams("arbitrary"),
        name="experts",
    )(tile_e, n_used, buf, w_up_t, b_glu, b_lin, w_dn, b_dn)


def _moe(h2, top_idx, gate, expert_weights):
    n_tok, d = h2.shape
    n_exp = expert_weights[0].shape[0]
    n_assign = n_tok * MOE_TOP_K
    flat_e = top_idx.reshape(-1)
    onehot = (flat_e[:, None] == jnp.arange(n_exp, dtype=jnp.int32)[None, :]).astype(jnp.int32)
    csum = jnp.cumsum(onehot, axis=0)
    rank = jnp.sum(csum * onehot, axis=1) - 1
    counts = csum[-1]
    padded = (counts + MOE_ROWS - 1) // MOE_ROWS * MOE_ROWS
    pend = jnp.cumsum(padded)
    dest = ((pend - padded)[flat_e] + rank).astype(jnp.int32)
    n_tiles = -(-n_assign // MOE_ROWS) + n_exp
    tile_e = jnp.minimum(jnp.searchsorted(pend, jnp.arange(n_tiles, dtype=jnp.int32) * MOE_ROWS,
                                          side='right'), n_exp - 1).astype(jnp.int32)
    n_used = (pend[-1:] // MOE_ROWS).astype(jnp.int32)
    src = jnp.zeros((n_tiles * MOE_ROWS,), jnp.int32).at[dest].set(
        jnp.arange(n_assign, dtype=jnp.int32) // MOE_TOP_K)
    y = _experts(h2[src], tile_e, n_used, *expert_weights)
    y_kt = y[dest.reshape(n_tok, MOE_TOP_K).T]
    return jnp.sum(y_kt.astype(F32) * gate.T[:, :, None], axis=0)


def _rope_tables_t(pos, hd):
    half = hd // 2
    inv = ROPE_THETA ** (-jnp.arange(half, dtype=F32) / half)
    ang = inv[:, None] * pos.astype(F32)[None, :]
    cos, sin = jnp.cos(ang), jnp.sin(ang)
    return jnp.concatenate([cos, cos], axis=0), jnp.concatenate([-sin, sin], axis=0)


def kernel(x_prompt, x_sample, cache_k, cache_v, state_gla, page_table, norm1_g, w_in, q_norm_g,
           k_norm_g, w_gate_up, b_gate_up, gla_norm_g, w_branch_a, w_branch_b, w_merge_gate, w_out,
           norm2_g, w_router, b_router, w_up, b_up, w_down, b_down):
    depth = norm1_g.shape[0]
    assert depth == 1, "sample/prompt streams are only chained through one layer here"
    layer = 0
    n_p, t_p, d = x_prompt.shape
    n_s, t_s, _ = x_sample.shape
    heads, page, hd = cache_k.shape[2:]
    b_heads, dk, dv = state_gla.shape[2:]
    bk, bv = b_heads * dk, b_heads * dv
    aqk = heads * hd
    rank = w_gate_up.shape[1]
    past_len = page_table.shape[1] * page
    ppb = MOBA_BLOCK // page
    n_full = past_len // MOBA_BLOCK
    assert past_len % MOBA_BLOCK == 0 and n_full >= MOBA_TOPK and rank <= LANES
    assert w_in.shape[2] == 3 * aqk + 2 * bk + 2 * bv + rank

    w_qkv_t = w_in[layer][:, :3 * aqk].T.astype(BF16)
    w_rest = jnp.pad(w_in[layer][:, 3 * aqk:], ((0, 0), (0, LANES - rank))).astype(BF16)
    w_gu_p = jnp.pad(w_gate_up[layer], ((0, LANES - rank), (0, 0))).astype(BF16)
    g1 = norm1_g[layer][None, :]
    g2 = norm2_g[layer][None, :]
    qg = q_norm_g[layer][:, None]
    kg = k_norm_g[layer][:, None]
    bgu = b_gate_up[layer][None, :]
    gn = gla_norm_g[layer][None, :]
    w_a = w_branch_a[layer].astype(BF16)
    w_b = w_branch_b[layer].astype(BF16)
    w_mg = w_merge_gate[layer].astype(BF16)
    w_o = w_out[layer].astype(BF16)
    w_r = w_router[layer]
    b_r = b_router[layer][None, :]
    expert_weights = (jnp.swapaxes(w_up[layer], 1, 2).astype(BF16),
                      b_up[layer][:, None, 0::2], b_up[layer][:, None, 1::2],
                      w_down[layer].astype(BF16), b_down[layer][:, None, :])
    proj = functools.partial(_project, norm1_g=g1, w_qkv_t=w_qkv_t, w_rest=w_rest, q_norm_g=qg,
                             k_norm_g=kg, w_gu_p=w_gu_p, b_gu=bgu, heads=heads, hd=hd, bk=bk, bv=bv)
    merge = functools.partial(_merge, norm1_g=g1, w_a=w_a, w_b=w_b, w_mg=w_mg, w_o=w_o,
                              norm2_g=g2, w_router=w_r, b_router=b_r)

    cos_p, sin_p = _rope_tables_t(jnp.arange(t_p), hd)
    qt, kt, vt, qb, kb, vb, la, gb = proj(x_prompt, cos_p, sin_p)
    o_a = _moba_prompt(qt, kt, vt)
    o_b, st_p = _gla_prompt(qb, kb, vb, la, gb, gn, heads=b_heads, dk=dk, dv=dv)
    s_prompt = jnp.swapaxes(st_p, 2, 3)
    x2_p, h2_p, idx_p, gate_p = merge(x_prompt.reshape(n_p * t_p, d), o_a.reshape(n_p * t_p, -1),
                                      o_b.reshape(n_p * t_p, -1))

    n_tok_s = n_s * t_s
    cos_s, sin_s = _rope_tables_t(past_len + jnp.arange(t_s), hd)
    qt_s, kt_s, vt_s, qb_s, kb_s, vb_s, la_s, gb_s = proj(
        x_sample.reshape(1, n_tok_s, d), jnp.tile(cos_s, (1, n_s)), jnp.tile(sin_s, (1, n_s)))
    per_seq = lambda a: a.reshape(heads, hd, n_s, t_s).transpose(2, 0, 3, 1)
    qa_s, ka_s, va_s = per_seq(qt_s), per_seq(kt_s), per_seq(vt_s)
    cache_kt = jnp.swapaxes(cache_k, 3, 4)
    cache_vt = jnp.swapaxes(cache_v, 3, 4)
    kmean = _sample_kmean(cache_kt, layer, page_table, n_full, ppb).transpose(0, 2, 1, 3)
    blk_idx = _sample_pick(qa_s, kmean, MOBA_TOPK)
    lpages = blk_idx[..., None] * ppb + jnp.arange(ppb, dtype=jnp.int32)
    phys = jnp.take_along_axis(page_table[:, None, None, :],
                               lpages.reshape(n_s, heads, t_s, MOBA_TOPK * ppb), axis=-1)
    o_a_s = _sample_attn(qa_s, ka_s, va_s, cache_kt, cache_vt, layer, phys)
    o_a_s = o_a_s.transpose(0, 2, 1, 3).reshape(n_tok_s, aqk).astype(BF16)
    tr = lambda a: a.reshape(n_s, t_s, b_heads, dk).transpose(0, 2, 3, 1)
    o_b_s, s_sample = _gla_sample(tr(qb_s), tr(kb_s), tr(la_s), vb_s.reshape(n_s, t_s, bv),
                                  gb_s.reshape(n_s, t_s, bv), gn, state_gla, layer, dk=dk, dv=dv)
    x2_s, h2_s, idx_s, gate_s = merge(x_sample.reshape(n_tok_s, d), o_a_s, o_b_s.reshape(n_tok_s, bv))

    moe = _moe(jnp.concatenate([h2_p, h2_s]), jnp.concatenate([idx_p, idx_s]),
               jnp.concatenate([gate_p, gate_s]), expert_weights)
    n_tok_p = n_p * t_p
    y_prompt = (x2_p + moe[:n_tok_p]).reshape(n_p, t_p, d)
    y_sample = (x2_s + moe[n_tok_p:]).reshape(n_s, t_s, d)
    k_prompt = jnp.swapaxes(kt, 2, 3)
    v_prompt = jnp.swapaxes(vt, 2, 3)
    return (y_prompt, y_sample, k_prompt[None], v_prompt[None], ka_s[None], va_s[None],
            s_prompt[None], s_sample[None])
```

```python
import functools

import jax
import jax.numpy as jnp
from jax import lax
from jax.experimental import pallas as pl
from jax.experimental.pallas import tpu as pltpu

F32 = jnp.float32
BF16 = jnp.bfloat16
HIGHEST = lax.Precision.HIGHEST

MOBA_BLOCK = 256
MOBA_TOPK = 3
ROPE_THETA = 10000.0
GLA_GATE_NORM = 16.0
NORM_EPS = 1e-6
MOE_TOP_K = 4
SWIGLU_ALPHA = 1.702
SWIGLU_LIMIT = 7.0

LANES = 128
PAD_ROWS = 16
VMEM_LIMIT_BYTES = 56 * 1024 * 1024

MASK_VALUE = -1e30
LOG2_E = 1.4426950408889634
PROJ_ROWS = 512
GLA_CHUNK = 64
GLA_SUB = 16
GLA_STEP_ROWS = 512
GLA_FAST_MIN_LOG_DECAY = -80.0
MOE_ROWS = 512
MOBA_GROUP = 4
SAMPLE_KMEAN_PAGES = 16


def _cparams(*sem):
    return pltpu.CompilerParams(dimension_semantics=sem, vmem_limit_bytes=VMEM_LIMIT_BYTES)


def _rms(x):
    return x * lax.rsqrt(jnp.mean(x * x, axis=-1, keepdims=True) + NORM_EPS)


def _dot(a, b):
    return jnp.dot(a, b, preferred_element_type=F32)


def _dot_nt(a, b, precision=None):
    return lax.dot_general(a, b, (((1,), (1,)), ((), ())), precision=precision,
                           preferred_element_type=F32)


def _dot_tn(a, b):
    return lax.dot_general(a, b, (((0,), (0,)), ((), ())), preferred_element_type=F32)


def _top_mask(g, pos, n_pick, n_pos, axis):
    sel = jnp.zeros(g.shape, F32)
    picks = []
    for _ in range(n_pick):
        mx = jnp.max(g, axis=axis, keepdims=True)
        cand = jnp.where((g == mx) & (mx > -jnp.inf), pos, n_pos)
        idx = jnp.min(cand, axis=axis, keepdims=True)
        pick = pos == idx
        sel = jnp.where(pick, 1.0, sel)
        g = jnp.where(pick, -jnp.inf, g)
        picks.append((idx, mx))
    return sel, picks


def _proj_kernel(x_ref, g1_ref, wqkv_ref, wrest_ref, qg_ref, kg_ref, wgu_ref, bgu_ref, cos_ref,
                 sin_ref, qt_ref, kt_ref, vt_ref, qb_ref, kb_ref, vb_ref, la_ref, gb_ref,
                 *, heads, hd, bk, bv):
    x = x_ref[0]
    hb = (_rms(x) * g1_ref[...]).astype(BF16)
    cos = cos_ref[...]
    sin = sin_ref[...]
    aqk = heads * hd

    def mm_t(lo, hi):
        return _dot_nt(wqkv_ref[lo:hi, :], hb)

    def norm_rope_t(z, g_ref, out_ref):
        for h in range(heads):
            y = z[h * hd:(h + 1) * hd, :]
            y = y * lax.rsqrt(jnp.mean(y * y, axis=0, keepdims=True) + NORM_EPS) * g_ref[...]
            rot = jnp.concatenate([y[hd // 2:], y[:hd // 2]], axis=0)
            out_ref[0, h] = y * cos + rot * sin

    norm_rope_t(mm_t(0, aqk), qg_ref, qt_ref)
    norm_rope_t(mm_t(aqk, 2 * aqk), kg_ref, kt_ref)
    zv = mm_t(2 * aqk, 3 * aqk)
    for h in range(heads):
        vt_ref[0, h] = zv[h * hd:(h + 1) * hd, :]

    def mm(lo, hi):
        return _dot(hb, wrest_ref[:, lo:hi])

    qb_ref[0] = mm(0, bk)
    kb_ref[0] = mm(bk, 2 * bk)
    vb_ref[0] = mm(2 * bk, 2 * bk + bv).astype(BF16)
    gb_ref[0] = mm(2 * bk + bv, 2 * bk + 2 * bv).astype(BF16)
    lr = mm(2 * bk + 2 * bv, 2 * bk + 2 * bv + LANES)
    u = _dot(lr.astype(BF16), wgu_ref[...]) + bgu_ref[...]
    log_sig = jnp.minimum(u, 0.0) - jnp.log(1.0 + jnp.exp(-jnp.abs(u)))
    la_ref[0] = log_sig / GLA_GATE_NORM


def _project(x, cos_t, sin_t, norm1_g, w_qkv_t, w_rest, q_norm_g, k_norm_g, w_gu_p, b_gu,
             *, heads, hd, bk, bv):
    n, t, d = x.shape
    tm = min(PROJ_ROWS, t)
    assert t % tm == 0
    full = lambda a: pl.BlockSpec(a.shape, lambda b, i: (0,) * a.ndim)
    head_out = pl.BlockSpec((1, heads, hd, tm), lambda b, i: (b, 0, 0, i))
    row_out = lambda w: pl.BlockSpec((1, tm, w), lambda b, i: (b, i, 0))
    table = pl.BlockSpec((hd, tm), lambda b, i: (0, i))
    head_shape = jax.ShapeDtypeStruct((n, heads, hd, t), F32)
    return pl.pallas_call(
        functools.partial(_proj_kernel, heads=heads, hd=hd, bk=bk, bv=bv),
        grid=(n, t // tm),
        in_specs=[pl.BlockSpec((1, tm, d), lambda b, i: (b, i, 0)),
                  full(norm1_g), full(w_qkv_t), full(w_rest), full(q_norm_g), full(k_norm_g),
                  full(w_gu_p), full(b_gu), table, table],
        out_specs=[head_out, head_out, head_out, row_out(bk), row_out(bk), row_out(bv),
                   row_out(bk), row_out(bv)],
        out_shape=[head_shape, head_shape, head_shape,
                   jax.ShapeDtypeStruct((n, t, bk), F32),
                   jax.ShapeDtypeStruct((n, t, bk), F32),
                   jax.ShapeDtypeStruct((n, t, bv), BF16),
                   jax.ShapeDtypeStruct((n, t, bk), F32),
                   jax.ShapeDtypeStruct((n, t, bv), BF16)],
        compiler_params=_cparams("parallel", "parallel"),
        name="proj",
    )(x, norm1_g, w_qkv_t, w_rest, q_norm_g, k_norm_g, w_gu_p, b_gu, cos_t, sin_t)


def _moba_kernel(qt_ref, kt_ref, vt_ref, o_ref, k_sc, vt_sc, kmean_sc, qs_sc, bias_sc, m_sc, acc_sc,
                 sa_sc, sb_sc, ma_sc, mb_sc, *, nb, blk, hd, hp, grp):
    c = pl.program_id(2)
    nq = hp * blk
    vr = hd + PAD_ROWS
    n_groups = nb // grp
    cols = [slice(i * blk, (i + 1) * blk) for i in range(hp)]

    @pl.when(c == 0)
    def _():
        row = lax.broadcasted_iota(jnp.int32, (LANES - hd, blk), 0)
        ones_row = (lax.broadcasted_iota(jnp.int32, (PAD_ROWS, blk), 0) == 0).astype(BF16)

        def stage_kv(j, carry):
            off = pl.multiple_of(j * blk, blk)
            place = (row == j % grp).astype(F32)
            for i in range(hp):
                k_nat = jnp.concatenate([kt_ref[0, i, :, pl.ds(off, blk)], place], axis=0).T
                k_sc[i, pl.ds(off, blk), :] = k_nat.astype(BF16)
                kmean_sc[i, pl.ds(j, 1), :] = jnp.mean(k_nat, axis=0, keepdims=True)
                vt_sc[i * vr:i * vr + hd, pl.ds(off, blk)] = vt_ref[0, i, :, pl.ds(off, blk)].astype(BF16)
                vt_sc[i * vr + hd:(i + 1) * vr, pl.ds(off, blk)] = ones_row
            return carry

        lax.fori_loop(0, nb, stage_kv, 0)

    zero_rows = jnp.zeros((LANES - hd, blk), F32)
    q_pad = [jnp.concatenate([qt_ref[0, i], zero_rows], axis=0) for i in range(hp)]
    gate = jnp.concatenate(
        [jnp.dot(kmean_sc[i], q_pad[i], precision=HIGHEST, preferred_element_type=F32)
         for i in range(hp)], axis=1)
    blk_row = lax.broadcasted_iota(jnp.int32, (nb, nq), 0)
    sel, _ = _top_mask(jnp.where(blk_row < c, gate, -jnp.inf), blk_row, min(MOBA_TOPK, nb - 1),
                       nb, axis=0)
    bias = (sel - 1.0) * -MASK_VALUE
    pad = jnp.zeros((PAD_ROWS - grp, blk), F32)
    for g in range(n_groups):
        for i in range(hp):
            bias_sc[g, i] = jnp.concatenate([bias[g * grp:(g + 1) * grp, cols[i]], pad],
                                            axis=0).astype(BF16)
    for i in range(hp):
        qs_sc[i] = (q_pad[i] * (hd ** -0.5 * LOG2_E)).astype(BF16)

    def scores(off, n_keys):
        return jnp.concatenate([_dot(k_sc[i, pl.ds(off, n_keys), :], qs_sc[i]) for i in range(hp)],
                               axis=1)

    def update(s, m_blk, off, n_keys, first):
        m_new = m_blk if first else jnp.maximum(m_sc[...], m_blk)
        pb = jnp.exp2(s - m_new).astype(BF16)
        if not first:
            alpha = jnp.exp2(m_sc[...] - m_new)
        m_sc[...] = m_new
        for i in range(hp):
            pv = _dot(vt_sc[i * vr:(i + 1) * vr, pl.ds(off, n_keys)], pb[:, cols[i]])
            acc_sc[:, cols[i]] = pv if first else alpha[:, cols[i]] * acc_sc[:, cols[i]] + pv

    own_off = pl.multiple_of(c * blk, blk)
    causal = (lax.broadcasted_iota(jnp.int32, (blk, nq), 0)
              <= lax.broadcasted_iota(jnp.int32, (blk, nq), 1) % blk)
    s_own = jnp.where(causal, scores(own_off, blk), MASK_VALUE)
    update(s_own, jnp.max(s_own, axis=0, keepdims=True), own_off, blk, True)

    n_grp = (c + grp - 1) // grp

    def stage(g, s_buf, mx_buf):
        g = jnp.minimum(g, n_groups - 1)
        for i in range(hp):
            qs_sc[i, hd:hd + PAD_ROWS, :] = bias_sc[g, i]
        s = scores(pl.multiple_of(g * (grp * blk), grp * blk), grp * blk)
        s_buf[...] = s
        mx_buf[...] = jnp.max(s, axis=0, keepdims=True)

    def consume(g, s_buf, mx_buf):
        update(s_buf[...], mx_buf[...], pl.multiple_of(g * (grp * blk), grp * blk), grp * blk, False)

    @pl.when(n_grp > 0)
    def _():
        stage(0, sa_sc, ma_sc)

    def body(u, carry):
        stage(2 * u + 1, sb_sc, mb_sc)
        consume(2 * u, sa_sc, ma_sc)
        stage(2 * u + 2, sa_sc, ma_sc)
        consume(2 * u + 1, sb_sc, mb_sc)
        return carry

    lax.fori_loop(0, n_grp // 2, body, 0)

    @pl.when(n_grp % 2 == 1)
    def _():
        consume(n_grp - 1, sa_sc, ma_sc)

    out_t = acc_sc[:hd, :] / acc_sc[hd:hd + 1, :]
    out_t = jnp.concatenate([out_t[:, cols[i]] for i in range(hp)], axis=0)
    o_ref[0] = out_t.T.astype(BF16)


def _moba_prompt(qt, kt, vt):
    n, heads, hd, t = qt.shape
    blk = MOBA_BLOCK
    assert t % blk == 0 and LANES % hd == 0
    hp = LANES // hd
    assert heads % hp == 0
    nb = t // blk
    grp = MOBA_GROUP if nb % MOBA_GROUP == 0 else 1
    nq = hp * blk
    kv_spec = pl.BlockSpec((1, hp, hd, t), lambda b, g, c: (b, g, 0, 0))
    return pl.pallas_call(
        functools.partial(_moba_kernel, nb=nb, blk=blk, hd=hd, hp=hp, grp=grp),
        grid=(n, heads // hp, nb),
        in_specs=[pl.BlockSpec((1, hp, hd, blk), lambda b, g, c: (b, g, 0, c)), kv_spec, kv_spec],
        out_specs=pl.BlockSpec((1, blk, hp * hd), lambda b, g, c: (b, c, g)),
        out_shape=jax.ShapeDtypeStruct((n, t, heads * hd), BF16),
        scratch_shapes=[pltpu.VMEM((hp, t, LANES), BF16),
                        pltpu.VMEM((hp * (hd + PAD_ROWS), t), BF16),
                        pltpu.VMEM((hp, nb, LANES), F32),
                        pltpu.VMEM((hp, LANES, blk), BF16),
                        pltpu.VMEM((nb // grp, hp, PAD_ROWS, blk), BF16),
                        pltpu.VMEM((1, nq), F32),
                        pltpu.VMEM((hd + PAD_ROWS, nq), F32),
                        pltpu.VMEM((grp * blk, nq), F32), pltpu.VMEM((grp * blk, nq), F32),
                        pltpu.VMEM((1, nq), F32), pltpu.VMEM((1, nq), F32)],
        compiler_params=_cparams("parallel", "parallel", "arbitrary"),
        name="moba_prompt",
    )(qt, kt, vt)


def _kmean_pages_kernel(pt_ref, *refs, n_pg, ppb, rows):
    del pt_ref
    out_ref = refs[n_pg]
    for b in range(n_pg // ppb):
        acc = refs[b * ppb][0, 0]
        for j in range(1, ppb):
            acc = acc + refs[b * ppb + j][0, 0]
        out_ref[0, b] = jnp.sum(acc, axis=-1) / rows


def _sample_kmean(cache_kt, layer, page_table, n_full, ppb):
    _, _, heads, hd, page = cache_kt.shape
    n, n_pages = page_table.shape
    n_pg = SAMPLE_KMEAN_PAGES
    assert (n_full * ppb) % n_pg == 0 and n_pg % ppb == 0

    def page_map(b, g, pt, j):
        return (layer, pt[b * n_pages + g * n_pg + j], 0, 0, 0)

    specs = [pl.BlockSpec((1, 1, heads, hd, page), functools.partial(page_map, j=j))
             for j in range(n_pg)]
    return pl.pallas_call(
        functools.partial(_kmean_pages_kernel, n_pg=n_pg, ppb=ppb, rows=float(ppb * page)),
        grid_spec=pltpu.PrefetchScalarGridSpec(
            num_scalar_prefetch=1, grid=(n, n_full * ppb // n_pg), in_specs=specs,
            out_specs=pl.BlockSpec((1, n_pg // ppb, heads, hd), lambda b, g, pt: (b, g, 0, 0))),
        out_shape=jax.ShapeDtypeStruct((n, n_full, heads, hd), F32),
        compiler_params=_cparams("parallel", "arbitrary"),
        name="sample_kmean",
    )(page_table.reshape(-1), *([cache_kt] * n_pg))


def _sample_pick_kernel(q_ref, km_ref, idx_ref, *, heads, n_full, n_sel):
    col = lax.broadcasted_iota(jnp.int32, (q_ref.shape[2], n_full), 1)
    for h in range(heads):
        gate = _dot_nt(q_ref[0, h], km_ref[0, h], precision=HIGHEST)
        _, picks = _top_mask(gate, col, n_sel, n_full, axis=1)
        idx_ref[0, h] = jnp.concatenate([p[0] for p in picks], axis=1)


def _sample_pick(q, kmean, n_sel):
    n, heads, t, hd = q.shape
    n_full = kmean.shape[2]
    return pl.pallas_call(
        functools.partial(_sample_pick_kernel, heads=heads, n_full=n_full, n_sel=n_sel),
        grid=(n,),
        in_specs=[pl.BlockSpec((1, heads, t, hd), lambda b: (b, 0, 0, 0)),
                  pl.BlockSpec((1, heads, n_full, hd), lambda b: (b, 0, 0, 0))],
        out_specs=pl.BlockSpec((1, heads, t, n_sel), lambda b: (b, 0, 0, 0)),
        out_shape=jax.ShapeDtypeStruct((n, heads, t, n_sel), jnp.int32),
        compiler_params=_cparams("parallel"),
        name="sample_pick",
    )(q, kmean)


def _sample_attn_kernel(ph_ref, q_ref, kn_ref, vn_ref, *refs, n_pg):
    del ph_ref
    tq, hd = q_ref.shape[2], q_ref.shape[3]
    k_refs, v_refs, o_ref = refs[:tq * n_pg], refs[tq * n_pg:2 * tq * n_pg], refs[2 * tq * n_pg]
    qs = (q_ref[0, 0] * hd ** -0.5).astype(BF16)
    own_ok = (lax.broadcasted_iota(jnp.int32, (tq, tq), 1)
              <= lax.broadcasted_iota(jnp.int32, (tq, tq), 0))
    s_own = jnp.where(own_ok, _dot_nt(qs, kn_ref[0, 0].astype(BF16)), MASK_VALUE)
    vn = vn_ref[0, 0].astype(BF16)
    for t in range(tq):
        pages = slice(t * n_pg, (t + 1) * n_pg)
        kt = jnp.concatenate([r[0, 0, 0] for r in k_refs[pages]], axis=1).astype(BF16)
        vt = jnp.concatenate([r[0, 0, 0] for r in v_refs[pages]], axis=1).astype(BF16)
        s_sel = _dot(qs, kt)
        m = jnp.maximum(jnp.max(s_sel, axis=1, keepdims=True), jnp.max(s_own, axis=1, keepdims=True))
        p_sel = jnp.exp(s_sel - m)
        p_own = jnp.exp(s_own - m)
        denom = jnp.sum(p_sel, axis=1, keepdims=True) + jnp.sum(p_own, axis=1, keepdims=True)
        o = (_dot_nt(p_sel.astype(BF16), vt) + _dot(p_own.astype(BF16), vn)) / denom
        o_ref[0, 0, t:t + 1, :] = o[t:t + 1]


def _sample_attn(q, k_new, v_new, cache_kt, cache_vt, layer, phys):
    n, heads, t, hd = q.shape
    page = cache_kt.shape[4]
    n_pg = phys.shape[-1]

    def page_map(b, h, ph, j):
        return (layer, ph[(b * heads + h) * (t * n_pg) + j], h, 0, 0)

    pages = [pl.BlockSpec((1, 1, 1, hd, page), functools.partial(page_map, j=j))
             for j in range(t * n_pg)]
    own = pl.BlockSpec((1, 1, t, hd), lambda b, h, ph: (b, h, 0, 0))
    return pl.pallas_call(
        functools.partial(_sample_attn_kernel, n_pg=n_pg),
        grid_spec=pltpu.PrefetchScalarGridSpec(
            num_scalar_prefetch=1, grid=(n, heads),
            in_specs=[own, own, own] + pages * 2, out_specs=own),
        out_shape=jax.ShapeDtypeStruct((n, heads, t, hd), F32),
        compiler_params=_cparams("parallel", "parallel"),
        name="sample_attn",
    )(phys.reshape(-1), q, k_new, v_new, *([cache_kt] * (t * n_pg)), *([cache_vt] * (t * n_pg)))


def _gla_kernel(q_ref, k_ref, v_ref, la_ref, gb_ref, gn_ref, o_ref, sfin_ref, st_sc, qk_sc, b_sc,
                la_sc, attn_sc, *, chunk, sub, n_chunk, heads, dk, dv):
    step = pl.program_id(1)

    @pl.when(step == 0)
    def _():
        st_sc[...] = jnp.zeros(st_sc.shape, F32)

    r_i = lax.broadcasted_iota(jnp.int32, (chunk, chunk), 0)
    c_i = lax.broadcasted_iota(jnp.int32, (chunk, chunk), 1)
    tri = r_i >= c_i
    tri_b = tri.astype(BF16)
    row_k = lax.broadcasted_iota(jnp.int32, (chunk, dk), 0)
    q_scale = dk ** -0.5

    def intra_fast(q, k, b):
        rows = []
        for i in range(chunk // sub):
            lo, hi = i * sub, (i + 1) * sub
            ref = b[lo - 1:lo, :] if i > 0 else jnp.zeros((1, dk), F32)
            qt = (q[lo:hi] * jnp.exp(b[lo:hi] - ref)).astype(BF16)
            kt = (k * jnp.exp(jnp.where(row_k < hi, ref - b, -jnp.inf))).astype(BF16)
            rows.append(_dot_nt(qt, kt))
        return jnp.where(tri, jnp.concatenate(rows, axis=0), 0.0)

    def intra_exact(q, k, b, la):
        qk_sc[...] = q
        b_sc[...] = b
        la_sc[...] = la
        col = lax.broadcasted_iota(jnp.int32, (1, chunk), 1)

        def row(t, carry):
            q_t = qk_sc[pl.ds(t, 1), :]
            la_t = la_sc[pl.ds(t, 1), :]
            before = b_sc[pl.ds(t, 1), :] - la_t
            kt = (k * jnp.exp(jnp.where(row_k < t, before - b, -jnp.inf))).astype(BF16)
            a_row = _dot_nt((q_t * jnp.exp(la_t)).astype(BF16), kt)
            k_t = jnp.sum(jnp.where(row_k == t, k, 0.0), axis=0, keepdims=True)
            diag = jnp.sum(q_t * k_t, axis=1, keepdims=True)
            attn_sc[pl.ds(t, 1), :] = jnp.where(col == t, diag, a_row)
            return carry

        lax.fori_loop(0, chunk, row, 0)
        return attn_sc[...]

    def one_chunk(ci, carry, exact):
        off = pl.multiple_of(ci * chunk, chunk)
        rows_ds = pl.ds(off, chunk)
        la = la_ref[0, rows_ds, :]
        la_hi = la.astype(BF16)
        la_mid = (la - la_hi.astype(F32)).astype(BF16)
        la_lo = (la - la_hi.astype(F32) - la_mid.astype(F32)).astype(BF16)
        b_all = _dot(tri_b, la_hi) + _dot(tri_b, la_mid) + _dot(tri_b, la_lo)
        for h in range(heads):
            kc = slice(h * dk, (h + 1) * dk)
            vc = slice(h * dv, (h + 1) * dv)
            q = q_ref[0, rows_ds, kc] * q_scale
            k = k_ref[0, rows_ds, kc]
            v = v_ref[0, rows_ds, vc]
            b = b_all[:, kc]
            st = st_sc[h]
            o = _dot_nt((q * jnp.exp(b)).astype(BF16), st.astype(BF16))
            attn = intra_exact(q, k, b, la[:, kc]) if exact else intra_fast(q, k, b)
            o = o + _dot(attn.astype(BF16), v)
            b_last = b[chunk - 1:chunk, :]
            k_dec = (k * jnp.exp(b_last - b)).astype(BF16)
            st_sc[h] = st * jnp.exp(b_last) + _dot_tn(v, k_dec)
            gb = gb_ref[0, rows_ds, vc].astype(F32)
            o_ref[0, rows_ds, vc] = (_rms(o) * gn_ref[...] * (gb * jax.nn.sigmoid(gb))).astype(BF16)
        return carry

    safe = jnp.min(la_ref[0]) * sub > GLA_FAST_MIN_LOG_DECAY

    @pl.when(safe)
    def _():
        lax.fori_loop(0, n_chunk, functools.partial(one_chunk, exact=False), 0)

    @pl.when(jnp.logical_not(safe))
    def _():
        lax.fori_loop(0, n_chunk, functools.partial(one_chunk, exact=True), 0)

    @pl.when(step == pl.num_programs(1) - 1)
    def _():
        sfin_ref[0] = st_sc[...]


def _gla_prompt(qb, kb, vb, la, gb, gla_norm_g, *, heads, dk, dv):
    n, t, _ = qb.shape
    rows = min(GLA_STEP_ROWS, t)
    chunk = min(GLA_CHUNK, rows)
    assert t % rows == 0 and rows % chunk == 0 and chunk % GLA_SUB == 0
    kspec = pl.BlockSpec((1, rows, heads * dk), lambda b, s: (b, s, 0))
    vspec = pl.BlockSpec((1, rows, heads * dv), lambda b, s: (b, s, 0))
    return pl.pallas_call(
        functools.partial(_gla_kernel, chunk=chunk, sub=GLA_SUB, n_chunk=rows // chunk,
                          heads=heads, dk=dk, dv=dv),
        grid=(n, t // rows),
        in_specs=[kspec, kspec, vspec, kspec, vspec, pl.BlockSpec((1, dv), lambda b, s: (0, 0))],
        out_specs=[vspec, pl.BlockSpec((1, heads, dv, dk), lambda b, s: (b, 0, 0, 0))],
        out_shape=[jax.ShapeDtypeStruct((n, t, heads * dv), BF16),
                   jax.ShapeDtypeStruct((n, heads, dv, dk), F32)],
        scratch_shapes=[pltpu.VMEM((heads, dv, dk), F32), pltpu.VMEM((chunk, dk), F32),
                        pltpu.VMEM((chunk, dk), F32), pltpu.VMEM((chunk, dk), F32),
                        pltpu.VMEM((chunk, chunk), F32)],
        compiler_params=_cparams("parallel", "arbitrary"),
        name="gla_prompt",
    )(qb, kb, vb, la, gb, gla_norm_g)


def _gla_sample_kernel(qt_ref, kt_ref, lat_ref, v_ref, gb_ref, gn_ref, s0_ref, o_ref, sfin_ref, *, dk):
    s = s0_ref[0, 0, 0]
    qt = qt_ref[0, 0] * dk ** -0.5
    kt = kt_ref[0, 0]
    at = jnp.exp(lat_ref[0, 0])
    v = v_ref[0].astype(F32)
    outs = []
    for t in range(v.shape[0]):
        s = at[:, t:t + 1] * s + kt[:, t:t + 1] * v[t:t + 1, :]
        outs.append(jnp.sum(qt[:, t:t + 1] * s, axis=0, keepdims=True))
    o = jnp.concatenate(outs, axis=0)
    gb = gb_ref[0].astype(F32)
    o_ref[0] = (_rms(o) * gn_ref[...] * (gb * jax.nn.sigmoid(gb))).astype(BF16)
    sfin_ref[0, 0] = s


def _gla_sample(qt, kt, lat, vb, gb, gla_norm_g, state, layer, *, dk, dv):
    n, heads, _, t = qt.shape
    tspec = pl.BlockSpec((1, 1, dk, t), lambda b, h: (b, h, 0, 0))
    vspec = pl.BlockSpec((1, t, dv), lambda b, h: (b, 0, h))
    return pl.pallas_call(
        functools.partial(_gla_sample_kernel, dk=dk),
        grid=(n, heads),
        in_specs=[tspec, tspec, tspec, vspec, vspec, pl.BlockSpec((1, dv), lambda b, h: (0, 0)),
                  pl.BlockSpec((1, 1, 1, dk, dv), lambda b, h: (layer, b, h, 0, 0))],
        out_specs=[vspec, pl.BlockSpec((1, 1, dk, dv), lambda b, h: (b, h, 0, 0))],
        out_shape=[jax.ShapeDtypeStruct((n, t, heads * dv), BF16),
                   jax.ShapeDtypeStruct((n, heads, dk, dv), F32)],
        compiler_params=_cparams("parallel", "parallel"),
        name="gla_sample",
    )(qt, kt, lat, vb, gb, gla_norm_g, state)


def _merge_kernel(x_ref, oa_ref, ob_ref, g1_ref, wa_ref, wb_ref, wmg_ref, wo_ref, g2_ref, wrh_ref,
                  wrl_ref, br_ref, x2_ref, h2_ref, idx_ref, gate_ref, *, d, n_exp):
    x = x_ref[...]
    hb = (_rms(x) * g1_ref[...]).astype(BF16)
    y_a = _dot(oa_ref[...], wa_ref[...])
    y_b = _dot(ob_ref[...], wb_ref[...])
    mixed = (jax.nn.sigmoid(_dot(hb, wmg_ref[:, :d])) * y_a
             + jax.nn.sigmoid(_dot(hb, wmg_ref[:, d:])) * y_b)
    x2 = x + _dot(mixed.astype(BF16), wo_ref[...])
    x2_ref[...] = x2
    h2 = _rms(x2) * g2_ref[...]
    h2_hi = h2.astype(BF16)
    h2_ref[...] = h2_hi
    h2_lo = (h2 - h2_hi.astype(F32)).astype(BF16)
    logits = (_dot(h2_hi, wrh_ref[...]) + _dot(h2_hi, wrl_ref[...]) + _dot(h2_lo, wrh_ref[...])
              + br_ref[...])
    col = lax.broadcasted_iota(jnp.int32, logits.shape, 1)
    _, picks = _top_mask(logits, col, MOE_TOP_K, n_exp, axis=1)
    top_val = jnp.concatenate([p[1] for p in picks], axis=1)
    e = jnp.exp(top_val - top_val[:, :1])
    idx_ref[...] = jnp.concatenate([p[0] for p in picks], axis=1)
    gate_ref[...] = e / jnp.sum(e, axis=1, keepdims=True)


def _merge(x, o_a, o_b, norm1_g, w_a, w_b, w_mg, w_o, norm2_g, w_router, b_router):
    n_tok, d = x.shape
    n_exp = w_router.shape[1]
    w_router_hi = w_router.astype(BF16)
    w_router_lo = (w_router - w_router_hi.astype(F32)).astype(BF16)
    tm = min(PROJ_ROWS, n_tok)
    assert n_tok % tm == 0
    full = lambda a: pl.BlockSpec(a.shape, lambda i: (0,) * a.ndim)
    rows = lambda w: pl.BlockSpec((tm, w), lambda i: (i, 0))
    return pl.pallas_call(
        functools.partial(_merge_kernel, d=d, n_exp=n_exp),
        grid=(n_tok // tm,),
        in_specs=[rows(d), rows(o_a.shape[1]), rows(o_b.shape[1]), full(norm1_g), full(w_a),
                  full(w_b), full(w_mg), full(w_o), full(norm2_g), full(w_router_hi),
                  full(w_router_lo), full(b_router)],
        out_specs=[rows(d), rows(d), rows(MOE_TOP_K), rows(MOE_TOP_K)],
        out_shape=[jax.ShapeDtypeStruct((n_tok, d), F32), jax.ShapeDtypeStruct((n_tok, d), BF16),
                   jax.ShapeDtypeStruct((n_tok, MOE_TOP_K), jnp.int32),
                   jax.ShapeDtypeStruct((n_tok, MOE_TOP_K), F32)],
        compiler_params=_cparams("parallel"),
        name="merge",
    )(x, o_a, o_b, norm1_g, w_a, w_b, w_mg, w_o, norm2_g, w_router_hi, w_router_lo, b_router)


def _expert_kernel(te_ref, nt_ref, x_ref, wup_ref, bg_ref, bl_ref, wdn_ref, bdn_ref, y_ref,
                   wg_sc, wl_sc, wd_sc, t_sc, *, slab):
    i = pl.program_id(0)

    @pl.when((i == 0) | (te_ref[i] != te_ref[jnp.maximum(i - 1, 0)]))
    def _():
        for c in range(0, wup_ref.shape[2], slab):
            rows = slice(c // 2, (c + slab) // 2)
            for j in range(t_sc.shape[0]):
                lanes = slice(j * LANES, (j + 1) * LANES)
                t_sc[j] = wup_ref[0, lanes, c:c + slab].T
                wg_sc[rows, lanes] = t_sc[j, pl.ds(0, slab // 2, stride=2), :].astype(BF16)
                wl_sc[rows, lanes] = t_sc[j, pl.ds(1, slab // 2, stride=2), :].astype(BF16)
        wd_sc[...] = wdn_ref[0].astype(BF16)

    @pl.when(i < nt_ref[0])
    def _():
        x = x_ref[...]
        x_glu = jnp.minimum(_dot_nt(x, wg_sc[...]) + bg_ref[0], SWIGLU_LIMIT)
        x_lin = jnp.clip(_dot_nt(x, wl_sc[...]) + bl_ref[0], -SWIGLU_LIMIT, SWIGLU_LIMIT)
        a = x_glu * jax.nn.sigmoid(SWIGLU_ALPHA * x_glu) * (x_lin + 1.0)
        y_ref[...] = (_dot(a.astype(BF16), wd_sc[...]) + bdn_ref[0]).astype(y_ref.dtype)

    @pl.when(i >= nt_ref[0])
    def _():
        y_ref[...] = jnp.zeros(y_ref.shape, y_ref.dtype)


def _experts(buf, tile_e, n_used, w_up, b_glu, b_lin, w_dn, b_dn):
    r, d = buf.shape
    d_ff = w_dn.shape[1]
    slab = min(2 * LANES, 2 * d_ff)
    by_expert = lambda *blk: pl.BlockSpec((1,) + blk, lambda i, te, nt: (te[i], 0, 0))
    return pl.pallas_call(
        functools.partial(_expert_kernel, slab=slab),
        grid_spec=pltpu.PrefetchScalarGridSpec(
            num_scalar_prefetch=2, grid=(r // MOE_ROWS,),
            in_specs=[pl.BlockSpec((MOE_ROWS, d), lambda i, te, nt: (i, 0)),
                      by_expert(d, 2 * d_ff), by_expert(1, d_ff), by_expert(1, d_ff),
                      by_expert(d_ff, d), by_expert(1, d)],
            out_specs=pl.BlockSpec((MOE_ROWS, d), lambda i, te, nt: (i, 0)),
            scratch_shapes=[pltpu.VMEM((d_ff, d), BF16), pltpu.VMEM((d_ff, d), BF16),
                            pltpu.VMEM((d_ff, d), BF16), pltpu.VMEM((d // LANES, slab, LANES), F32)]),
        out_shape=jax.ShapeDtypeStruct((r, d), BF16),
        compiler_params=_cparams("arbitrary"),
        name="experts",
    )(tile_e, n_used, buf, w_up, b_glu, b_lin, w_dn, b_dn)


def _moe(h2, top_idx, gate, expert_weights):
    n_tok, d = h2.shape
    n_exp = expert_weights[0].shape[0]
    n_assign = n_tok * MOE_TOP_K
    flat_e = top_idx.reshape(-1)
    onehot = (flat_e[:, None] == jnp.arange(n_exp, dtype=jnp.int32)[None, :]).astype(jnp.int32)
    csum = jnp.cumsum(onehot, axis=0)
    rank = jnp.sum(csum * onehot, axis=1) - 1
    counts = csum[-1]
    padded = (counts + MOE_ROWS - 1) // MOE_ROWS * MOE_ROWS
    pend = jnp.cumsum(padded)
    dest = ((pend - padded)[flat_e] + rank).astype(jnp.int32)
    n_tiles = -(-n_assign // MOE_ROWS) + n_exp
    tile_e = jnp.minimum(jnp.searchsorted(pend, jnp.arange(n_tiles, dtype=jnp.int32) * MOE_ROWS,
                                          side='right'), n_exp - 1).astype(jnp.int32)
    n_used = (pend[-1:] // MOE_ROWS).astype(jnp.int32)
    src = jnp.zeros((n_tiles * MOE_ROWS,), jnp.int32).at[dest].set(
        jnp.arange(n_assign, dtype=jnp.int32) // MOE_TOP_K)
    y = _experts(h2[src], tile_e, n_used, *expert_weights)
    y_kt = y[dest.reshape(n_tok, MOE_TOP_K).T]
    return jnp.sum(y_kt.astype(F32) * gate.T[:, :, None], axis=0)


def _rope_tables_t(pos, hd):
    half = hd // 2
    inv = ROPE_THETA ** (-jnp.arange(half, dtype=F32) / half)
    ang = inv[:, None] * pos.astype(F32)[None, :]
    cos, sin = jnp.cos(ang), jnp.sin(ang)
    return jnp.concatenate([cos, cos], axis=0), jnp.concatenate([-sin, sin], axis=0)


def kernel(x_prompt, x_sample, cache_k, cache_v, state_gla, page_table, norm1_g, w_in, q_norm_g,
           k_norm_g, w_gate_up, b_gate_up, gla_norm_g, w_branch_a, w_branch_b, w_merge_gate, w_out,
           norm2_g, w_router, b_router, w_up, b_up, w_down, b_down):
    depth = norm1_g.shape[0]
    assert depth == 1, "sample/prompt streams are only chained through one layer here"
    layer = 0
    n_p, t_p, d = x_prompt.shape
    n_s, t_s, _ = x_sample.shape
    heads, page, hd = cache_k.shape[2:]
    b_heads, dk, dv = state_gla.shape[2:]
    bk, bv = b_heads * dk, b_heads * dv
    aqk = heads * hd
    rank = w_gate_up.shape[1]
    past_len = page_table.shape[1] * page
    ppb = MOBA_BLOCK // page
    n_full = past_len // MOBA_BLOCK
    assert past_len % MOBA_BLOCK == 0 and n_full >= MOBA_TOPK and rank <= LANES
    assert w_in.shape[2] == 3 * aqk + 2 * bk + 2 * bv + rank

    w_qkv_t = w_in[layer][:, :3 * aqk].T.astype(BF16)
    w_rest = jnp.pad(w_in[layer][:, 3 * aqk:], ((0, 0), (0, LANES - rank))).astype(BF16)
    w_gu_p = jnp.pad(w_gate_up[layer], ((0, LANES - rank), (0, 0))).astype(BF16)
    g1 = norm1_g[layer][None, :]
    g2 = norm2_g[layer][None, :]
    qg = q_norm_g[layer][:, None]
    kg = k_norm_g[layer][:, None]
    bgu = b_gate_up[layer][None, :]
    gn = gla_norm_g[layer][None, :]
    w_a = w_branch_a[layer].astype(BF16)
    w_b = w_branch_b[layer].astype(BF16)
    w_mg = w_merge_gate[layer].astype(BF16)
    w_o = w_out[layer].astype(BF16)
    w_r = w_router[layer]
    b_r = b_router[layer][None, :]
    expert_weights = (w_up[layer], b_up[layer][:, None, 0::2], b_up[layer][:, None, 1::2],
                      w_down[layer], b_down[layer][:, None, :])
    proj = functools.partial(_project, norm1_g=g1, w_qkv_t=w_qkv_t, w_rest=w_rest, q_norm_g=qg,
                             k_norm_g=kg, w_gu_p=w_gu_p, b_gu=bgu, heads=heads, hd=hd, bk=bk, bv=bv)
    merge = functools.partial(_merge, norm1_g=g1, w_a=w_a, w_b=w_b, w_mg=w_mg, w_o=w_o,
                              norm2_g=g2, w_router=w_r, b_router=b_r)

    cos_p, sin_p = _rope_tables_t(jnp.arange(t_p), hd)
    qt, kt, vt, qb, kb, vb, la, gb = proj(x_prompt, cos_p, sin_p)
    o_a = _moba_prompt(qt, kt, vt)
    o_b, st_p = _gla_prompt(qb, kb, vb, la, gb, gn, heads=b_heads, dk=dk, dv=dv)
    s_prompt = jnp.swapaxes(st_p, 2, 3)
    x2_p, h2_p, idx_p, gate_p = merge(x_prompt.reshape(n_p * t_p, d), o_a.reshape(n_p * t_p, -1),
                                      o_b.reshape(n_p * t_p, -1))

    n_tok_s = n_s * t_s
    cos_s, sin_s = _rope_tables_t(past_len + jnp.arange(t_s), hd)
    qt_s, kt_s, vt_s, qb_s, kb_s, vb_s, la_s, gb_s = proj(
        x_sample.reshape(1, n_tok_s, d), jnp.tile(cos_s, (1, n_s)), jnp.tile(sin_s, (1, n_s)))
    per_seq = lambda a: a.reshape(heads, hd, n_s, t_s).transpose(2, 0, 3, 1)
    qa_s, ka_s, va_s = per_seq(qt_s), per_seq(kt_s), per_seq(vt_s)
    cache_kt = jnp.swapaxes(cache_k, 3, 4)
    cache_vt = jnp.swapaxes(cache_v, 3, 4)
    kmean = _sample_kmean(cache_kt, layer, page_table, n_full, ppb).transpose(0, 2, 1, 3)
    blk_idx = _sample_pick(qa_s, kmean, MOBA_TOPK)
    lpages = blk_idx[..., None] * ppb + jnp.arange(ppb, dtype=jnp.int32)
    phys = jnp.take_along_axis(page_table[:, None, None, :],
                               lpages.reshape(n_s, heads, t_s, MOBA_TOPK * ppb), axis=-1)
    o_a_s = _sample_attn(qa_s, ka_s, va_s, cache_kt, cache_vt, layer, phys)
    o_a_s = o_a_s.transpose(0, 2, 1, 3).reshape(n_tok_s, aqk).astype(BF16)
    tr = lambda a: a.reshape(n_s, t_s, b_heads, dk).transpose(0, 2, 3, 1)
    o_b_s, s_sample = _gla_sample(tr(qb_s), tr(kb_s), tr(la_s), vb_s.reshape(n_s, t_s, bv),
                                  gb_s.reshape(n_s, t_s, bv), gn, state_gla, layer, dk=dk, dv=dv)
    x2_s, h2_s, idx_s, gate_s = merge(x_sample.reshape(n_tok_s, d), o_a_s, o_b_s.reshape(n_tok_s, bv))

    moe = _moe(jnp.concatenate([h2_p, h2_s]), jnp.concatenate([idx_p, idx_s]),
               jnp.concatenate([gate_p, gate_s]), expert_weights)
    n_tok_p = n_p * t_p
    y_prompt = (x2_p + moe[:n_tok_p]).reshape(n_p, t_p, d)
    y_sample = (x2_s + moe[n_tok_p:]).reshape(n_s, t_s, d)
    k_prompt = jnp.swapaxes(kt, 2, 3)
    v_prompt = jnp.swapaxes(vt, 2, 3)
    return (y_prompt, y_sample, k_prompt[None], v_prompt[None], ka_s[None], va_s[None],
            s_prompt[None], s_sample[None])
```

```python
import functools

import jax
import jax.numpy as jnp
from jax import lax
from jax.experimental import pallas as pl
from jax.experimental.pallas import tpu as pltpu

F32 = jnp.float32
BF16 = jnp.bfloat16
HIGHEST = lax.Precision.HIGHEST

MOBA_BLOCK = 256
MOBA_TOPK = 3
ROPE_THETA = 10000.0
GLA_GATE_NORM = 16.0
NORM_EPS = 1e-6
MOE_TOP_K = 4
SWIGLU_ALPHA = 1.702
SWIGLU_LIMIT = 7.0

LANES = 128
PAD_ROWS = 16
VMEM_LIMIT_BYTES = 56 * 1024 * 1024

MASK_VALUE = -1e30
LOG2_E = 1.4426950408889634
PROJ_ROWS = 512
GLA_CHUNK = 64
GLA_SUB = 16
GLA_STEP_ROWS = 512
GLA_FAST_MIN_LOG_DECAY = -80.0
MOE_ROWS = 512
ROUTE_ROWS = 1024
MOBA_GROUP = 4
SAMPLE_KMEAN_PAGES = 16


def _cparams(*sem):
    return pltpu.CompilerParams(dimension_semantics=sem, vmem_limit_bytes=VMEM_LIMIT_BYTES)


def _rms(x):
    return x * lax.rsqrt(jnp.mean(x * x, axis=-1, keepdims=True) + NORM_EPS)


def _dot(a, b):
    return jnp.dot(a, b, preferred_element_type=F32)


def _dot_nt(a, b, precision=None):
    return lax.dot_general(a, b, (((1,), (1,)), ((), ())), precision=precision,
                           preferred_element_type=F32)


def _dot_tn(a, b):
    return lax.dot_general(a, b, (((0,), (0,)), ((), ())), preferred_element_type=F32)


def _top_mask(g, pos, n_pick, n_pos, axis):
    sel = jnp.zeros(g.shape, F32)
    picks = []
    for _ in range(n_pick):
        mx = jnp.max(g, axis=axis, keepdims=True)
        cand = jnp.where((g == mx) & (mx > -jnp.inf), pos, n_pos)
        idx = jnp.min(cand, axis=axis, keepdims=True)
        pick = pos == idx
        sel = jnp.where(pick, 1.0, sel)
        g = jnp.where(pick, -jnp.inf, g)
        picks.append((idx, mx))
    return sel, picks


def _proj_kernel(x_ref, g1_ref, wqkv_ref, wrest_ref, qg_ref, kg_ref, wgu_ref, bgu_ref, cos_ref,
                 sin_ref, qt_ref, kt_ref, vt_ref, qb_ref, kb_ref, vb_ref, la_ref, gb_ref,
                 *, heads, hd, bk, bv):
    x = x_ref[0]
    hb = (_rms(x) * g1_ref[...]).astype(BF16)
    cos = cos_ref[...]
    sin = sin_ref[...]
    aqk = heads * hd

    def mm_t(lo, hi):
        return _dot_nt(wqkv_ref[lo:hi, :], hb)

    def norm_rope_t(z, g_ref, out_ref):
        for h in range(heads):
            y = z[h * hd:(h + 1) * hd, :]
            y = y * lax.rsqrt(jnp.mean(y * y, axis=0, keepdims=True) + NORM_EPS) * g_ref[...]
            rot = jnp.concatenate([y[hd // 2:], y[:hd // 2]], axis=0)
            out_ref[0, h] = y * cos + rot * sin

    norm_rope_t(mm_t(0, aqk), qg_ref, qt_ref)
    norm_rope_t(mm_t(aqk, 2 * aqk), kg_ref, kt_ref)
    zv = mm_t(2 * aqk, 3 * aqk)
    for h in range(heads):
        vt_ref[0, h] = zv[h * hd:(h + 1) * hd, :]

    def mm(lo, hi):
        return _dot(hb, wrest_ref[:, lo:hi])

    qb_ref[0] = mm(0, bk)
    kb_ref[0] = mm(bk, 2 * bk)
    vb_ref[0] = mm(2 * bk, 2 * bk + bv).astype(BF16)
    gb_ref[0] = mm(2 * bk + bv, 2 * bk + 2 * bv).astype(BF16)
    lr = mm(2 * bk + 2 * bv, 2 * bk + 2 * bv + LANES)
    u = _dot(lr.astype(BF16), wgu_ref[...]) + bgu_ref[...]
    log_sig = jnp.minimum(u, 0.0) - jnp.log(1.0 + jnp.exp(-jnp.abs(u)))
    la_ref[0] = log_sig / GLA_GATE_NORM


def _project(x, cos_t, sin_t, norm1_g, w_qkv_t, w_rest, q_norm_g, k_norm_g, w_gu_p, b_gu,
             *, heads, hd, bk, bv):
    n, t, d = x.shape
    tm = min(PROJ_ROWS, t)
    assert t % tm == 0
    full = lambda a: pl.BlockSpec(a.shape, lambda b, i: (0,) * a.ndim)
    head_out = pl.BlockSpec((1, heads, hd, tm), lambda b, i: (b, 0, 0, i))
    row_out = lambda w: pl.BlockSpec((1, tm, w), lambda b, i: (b, i, 0))
    table = pl.BlockSpec((hd, tm), lambda b, i: (0, i))
    head_shape = jax.ShapeDtypeStruct((n, heads, hd, t), F32)
    return pl.pallas_call(
        functools.partial(_proj_kernel, heads=heads, hd=hd, bk=bk, bv=bv),
        grid=(n, t // tm),
        in_specs=[pl.BlockSpec((1, tm, d), lambda b, i: (b, i, 0)),
                  full(norm1_g), full(w_qkv_t), full(w_rest), full(q_norm_g), full(k_norm_g),
                  full(w_gu_p), full(b_gu), table, table],
        out_specs=[head_out, head_out, head_out, row_out(bk), row_out(bk), row_out(bv),
                   row_out(bk), row_out(bv)],
        out_shape=[head_shape, head_shape, head_shape,
                   jax.ShapeDtypeStruct((n, t, bk), F32),
                   jax.ShapeDtypeStruct((n, t, bk), F32),
                   jax.ShapeDtypeStruct((n, t, bv), BF16),
                   jax.ShapeDtypeStruct((n, t, bk), F32),
                   jax.ShapeDtypeStruct((n, t, bv), BF16)],
        compiler_params=_cparams("parallel", "parallel"),
        name="proj",
    )(x, norm1_g, w_qkv_t, w_rest, q_norm_g, k_norm_g, w_gu_p, b_gu, cos_t, sin_t)


def _moba_kernel(qt_ref, kt_ref, vt_ref, o_ref, k_sc, vt_sc, kmean_sc, qs_sc, bias_sc, m_sc, acc_sc,
                 sa_sc, sb_sc, ma_sc, mb_sc, *, nb, blk, hd, hp, grp):
    c = pl.program_id(2)
    nq = hp * blk
    vr = hd + PAD_ROWS
    n_groups = nb // grp
    cols = [slice(i * blk, (i + 1) * blk) for i in range(hp)]

    @pl.when(c == 0)
    def _():
        row = lax.broadcasted_iota(jnp.int32, (LANES - hd, blk), 0)
        ones_row = (lax.broadcasted_iota(jnp.int32, (PAD_ROWS, blk), 0) == 0).astype(BF16)

        def stage_kv(j, carry):
            off = pl.multiple_of(j * blk, blk)
            place = (row == j % grp).astype(F32)
            for i in range(hp):
                k_nat = jnp.concatenate([kt_ref[0, i, :, pl.ds(off, blk)], place], axis=0).T
                k_sc[i, pl.ds(off, blk), :] = k_nat.astype(BF16)
                kmean_sc[i, pl.ds(j, 1), :] = jnp.mean(k_nat, axis=0, keepdims=True)
                vt_sc[i * vr:i * vr + hd, pl.ds(off, blk)] = vt_ref[0, i, :, pl.ds(off, blk)].astype(BF16)
                vt_sc[i * vr + hd:(i + 1) * vr, pl.ds(off, blk)] = ones_row
            return carry

        lax.fori_loop(0, nb, stage_kv, 0)

    zero_rows = jnp.zeros((LANES - hd, blk), F32)
    q_pad = [jnp.concatenate([qt_ref[0, i], zero_rows], axis=0) for i in range(hp)]
    gate = jnp.concatenate(
        [jnp.dot(kmean_sc[i], q_pad[i], precision=HIGHEST, preferred_element_type=F32)
         for i in range(hp)], axis=1)
    blk_row = lax.broadcasted_iota(jnp.int32, (nb, nq), 0)
    sel, _ = _top_mask(jnp.where(blk_row < c, gate, -jnp.inf), blk_row, min(MOBA_TOPK, nb - 1),
                       nb, axis=0)
    bias = (sel - 1.0) * -MASK_VALUE
    pad = jnp.zeros((PAD_ROWS - grp, blk), F32)
    for g in range(n_groups):
        for i in range(hp):
            bias_sc[g, i] = jnp.concatenate([bias[g * grp:(g + 1) * grp, cols[i]], pad],
                                            axis=0).astype(BF16)
    for i in range(hp):
        qs_sc[i] = (q_pad[i] * (hd ** -0.5 * LOG2_E)).astype(BF16)

    def scores(off, n_keys):
        return jnp.concatenate([_dot(k_sc[i, pl.ds(off, n_keys), :], qs_sc[i]) for i in range(hp)],
                               axis=1)

    def update(s, m_blk, off, n_keys, first):
        m_new = m_blk if first else jnp.maximum(m_sc[...], m_blk)
        pb = jnp.exp2(s - m_new).astype(BF16)
        if not first:
            alpha = jnp.exp2(m_sc[...] - m_new)
        m_sc[...] = m_new
        for i in range(hp):
            pv = _dot(vt_sc[i * vr:(i + 1) * vr, pl.ds(off, n_keys)], pb[:, cols[i]])
            acc_sc[:, cols[i]] = pv if first else alpha[:, cols[i]] * acc_sc[:, cols[i]] + pv

    own_off = pl.multiple_of(c * blk, blk)
    causal = (lax.broadcasted_iota(jnp.int32, (blk, nq), 0)
              <= lax.broadcasted_iota(jnp.int32, (blk, nq), 1) % blk)
    s_own = jnp.where(causal, scores(own_off, blk), MASK_VALUE)
    update(s_own, jnp.max(s_own, axis=0, keepdims=True), own_off, blk, True)

    n_grp = (c + grp - 1) // grp

    def stage(g, s_buf, mx_buf):
        g = jnp.minimum(g, n_groups - 1)
        for i in range(hp):
            qs_sc[i, hd:hd + PAD_ROWS, :] = bias_sc[g, i]
        s = scores(pl.multiple_of(g * (grp * blk), grp * blk), grp * blk)
        s_buf[...] = s
        mx_buf[...] = jnp.max(s, axis=0, keepdims=True)

    def consume(g, s_buf, mx_buf):
        update(s_buf[...], mx_buf[...], pl.multiple_of(g * (grp * blk), grp * blk), grp * blk, False)

    @pl.when(n_grp > 0)
    def _():
        stage(0, sa_sc, ma_sc)

    def body(u, carry):
        stage(2 * u + 1, sb_sc, mb_sc)
        consume(2 * u, sa_sc, ma_sc)
        stage(2 * u + 2, sa_sc, ma_sc)
        consume(2 * u + 1, sb_sc, mb_sc)
        return carry

    lax.fori_loop(0, n_grp // 2, body, 0)

    @pl.when(n_grp % 2 == 1)
    def _():
        consume(n_grp - 1, sa_sc, ma_sc)

    out_t = acc_sc[:hd, :] / acc_sc[hd:hd + 1, :]
    out_t = jnp.concatenate([out_t[:, cols[i]] for i in range(hp)], axis=0)
    o_ref[0] = out_t.T.astype(BF16)


def _moba_prompt(qt, kt, vt):
    n, heads, hd, t = qt.shape
    blk = MOBA_BLOCK
    assert t % blk == 0 and LANES % hd == 0
    hp = LANES // hd
    assert heads % hp == 0
    nb = t // blk
    grp = MOBA_GROUP if nb % MOBA_GROUP == 0 else 1
    nq = hp * blk
    kv_spec = pl.BlockSpec((1, hp, hd, t), lambda b, g, c: (b, g, 0, 0))
    return pl.pallas_call(
        functools.partial(_moba_kernel, nb=nb, blk=blk, hd=hd, hp=hp, grp=grp),
        grid=(n, heads // hp, nb),
        in_specs=[pl.BlockSpec((1, hp, hd, blk), lambda b, g, c: (b, g, 0, c)), kv_spec, kv_spec],
        out_specs=pl.BlockSpec((1, blk, hp * hd), lambda b, g, c: (b, c, g)),
        out_shape=jax.ShapeDtypeStruct((n, t, heads * hd), BF16),
        scratch_shapes=[pltpu.VMEM((hp, t, LANES), BF16),
                        pltpu.VMEM((hp * (hd + PAD_ROWS), t), BF16),
                        pltpu.VMEM((hp, nb, LANES), F32),
                        pltpu.VMEM((hp, LANES, blk), BF16),
                        pltpu.VMEM((nb // grp, hp, PAD_ROWS, blk), BF16),
                        pltpu.VMEM((1, nq), F32),
                        pltpu.VMEM((hd + PAD_ROWS, nq), F32),
                        pltpu.VMEM((grp * blk, nq), F32), pltpu.VMEM((grp * blk, nq), F32),
                        pltpu.VMEM((1, nq), F32), pltpu.VMEM((1, nq), F32)],
        compiler_params=_cparams("parallel", "parallel", "arbitrary"),
        name="moba_prompt",
    )(qt, kt, vt)


def _kmean_pages_kernel(pt_ref, *refs, n_pg, ppb, rows):
    del pt_ref
    out_ref = refs[n_pg]
    for b in range(n_pg // ppb):
        acc = refs[b * ppb][0, 0]
        for j in range(1, ppb):
            acc = acc + refs[b * ppb + j][0, 0]
        out_ref[0, b] = jnp.sum(acc, axis=-1) / rows


def _sample_kmean(cache_kt, layer, page_table, n_full, ppb):
    _, _, heads, hd, page = cache_kt.shape
    n, n_pages = page_table.shape
    n_pg = SAMPLE_KMEAN_PAGES
    assert (n_full * ppb) % n_pg == 0 and n_pg % ppb == 0

    def page_map(b, g, pt, j):
        return (layer, pt[b * n_pages + g * n_pg + j], 0, 0, 0)

    specs = [pl.BlockSpec((1, 1, heads, hd, page), functools.partial(page_map, j=j))
             for j in range(n_pg)]
    return pl.pallas_call(
        functools.partial(_kmean_pages_kernel, n_pg=n_pg, ppb=ppb, rows=float(ppb * page)),
        grid_spec=pltpu.PrefetchScalarGridSpec(
            num_scalar_prefetch=1, grid=(n, n_full * ppb // n_pg), in_specs=specs,
            out_specs=pl.BlockSpec((1, n_pg // ppb, heads, hd), lambda b, g, pt: (b, g, 0, 0))),
        out_shape=jax.ShapeDtypeStruct((n, n_full, heads, hd), F32),
        compiler_params=_cparams("parallel", "arbitrary"),
        name="sample_kmean",
    )(page_table.reshape(-1), *([cache_kt] * n_pg))


def _sample_pick_kernel(q_ref, km_ref, idx_ref, *, heads, n_full, n_sel):
    col = lax.broadcasted_iota(jnp.int32, (q_ref.shape[2], n_full), 1)
    for h in range(heads):
        gate = _dot_nt(q_ref[0, h], km_ref[0, h], precision=HIGHEST)
        _, picks = _top_mask(gate, col, n_sel, n_full, axis=1)
        idx_ref[0, h] = jnp.concatenate([p[0] for p in picks], axis=1)


def _sample_pick(q, kmean, n_sel):
    n, heads, t, hd = q.shape
    n_full = kmean.shape[2]
    return pl.pallas_call(
        functools.partial(_sample_pick_kernel, heads=heads, n_full=n_full, n_sel=n_sel),
        grid=(n,),
        in_specs=[pl.BlockSpec((1, heads, t, hd), lambda b: (b, 0, 0, 0)),
                  pl.BlockSpec((1, heads, n_full, hd), lambda b: (b, 0, 0, 0))],
        out_specs=pl.BlockSpec((1, heads, t, n_sel), lambda b: (b, 0, 0, 0)),
        out_shape=jax.ShapeDtypeStruct((n, heads, t, n_sel), jnp.int32),
        compiler_params=_cparams("parallel"),
        name="sample_pick",
    )(q, kmean)


def _sample_attn_kernel(ph_ref, q_ref, kn_ref, vn_ref, *refs, n_pg):
    del ph_ref
    tq, hd = q_ref.shape[2], q_ref.shape[3]
    k_refs, v_refs, o_ref = refs[:tq * n_pg], refs[tq * n_pg:2 * tq * n_pg], refs[2 * tq * n_pg]
    qs = (q_ref[0, 0] * hd ** -0.5).astype(BF16)
    own_ok = (lax.broadcasted_iota(jnp.int32, (tq, tq), 1)
              <= lax.broadcasted_iota(jnp.int32, (tq, tq), 0))
    s_own = jnp.where(own_ok, _dot_nt(qs, kn_ref[0, 0].astype(BF16)), MASK_VALUE)
    vn = vn_ref[0, 0].astype(BF16)
    for t in range(tq):
        pages = slice(t * n_pg, (t + 1) * n_pg)
        kt = jnp.concatenate([r[0, 0, 0] for r in k_refs[pages]], axis=1).astype(BF16)
        vt = jnp.concatenate([r[0, 0, 0] for r in v_refs[pages]], axis=1).astype(BF16)
        s_sel = _dot(qs, kt)
        m = jnp.maximum(jnp.max(s_sel, axis=1, keepdims=True), jnp.max(s_own, axis=1, keepdims=True))
        p_sel = jnp.exp(s_sel - m)
        p_own = jnp.exp(s_own - m)
        denom = jnp.sum(p_sel, axis=1, keepdims=True) + jnp.sum(p_own, axis=1, keepdims=True)
        o = (_dot_nt(p_sel.astype(BF16), vt) + _dot(p_own.astype(BF16), vn)) / denom
        o_ref[0, 0, t:t + 1, :] = o[t:t + 1]


def _sample_attn(q, k_new, v_new, cache_kt, cache_vt, layer, phys):
    n, heads, t, hd = q.shape
    page = cache_kt.shape[4]
    n_pg = phys.shape[-1]

    def page_map(b, h, ph, j):
        return (layer, ph[(b * heads + h) * (t * n_pg) + j], h, 0, 0)

    pages = [pl.BlockSpec((1, 1, 1, hd, page), functools.partial(page_map, j=j))
             for j in range(t * n_pg)]
    own = pl.BlockSpec((1, 1, t, hd), lambda b, h, ph: (b, h, 0, 0))
    return pl.pallas_call(
        functools.partial(_sample_attn_kernel, n_pg=n_pg),
        grid_spec=pltpu.PrefetchScalarGridSpec(
            num_scalar_prefetch=1, grid=(n, heads),
            in_specs=[own, own, own] + pages * 2, out_specs=own),
        out_shape=jax.ShapeDtypeStruct((n, heads, t, hd), F32),
        compiler_params=_cparams("parallel", "parallel"),
        name="sample_attn",
    )(phys.reshape(-1), q, k_new, v_new, *([cache_kt] * (t * n_pg)), *([cache_vt] * (t * n_pg)))


def _gla_kernel(q_ref, k_ref, v_ref, la_ref, gb_ref, gn_ref, o_ref, sfin_ref, st_sc, qk_sc, b_sc,
                la_sc, attn_sc, *, chunk, sub, n_chunk, heads, dk, dv):
    step = pl.program_id(1)

    @pl.when(step == 0)
    def _():
        st_sc[...] = jnp.zeros(st_sc.shape, F32)

    r_i = lax.broadcasted_iota(jnp.int32, (chunk, chunk), 0)
    c_i = lax.broadcasted_iota(jnp.int32, (chunk, chunk), 1)
    tri = r_i >= c_i
    tri_b = tri.astype(BF16)
    row_k = lax.broadcasted_iota(jnp.int32, (chunk, dk), 0)
    q_scale = dk ** -0.5

    def intra_fast(q, k, b):
        rows = []
        for i in range(chunk // sub):
            lo, hi = i * sub, (i + 1) * sub
            ref = b[lo - 1:lo, :] if i > 0 else jnp.zeros((1, dk), F32)
            qt = (q[lo:hi] * jnp.exp(b[lo:hi] - ref)).astype(BF16)
            kt = (k * jnp.exp(jnp.where(row_k < hi, ref - b, -jnp.inf))).astype(BF16)
            rows.append(_dot_nt(qt, kt))
        return jnp.where(tri, jnp.concatenate(rows, axis=0), 0.0)

    def intra_exact(q, k, b, la):
        qk_sc[...] = q
        b_sc[...] = b
        la_sc[...] = la
        col = lax.broadcasted_iota(jnp.int32, (1, chunk), 1)

        def row(t, carry):
            q_t = qk_sc[pl.ds(t, 1), :]
            la_t = la_sc[pl.ds(t, 1), :]
            before = b_sc[pl.ds(t, 1), :] - la_t
            kt = (k * jnp.exp(jnp.where(row_k < t, before - b, -jnp.inf))).astype(BF16)
            a_row = _dot_nt((q_t * jnp.exp(la_t)).astype(BF16), kt)
            k_t = jnp.sum(jnp.where(row_k == t, k, 0.0), axis=0, keepdims=True)
            diag = jnp.sum(q_t * k_t, axis=1, keepdims=True)
            attn_sc[pl.ds(t, 1), :] = jnp.where(col == t, diag, a_row)
            return carry

        lax.fori_loop(0, chunk, row, 0)
        return attn_sc[...]

    def one_chunk(ci, carry, exact):
        off = pl.multiple_of(ci * chunk, chunk)
        rows_ds = pl.ds(off, chunk)
        la = la_ref[0, rows_ds, :]
        la_hi = la.astype(BF16)
        la_mid = (la - la_hi.astype(F32)).astype(BF16)
        la_lo = (la - la_hi.astype(F32) - la_mid.astype(F32)).astype(BF16)
        b_all = _dot(tri_b, la_hi) + _dot(tri_b, la_mid) + _dot(tri_b, la_lo)
        for h in range(heads):
            kc = slice(h * dk, (h + 1) * dk)
            vc = slice(h * dv, (h + 1) * dv)
            q = q_ref[0, rows_ds, kc] * q_scale
            k = k_ref[0, rows_ds, kc]
            v = v_ref[0, rows_ds, vc]
            b = b_all[:, kc]
            st = st_sc[h]
            o = _dot_nt((q * jnp.exp(b)).astype(BF16), st.astype(BF16))
            attn = intra_exact(q, k, b, la[:, kc]) if exact else intra_fast(q, k, b)
            o = o + _dot(attn.astype(BF16), v)
            b_last = b[chunk - 1:chunk, :]
            k_dec = (k * jnp.exp(b_last - b)).astype(BF16)
            st_sc[h] = st * jnp.exp(b_last) + _dot_tn(v, k_dec)
            gb = gb_ref[0, rows_ds, vc].astype(F32)
            o_ref[0, rows_ds, vc] = (_rms(o) * gn_ref[...] * (gb * jax.nn.sigmoid(gb))).astype(BF16)
        return carry

    safe = jnp.min(la_ref[0]) * sub > GLA_FAST_MIN_LOG_DECAY

    @pl.when(safe)
    def _():
        lax.fori_loop(0, n_chunk, functools.partial(one_chunk, exact=False), 0)

    @pl.when(jnp.logical_not(safe))
    def _():
        lax.fori_loop(0, n_chunk, functools.partial(one_chunk, exact=True), 0)

    @pl.when(step == pl.num_programs(1) - 1)
    def _():
        sfin_ref[0] = st_sc[...]


def _gla_prompt(qb, kb, vb, la, gb, gla_norm_g, *, heads, dk, dv):
    n, t, _ = qb.shape
    rows = min(GLA_STEP_ROWS, t)
    chunk = min(GLA_CHUNK, rows)
    assert t % rows == 0 and rows % chunk == 0 and chunk % GLA_SUB == 0
    kspec = pl.BlockSpec((1, rows, heads * dk), lambda b, s: (b, s, 0))
    vspec = pl.BlockSpec((1, rows, heads * dv), lambda b, s: (b, s, 0))
    return pl.pallas_call(
        functools.partial(_gla_kernel, chunk=chunk, sub=GLA_SUB, n_chunk=rows // chunk,
                          heads=heads, dk=dk, dv=dv),
        grid=(n, t // rows),
        in_specs=[kspec, kspec, vspec, kspec, vspec, pl.BlockSpec((1, dv), lambda b, s: (0, 0))],
        out_specs=[vspec, pl.BlockSpec((1, heads, dv, dk), lambda b, s: (b, 0, 0, 0))],
        out_shape=[jax.ShapeDtypeStruct((n, t, heads * dv), BF16),
                   jax.ShapeDtypeStruct((n, heads, dv, dk), F32)],
        scratch_shapes=[pltpu.VMEM((heads, dv, dk), F32), pltpu.VMEM((chunk, dk), F32),
                        pltpu.VMEM((chunk, dk), F32), pltpu.VMEM((chunk, dk), F32),
                        pltpu.VMEM((chunk, chunk), F32)],
        compiler_params=_cparams("parallel", "arbitrary"),
        name="gla_prompt",
    )(qb, kb, vb, la, gb, gla_norm_g)


def _gla_sample_kernel(qt_ref, kt_ref, lat_ref, v_ref, gb_ref, gn_ref, s0_ref, o_ref, sfin_ref, *, dk):
    s = s0_ref[0, 0, 0]
    qt = qt_ref[0, 0] * dk ** -0.5
    kt = kt_ref[0, 0]
    at = jnp.exp(lat_ref[0, 0])
    v = v_ref[0].astype(F32)
    outs = []
    for t in range(v.shape[0]):
        s = at[:, t:t + 1] * s + kt[:, t:t + 1] * v[t:t + 1, :]
        outs.append(jnp.sum(qt[:, t:t + 1] * s, axis=0, keepdims=True))
    o = jnp.concatenate(outs, axis=0)
    gb = gb_ref[0].astype(F32)
    o_ref[0] = (_rms(o) * gn_ref[...] * (gb * jax.nn.sigmoid(gb))).astype(BF16)
    sfin_ref[0, 0] = s


def _gla_sample(qt, kt, lat, vb, gb, gla_norm_g, state, layer, *, dk, dv):
    n, heads, _, t = qt.shape
    tspec = pl.BlockSpec((1, 1, dk, t), lambda b, h: (b, h, 0, 0))
    vspec = pl.BlockSpec((1, t, dv), lambda b, h: (b, 0, h))
    return pl.pallas_call(
        functools.partial(_gla_sample_kernel, dk=dk),
        grid=(n, heads),
        in_specs=[tspec, tspec, tspec, vspec, vspec, pl.BlockSpec((1, dv), lambda b, h: (0, 0)),
                  pl.BlockSpec((1, 1, 1, dk, dv), lambda b, h: (layer, b, h, 0, 0))],
        out_specs=[vspec, pl.BlockSpec((1, 1, dk, dv), lambda b, h: (b, h, 0, 0))],
        out_shape=[jax.ShapeDtypeStruct((n, t, heads * dv), BF16),
                   jax.ShapeDtypeStruct((n, heads, dk, dv), F32)],
        compiler_params=_cparams("parallel", "parallel"),
        name="gla_sample",
    )(qt, kt, lat, vb, gb, gla_norm_g, state)


def _merge_kernel(x_ref, oa_ref, ob_ref, g1_ref, wa_ref, wb_ref, wmg_ref, wo_ref, g2_ref, wrh_ref,
                  wrl_ref, br_ref, x2_ref, h2_ref, idx_ref, gate_ref, *, d, n_exp):
    x = x_ref[...]
    hb = (_rms(x) * g1_ref[...]).astype(BF16)
    y_a = _dot(oa_ref[...], wa_ref[...])
    y_b = _dot(ob_ref[...], wb_ref[...])
    mixed = (jax.nn.sigmoid(_dot(hb, wmg_ref[:, :d])) * y_a
             + jax.nn.sigmoid(_dot(hb, wmg_ref[:, d:])) * y_b)
    x2 = x + _dot(mixed.astype(BF16), wo_ref[...])
    x2_ref[...] = x2
    h2 = _rms(x2) * g2_ref[...]
    h2_hi = h2.astype(BF16)
    h2_ref[...] = h2_hi
    h2_lo = (h2 - h2_hi.astype(F32)).astype(BF16)
    logits = (_dot(h2_hi, wrh_ref[...]) + _dot(h2_hi, wrl_ref[...]) + _dot(h2_lo, wrh_ref[...])
              + br_ref[...])
    col = lax.broadcasted_iota(jnp.int32, logits.shape, 1)
    _, picks = _top_mask(logits, col, MOE_TOP_K, n_exp, axis=1)
    top_val = jnp.concatenate([p[1] for p in picks], axis=1)
    e = jnp.exp(top_val - top_val[:, :1])
    idx_ref[...] = jnp.concatenate([p[0] for p in picks], axis=1)
    gate_ref[...] = e / jnp.sum(e, axis=1, keepdims=True)


def _merge(x, o_a, o_b, norm1_g, w_a, w_b, w_mg, w_o, norm2_g, w_router, b_router):
    n_tok, d = x.shape
    n_exp = w_router.shape[1]
    w_router_hi = w_router.astype(BF16)
    w_router_lo = (w_router - w_router_hi.astype(F32)).astype(BF16)
    tm = min(PROJ_ROWS, n_tok)
    assert n_tok % tm == 0
    full = lambda a: pl.BlockSpec(a.shape, lambda i: (0,) * a.ndim)
    rows = lambda w: pl.BlockSpec((tm, w), lambda i: (i, 0))
    return pl.pallas_call(
        functools.partial(_merge_kernel, d=d, n_exp=n_exp),
        grid=(n_tok // tm,),
        in_specs=[rows(d), rows(o_a.shape[1]), rows(o_b.shape[1]), full(norm1_g), full(w_a),
                  full(w_b), full(w_mg), full(w_o), full(norm2_g), full(w_router_hi),
                  full(w_router_lo), full(b_router)],
        out_specs=[rows(d), rows(d), rows(MOE_TOP_K), rows(MOE_TOP_K)],
        out_shape=[jax.ShapeDtypeStruct((n_tok, d), F32), jax.ShapeDtypeStruct((n_tok, d), BF16),
                   jax.ShapeDtypeStruct((n_tok, MOE_TOP_K), jnp.int32),
                   jax.ShapeDtypeStruct((n_tok, MOE_TOP_K), F32)],
        compiler_params=_cparams("parallel"),
        name="merge",
    )(x, o_a, o_b, norm1_g, w_a, w_b, w_mg, w_o, norm2_g, w_router_hi, w_router_lo, b_router)


def _expert_kernel(te_ref, nt_ref, x_ref, wup_ref, bg_ref, bl_ref, wdn_ref, bdn_ref, y_ref,
                   wg_sc, wl_sc, wd_sc, t_sc, *, slab):
    i = pl.program_id(0)

    @pl.when((i == 0) | (te_ref[i] != te_ref[jnp.maximum(i - 1, 0)]))
    def _():
        for c in range(0, wup_ref.shape[2], slab):
            rows = slice(c // 2, (c + slab) // 2)
            for j in range(t_sc.shape[0]):
                lanes = slice(j * LANES, (j + 1) * LANES)
                t_sc[j] = wup_ref[0, lanes, c:c + slab].T
                wg_sc[rows, lanes] = t_sc[j, pl.ds(0, slab // 2, stride=2), :].astype(BF16)
                wl_sc[rows, lanes] = t_sc[j, pl.ds(1, slab // 2, stride=2), :].astype(BF16)
        wd_sc[...] = wdn_ref[0].astype(BF16)

    @pl.when(i < nt_ref[0])
    def _():
        x = x_ref[...]
        x_glu = jnp.minimum(_dot_nt(x, wg_sc[...]) + bg_ref[0], SWIGLU_LIMIT)
        x_lin = jnp.clip(_dot_nt(x, wl_sc[...]) + bl_ref[0], -SWIGLU_LIMIT, SWIGLU_LIMIT)
        a = x_glu * jax.nn.sigmoid(SWIGLU_ALPHA * x_glu) * (x_lin + 1.0)
        y_ref[...] = (_dot(a.astype(BF16), wd_sc[...]) + bdn_ref[0]).astype(y_ref.dtype)

    @pl.when(i >= nt_ref[0])
    def _():
        y_ref[...] = jnp.zeros(y_ref.shape, y_ref.dtype)


def _experts(buf, tile_e, n_used, w_up, b_glu, b_lin, w_dn, b_dn, *, tile_rows):
    r, d = buf.shape
    d_ff = w_dn.shape[1]
    slab = min(2 * LANES, 2 * d_ff)
    by_expert = lambda *blk: pl.BlockSpec((1,) + blk, lambda i, te, nt: (te[i], 0, 0))
    return pl.pallas_call(
        functools.partial(_expert_kernel, slab=slab),
        grid_spec=pltpu.PrefetchScalarGridSpec(
            num_scalar_prefetch=2, grid=(r // tile_rows,),
            in_specs=[pl.BlockSpec((tile_rows, d), lambda i, te, nt: (i, 0)),
                      by_expert(d, 2 * d_ff), by_expert(1, d_ff), by_expert(1, d_ff),
                      by_expert(d_ff, d), by_expert(1, d)],
            out_specs=pl.BlockSpec((tile_rows, d), lambda i, te, nt: (i, 0)),
            scratch_shapes=[pltpu.VMEM((d_ff, d), BF16), pltpu.VMEM((d_ff, d), BF16),
                            pltpu.VMEM((d_ff, d), BF16), pltpu.VMEM((d // LANES, slab, LANES), F32)]),
        out_shape=jax.ShapeDtypeStruct((r, d), BF16),
        compiler_params=_cparams("arbitrary"),
        name="experts",
    )(tile_e, n_used, buf, w_up, b_glu, b_lin, w_dn, b_dn)


def _rank_kernel(idx_ref, rank_ref, cnt_ref, seen_sc):
    @pl.when(pl.program_id(0) == 0)
    def _():
        seen_sc[...] = jnp.zeros(seen_sc.shape, F32)

    idx = idx_ref[...]
    tr, top_k = idx.shape
    lane = lax.broadcasted_iota(jnp.int32, (tr, LANES), 1)
    hits = [idx[:, k:k + 1] == lane for k in range(top_k)]
    routed = hits[0].astype(F32)
    for h in hits[1:]:
        routed = routed + h.astype(F32)
    earlier = (lax.broadcasted_iota(jnp.int32, (tr, tr), 1)
               < lax.broadcasted_iota(jnp.int32, (tr, tr), 0)).astype(BF16)
    before = _dot(earlier, routed.astype(BF16)) + seen_sc[...]
    rank_ref[...] = jnp.concatenate(
        [jnp.sum(jnp.where(h, before, 0.0), axis=1, keepdims=True) for h in hits],
        axis=1).astype(jnp.int32)
    seen_sc[...] = seen_sc[...] + jnp.sum(routed, axis=0, keepdims=True)
    cnt_ref[...] = seen_sc[...]


def _rank(top_idx):
    n_tok, top_k = top_idx.shape
    tr = min(ROUTE_ROWS, n_tok)
    assert n_tok % tr == 0
    return pl.pallas_call(
        _rank_kernel,
        grid=(n_tok // tr,),
        in_specs=[pl.BlockSpec((tr, top_k), lambda i: (i, 0))],
        out_specs=[pl.BlockSpec((tr, top_k), lambda i: (i, 0)),
                   pl.BlockSpec((1, LANES), lambda i: (0, 0))],
        out_shape=[jax.ShapeDtypeStruct((n_tok, top_k), jnp.int32),
                   jax.ShapeDtypeStruct((1, LANES), F32)],
        scratch_shapes=[pltpu.VMEM((1, LANES), F32)],
        compiler_params=_cparams("arbitrary"),
        name="moe_rank",
    )(top_idx)


def _combine_kernel(x_ref, y_ref, g_ref, o_ref):
    acc = x_ref[...]
    g = g_ref[...]
    for k in range(y_ref.shape[0]):
        acc = acc + g[:, k:k + 1] * y_ref[k].astype(F32)
    o_ref[...] = acc


def _combine(x2, y_kt, gate):
    top_k, n_tok, d = y_kt.shape
    tm = min(ROUTE_ROWS, n_tok)
    assert n_tok % tm == 0
    return pl.pallas_call(
        _combine_kernel,
        grid=(n_tok // tm,),
        in_specs=[pl.BlockSpec((tm, d), lambda i: (i, 0)),
                  pl.BlockSpec((top_k, tm, d), lambda i: (0, i, 0)),
                  pl.BlockSpec((tm, top_k), lambda i: (i, 0))],
        out_specs=pl.BlockSpec((tm, d), lambda i: (i, 0)),
        out_shape=jax.ShapeDtypeStruct((n_tok, d), F32),
        compiler_params=_cparams("parallel"),
        name="moe_combine",
    )(x2, y_kt, gate)


def _moe(x2, h2, top_idx, gate, expert_weights, tile_rows):
    n_tok, d = h2.shape
    n_exp = expert_weights[0].shape[0]
    assert n_exp <= LANES
    n_assign = n_tok * MOE_TOP_K
    rank, cnt = _rank(top_idx)
    counts = cnt[0, :n_exp].astype(jnp.int32)
    padded = (counts + tile_rows - 1) // tile_rows * tile_rows
    pend = jnp.cumsum(padded)
    experts = jnp.arange(n_exp, dtype=jnp.int32)
    first_row = jnp.sum(jnp.where(top_idx[:, :, None] == experts, pend - padded, 0), axis=-1)
    dest = (first_row + rank).reshape(-1)
    n_tiles = -(-n_assign // tile_rows) + n_exp
    tile_e = jnp.minimum(jnp.searchsorted(pend, jnp.arange(n_tiles, dtype=jnp.int32) * tile_rows,
                                          side='right'), n_exp - 1).astype(jnp.int32)
    n_used = (pend[-1:] // tile_rows).astype(jnp.int32)
    src = jnp.zeros((n_tiles * tile_rows,), jnp.int32).at[dest].set(
        jnp.arange(n_assign, dtype=jnp.int32) // MOE_TOP_K)
    y = _experts(h2[src], tile_e, n_used, *expert_weights, tile_rows=tile_rows)
    y_kt = y[dest.reshape(n_tok, MOE_TOP_K).T]
    return _combine(x2, y_kt, gate)


def _rope_tables_t(pos, hd):
    half = hd // 2
    inv = ROPE_THETA ** (-jnp.arange(half, dtype=F32) / half)
    ang = inv[:, None] * pos.astype(F32)[None, :]
    cos, sin = jnp.cos(ang), jnp.sin(ang)
    return jnp.concatenate([cos, cos], axis=0), jnp.concatenate([-sin, sin], axis=0)


def kernel(x_prompt, x_sample, cache_k, cache_v, state_gla, page_table, norm1_g, w_in, q_norm_g,
           k_norm_g, w_gate_up, b_gate_up, gla_norm_g, w_branch_a, w_branch_b, w_merge_gate, w_out,
           norm2_g, w_router, b_router, w_up, b_up, w_down, b_down):
    depth = norm1_g.shape[0]
    assert depth == 1, "sample/prompt streams are only chained through one layer here"
    layer = 0
    n_p, t_p, d = x_prompt.shape
    n_s, t_s, _ = x_sample.shape
    heads, page, hd = cache_k.shape[2:]
    b_heads, dk, dv = state_gla.shape[2:]
    bk, bv = b_heads * dk, b_heads * dv
    aqk = heads * hd
    rank = w_gate_up.shape[1]
    past_len = page_table.shape[1] * page
    ppb = MOBA_BLOCK // page
    n_full = past_len // MOBA_BLOCK
    assert past_len % MOBA_BLOCK == 0 and n_full >= MOBA_TOPK and rank <= LANES
    assert w_in.shape[2] == 3 * aqk + 2 * bk + 2 * bv + rank

    w_qkv_t = w_in[layer][:, :3 * aqk].T.astype(BF16)
    w_rest = jnp.pad(w_in[layer][:, 3 * aqk:], ((0, 0), (0, LANES - rank))).astype(BF16)
    w_gu_p = jnp.pad(w_gate_up[layer], ((0, LANES - rank), (0, 0))).astype(BF16)
    g1 = norm1_g[layer][None, :]
    g2 = norm2_g[layer][None, :]
    qg = q_norm_g[layer][:, None]
    kg = k_norm_g[layer][:, None]
    bgu = b_gate_up[layer][None, :]
    gn = gla_norm_g[layer][None, :]
    w_a = w_branch_a[layer].astype(BF16)
    w_b = w_branch_b[layer].astype(BF16)
    w_mg = w_merge_gate[layer].astype(BF16)
    w_o = w_out[layer].astype(BF16)
    w_r = w_router[layer]
    b_r = b_router[layer][None, :]
    expert_weights = (w_up[layer], b_up[layer][:, None, 0::2], b_up[layer][:, None, 1::2],
                      w_down[layer], b_down[layer][:, None, :])
    proj = functools.partial(_project, norm1_g=g1, w_qkv_t=w_qkv_t, w_rest=w_rest, q_norm_g=qg,
                             k_norm_g=kg, w_gu_p=w_gu_p, b_gu=bgu, heads=heads, hd=hd, bk=bk, bv=bv)
    merge = functools.partial(_merge, norm1_g=g1, w_a=w_a, w_b=w_b, w_mg=w_mg, w_o=w_o,
                              norm2_g=g2, w_router=w_r, b_router=b_r)

    cos_p, sin_p = _rope_tables_t(jnp.arange(t_p), hd)
    qt, kt, vt, qb, kb, vb, la, gb = proj(x_prompt, cos_p, sin_p)
    o_a = _moba_prompt(qt, kt, vt)
    o_b, st_p = _gla_prompt(qb, kb, vb, la, gb, gn, heads=b_heads, dk=dk, dv=dv)
    s_prompt = jnp.swapaxes(st_p, 2, 3)
    x2_p, h2_p, idx_p, gate_p = merge(x_prompt.reshape(n_p * t_p, d), o_a.reshape(n_p * t_p, -1),
                                      o_b.reshape(n_p * t_p, -1))

    n_tok_s = n_s * t_s
    cos_s, sin_s = _rope_tables_t(past_len + jnp.arange(t_s), hd)
    qt_s, kt_s, vt_s, qb_s, kb_s, vb_s, la_s, gb_s = proj(
        x_sample.reshape(1, n_tok_s, d), jnp.tile(cos_s, (1, n_s)), jnp.tile(sin_s, (1, n_s)))
    per_seq = lambda a: a.reshape(heads, hd, n_s, t_s).transpose(2, 0, 3, 1)
    qa_s, ka_s, va_s = per_seq(qt_s), per_seq(kt_s), per_seq(vt_s)
    cache_kt = jnp.swapaxes(cache_k, 3, 4)
    cache_vt = jnp.swapaxes(cache_v, 3, 4)
    kmean = _sample_kmean(cache_kt, layer, page_table, n_full, ppb).transpose(0, 2, 1, 3)
    blk_idx = _sample_pick(qa_s, kmean, MOBA_TOPK)
    lpages = blk_idx[..., None] * ppb + jnp.arange(ppb, dtype=jnp.int32)
    phys = jnp.take_along_axis(page_table[:, None, None, :],
                               lpages.reshape(n_s, heads, t_s, MOBA_TOPK * ppb), axis=-1)
    o_a_s = _sample_attn(qa_s, ka_s, va_s, cache_kt, cache_vt, layer, phys)
    o_a_s = o_a_s.transpose(0, 2, 1, 3).reshape(n_tok_s, aqk).astype(BF16)
    tr = lambda a: a.reshape(n_s, t_s, b_heads, dk).transpose(0, 2, 3, 1)
    o_b_s, s_sample = _gla_sample(tr(qb_s), tr(kb_s), tr(la_s), vb_s.reshape(n_s, t_s, bv),
                                  gb_s.reshape(n_s, t_s, bv), gn, state_gla, layer, dk=dk, dv=dv)
    x2_s, h2_s, idx_s, gate_s = merge(x_sample.reshape(n_tok_s, d), o_a_s, o_b_s.reshape(n_tok_s, bv))

    y_prompt = _moe(x2_p, h2_p, idx_p, gate_p, expert_weights, MOE_ROWS).reshape(n_p, t_p, d)
    y_sample = _moe(x2_s, h2_s, idx_s, gate_s, expert_weights,
                    min(MOE_ROWS, n_tok_s)).reshape(n_s, t_s, d)
    k_prompt = jnp.swapaxes(kt, 2, 3)
    v_prompt = jnp.swapaxes(vt, 2, 3)
    return (y_prompt, y_sample, k_prompt[None], v_prompt[None], ka_s[None], va_s[None],
            s_prompt[None], s_sample[None])
```

```python
import functools

import jax
import jax.numpy as jnp
from jax import lax
from jax.experimental import pallas as pl
from jax.experimental.pallas import tpu as pltpu

F32 = jnp.float32
BF16 = jnp.bfloat16
HIGHEST = lax.Precision.HIGHEST

MOBA_BLOCK = 256
MOBA_TOPK = 3
ROPE_THETA = 10000.0
GLA_GATE_NORM = 16.0
NORM_EPS = 1e-6
MOE_TOP_K = 4
SWIGLU_ALPHA = 1.702
SWIGLU_LIMIT = 7.0

LANES = 128
PAD_ROWS = 16
VMEM_LIMIT_BYTES = 56 * 1024 * 1024

MASK_VALUE = -1e30
LOG2_E = 1.4426950408889634
PROJ_ROWS = 512
GLA_CHUNK = 64
GLA_SUB = 16
GLA_STEP_ROWS = 512
GLA_FAST_MIN_LOG_DECAY = -80.0
MOE_ROWS = 512
ROUTE_ROWS = 1024
PROMPT_SPLITS = 2
MOBA_GROUP = 4
SAMPLE_KMEAN_PAGES = 16


def _cparams(*sem):
    return pltpu.CompilerParams(dimension_semantics=sem, vmem_limit_bytes=VMEM_LIMIT_BYTES)


def _rms(x):
    return x * lax.rsqrt(jnp.mean(x * x, axis=-1, keepdims=True) + NORM_EPS)


def _dot(a, b):
    return jnp.dot(a, b, preferred_element_type=F32)


def _dot_nt(a, b, precision=None):
    return lax.dot_general(a, b, (((1,), (1,)), ((), ())), precision=precision,
                           preferred_element_type=F32)


def _dot_tn(a, b):
    return lax.dot_general(a, b, (((0,), (0,)), ((), ())), preferred_element_type=F32)


def _top_mask(g, pos, n_pick, n_pos, axis):
    sel = jnp.zeros(g.shape, F32)
    picks = []
    for _ in range(n_pick):
        mx = jnp.max(g, axis=axis, keepdims=True)
        cand = jnp.where((g == mx) & (mx > -jnp.inf), pos, n_pos)
        idx = jnp.min(cand, axis=axis, keepdims=True)
        pick = pos == idx
        sel = jnp.where(pick, 1.0, sel)
        g = jnp.where(pick, -jnp.inf, g)
        picks.append((idx, mx))
    return sel, picks


def _proj_kernel(x_ref, g1_ref, wqkv_ref, wrest_ref, qg_ref, kg_ref, wgu_ref, bgu_ref, cos_ref,
                 sin_ref, qt_ref, kt_ref, vt_ref, qb_ref, kb_ref, vb_ref, la_ref, gb_ref,
                 *, heads, hd, bk, bv):
    x = x_ref[0]
    hb = (_rms(x) * g1_ref[...]).astype(BF16)
    cos = cos_ref[...]
    sin = sin_ref[...]
    aqk = heads * hd

    def mm_t(lo, hi):
        return _dot_nt(wqkv_ref[lo:hi, :], hb)

    def norm_rope_t(z, g_ref, out_ref):
        for h in range(heads):
            y = z[h * hd:(h + 1) * hd, :]
            y = y * lax.rsqrt(jnp.mean(y * y, axis=0, keepdims=True) + NORM_EPS) * g_ref[...]
            rot = jnp.concatenate([y[hd // 2:], y[:hd // 2]], axis=0)
            out_ref[0, h] = y * cos + rot * sin

    norm_rope_t(mm_t(0, aqk), qg_ref, qt_ref)
    norm_rope_t(mm_t(aqk, 2 * aqk), kg_ref, kt_ref)
    zv = mm_t(2 * aqk, 3 * aqk)
    for h in range(heads):
        vt_ref[0, h] = zv[h * hd:(h + 1) * hd, :]

    def mm(lo, hi):
        return _dot(hb, wrest_ref[:, lo:hi])

    qb_ref[0] = mm(0, bk)
    kb_ref[0] = mm(bk, 2 * bk)
    vb_ref[0] = mm(2 * bk, 2 * bk + bv).astype(BF16)
    gb_ref[0] = mm(2 * bk + bv, 2 * bk + 2 * bv).astype(BF16)
    lr = mm(2 * bk + 2 * bv, 2 * bk + 2 * bv + LANES)
    u = _dot(lr.astype(BF16), wgu_ref[...]) + bgu_ref[...]
    log_sig = jnp.minimum(u, 0.0) - jnp.log(1.0 + jnp.exp(-jnp.abs(u)))
    la_ref[0] = log_sig / GLA_GATE_NORM


def _project(x, cos_t, sin_t, norm1_g, w_qkv_t, w_rest, q_norm_g, k_norm_g, w_gu_p, b_gu,
             *, heads, hd, bk, bv):
    n, t, d = x.shape
    tm = min(PROJ_ROWS, t)
    assert t % tm == 0
    full = lambda a: pl.BlockSpec(a.shape, lambda b, i: (0,) * a.ndim)
    head_out = pl.BlockSpec((1, heads, hd, tm), lambda b, i: (b, 0, 0, i))
    row_out = lambda w: pl.BlockSpec((1, tm, w), lambda b, i: (b, i, 0))
    table = pl.BlockSpec((hd, tm), lambda b, i: (0, i))
    head_shape = jax.ShapeDtypeStruct((n, heads, hd, t), F32)
    return pl.pallas_call(
        functools.partial(_proj_kernel, heads=heads, hd=hd, bk=bk, bv=bv),
        grid=(n, t // tm),
        in_specs=[pl.BlockSpec((1, tm, d), lambda b, i: (b, i, 0)),
                  full(norm1_g), full(w_qkv_t), full(w_rest), full(q_norm_g), full(k_norm_g),
                  full(w_gu_p), full(b_gu), table, table],
        out_specs=[head_out, head_out, head_out, row_out(bk), row_out(bk), row_out(bv),
                   row_out(bk), row_out(bv)],
        out_shape=[head_shape, head_shape, head_shape,
                   jax.ShapeDtypeStruct((n, t, bk), F32),
                   jax.ShapeDtypeStruct((n, t, bk), F32),
                   jax.ShapeDtypeStruct((n, t, bv), BF16),
                   jax.ShapeDtypeStruct((n, t, bk), F32),
                   jax.ShapeDtypeStruct((n, t, bv), BF16)],
        compiler_params=_cparams("parallel", "parallel"),
        name="proj",
    )(x, norm1_g, w_qkv_t, w_rest, q_norm_g, k_norm_g, w_gu_p, b_gu, cos_t, sin_t)


def _moba_kernel(qt_ref, kt_ref, vt_ref, o_ref, k_sc, vt_sc, kmean_sc, qs_sc, bias_sc, m_sc, acc_sc,
                 sa_sc, sb_sc, ma_sc, mb_sc, *, nb, blk, hd, hp, grp):
    c = pl.program_id(2)
    nq = hp * blk
    vr = hd + PAD_ROWS
    n_groups = nb // grp
    cols = [slice(i * blk, (i + 1) * blk) for i in range(hp)]

    @pl.when(c == 0)
    def _():
        row = lax.broadcasted_iota(jnp.int32, (LANES - hd, blk), 0)
        ones_row = (lax.broadcasted_iota(jnp.int32, (PAD_ROWS, blk), 0) == 0).astype(BF16)

        def stage_kv(j, carry):
            off = pl.multiple_of(j * blk, blk)
            place = (row == j % grp).astype(F32)
            for i in range(hp):
                k_nat = jnp.concatenate([kt_ref[0, i, :, pl.ds(off, blk)], place], axis=0).T
                k_sc[i, pl.ds(off, blk), :] = k_nat.astype(BF16)
                kmean_sc[i, pl.ds(j, 1), :] = jnp.mean(k_nat, axis=0, keepdims=True)
                vt_sc[i * vr:i * vr + hd, pl.ds(off, blk)] = vt_ref[0, i, :, pl.ds(off, blk)].astype(BF16)
                vt_sc[i * vr + hd:(i + 1) * vr, pl.ds(off, blk)] = ones_row
            return carry

        lax.fori_loop(0, nb, stage_kv, 0)

    zero_rows = jnp.zeros((LANES - hd, blk), F32)
    q_pad = [jnp.concatenate([qt_ref[0, i], zero_rows], axis=0) for i in range(hp)]
    gate = jnp.concatenate(
        [jnp.dot(kmean_sc[i], q_pad[i], precision=HIGHEST, preferred_element_type=F32)
         for i in range(hp)], axis=1)
    blk_row = lax.broadcasted_iota(jnp.int32, (nb, nq), 0)
    sel, _ = _top_mask(jnp.where(blk_row < c, gate, -jnp.inf), blk_row, min(MOBA_TOPK, nb - 1),
                       nb, axis=0)
    bias = (sel - 1.0) * -MASK_VALUE
    pad = jnp.zeros((PAD_ROWS - grp, blk), F32)
    for g in range(n_groups):
        for i in range(hp):
            bias_sc[g, i] = jnp.concatenate([bias[g * grp:(g + 1) * grp, cols[i]], pad],
                                            axis=0).astype(BF16)
    for i in range(hp):
        qs_sc[i] = (q_pad[i] * (hd ** -0.5 * LOG2_E)).astype(BF16)

    def scores(off, n_keys):
        return jnp.concatenate([_dot(k_sc[i, pl.ds(off, n_keys), :], qs_sc[i]) for i in range(hp)],
                               axis=1)

    def update(s, m_blk, off, n_keys, first):
        m_new = m_blk if first else jnp.maximum(m_sc[...], m_blk)
        pb = jnp.exp2(s - m_new).astype(BF16)
        if not first:
            alpha = jnp.exp2(m_sc[...] - m_new)
        m_sc[...] = m_new
        for i in range(hp):
            pv = _dot(vt_sc[i * vr:(i + 1) * vr, pl.ds(off, n_keys)], pb[:, cols[i]])
            acc_sc[:, cols[i]] = pv if first else alpha[:, cols[i]] * acc_sc[:, cols[i]] + pv

    own_off = pl.multiple_of(c * blk, blk)
    causal = (lax.broadcasted_iota(jnp.int32, (blk, nq), 0)
              <= lax.broadcasted_iota(jnp.int32, (blk, nq), 1) % blk)
    s_own = jnp.where(causal, scores(own_off, blk), MASK_VALUE)
    update(s_own, jnp.max(s_own, axis=0, keepdims=True), own_off, blk, True)

    n_grp = (c + grp - 1) // grp

    def stage(g, s_buf, mx_buf):
        g = jnp.minimum(g, n_groups - 1)
        for i in range(hp):
            qs_sc[i, hd:hd + PAD_ROWS, :] = bias_sc[g, i]
        s = scores(pl.multiple_of(g * (grp * blk), grp * blk), grp * blk)
        s_buf[...] = s
        mx_buf[...] = jnp.max(s, axis=0, keepdims=True)

    def consume(g, s_buf, mx_buf):
        update(s_buf[...], mx_buf[...], pl.multiple_of(g * (grp * blk), grp * blk), grp * blk, False)

    @pl.when(n_grp > 0)
    def _():
        stage(0, sa_sc, ma_sc)

    def body(u, carry):
        stage(2 * u + 1, sb_sc, mb_sc)
        consume(2 * u, sa_sc, ma_sc)
        stage(2 * u + 2, sa_sc, ma_sc)
        consume(2 * u + 1, sb_sc, mb_sc)
        return carry

    lax.fori_loop(0, n_grp // 2, body, 0)

    @pl.when(n_grp % 2 == 1)
    def _():
        consume(n_grp - 1, sa_sc, ma_sc)

    out_t = acc_sc[:hd, :] / acc_sc[hd:hd + 1, :]
    out_t = jnp.concatenate([out_t[:, cols[i]] for i in range(hp)], axis=0)
    o_ref[0] = out_t.T.astype(BF16)


def _moba_prompt(qt, kt, vt, b0, n):
    _, heads, hd, t = qt.shape
    blk = MOBA_BLOCK
    assert t % blk == 0 and LANES % hd == 0
    hp = LANES // hd
    assert heads % hp == 0
    nb = t // blk
    grp = MOBA_GROUP if nb % MOBA_GROUP == 0 else 1
    nq = hp * blk
    kv_spec = pl.BlockSpec((1, hp, hd, t), lambda b, g, c: (b + b0, g, 0, 0))
    return pl.pallas_call(
        functools.partial(_moba_kernel, nb=nb, blk=blk, hd=hd, hp=hp, grp=grp),
        grid=(n, heads // hp, nb),
        in_specs=[pl.BlockSpec((1, hp, hd, blk), lambda b, g, c: (b + b0, g, 0, c)), kv_spec, kv_spec],
        out_specs=pl.BlockSpec((1, blk, hp * hd), lambda b, g, c: (b, c, g)),
        out_shape=jax.ShapeDtypeStruct((n, t, heads * hd), BF16),
        scratch_shapes=[pltpu.VMEM((hp, t, LANES), BF16),
                        pltpu.VMEM((hp * (hd + PAD_ROWS), t), BF16),
                        pltpu.VMEM((hp, nb, LANES), F32),
                        pltpu.VMEM((hp, LANES, blk), BF16),
                        pltpu.VMEM((nb // grp, hp, PAD_ROWS, blk), BF16),
                        pltpu.VMEM((1, nq), F32),
                        pltpu.VMEM((hd + PAD_ROWS, nq), F32),
                        pltpu.VMEM((grp * blk, nq), F32), pltpu.VMEM((grp * blk, nq), F32),
                        pltpu.VMEM((1, nq), F32), pltpu.VMEM((1, nq), F32)],
        compiler_params=_cparams("parallel", "parallel", "arbitrary"),
        name="moba_prompt",
    )(qt, kt, vt)


def _kmean_pages_kernel(pt_ref, *refs, n_pg, ppb, rows):
    del pt_ref
    out_ref = refs[n_pg]
    for b in range(n_pg // ppb):
        acc = refs[b * ppb][0, 0]
        for j in range(1, ppb):
            acc = acc + refs[b * ppb + j][0, 0]
        out_ref[0, b] = jnp.sum(acc, axis=-1) / rows


def _sample_kmean(cache_kt, layer, page_table, n_full, ppb):
    _, _, heads, hd, page = cache_kt.shape
    n, n_pages = page_table.shape
    n_pg = SAMPLE_KMEAN_PAGES
    assert (n_full * ppb) % n_pg == 0 and n_pg % ppb == 0

    def page_map(b, g, pt, j):
        return (layer, pt[b * n_pages + g * n_pg + j], 0, 0, 0)

    specs = [pl.BlockSpec((1, 1, heads, hd, page), functools.partial(page_map, j=j))
             for j in range(n_pg)]
    return pl.pallas_call(
        functools.partial(_kmean_pages_kernel, n_pg=n_pg, ppb=ppb, rows=float(ppb * page)),
        grid_spec=pltpu.PrefetchScalarGridSpec(
            num_scalar_prefetch=1, grid=(n, n_full * ppb // n_pg), in_specs=specs,
            out_specs=pl.BlockSpec((1, n_pg // ppb, heads, hd), lambda b, g, pt: (b, g, 0, 0))),
        out_shape=jax.ShapeDtypeStruct((n, n_full, heads, hd), F32),
        compiler_params=_cparams("parallel", "arbitrary"),
        name="sample_kmean",
    )(page_table.reshape(-1), *([cache_kt] * n_pg))


def _sample_pick_kernel(q_ref, km_ref, idx_ref, *, heads, n_full, n_sel):
    col = lax.broadcasted_iota(jnp.int32, (q_ref.shape[2], n_full), 1)
    for h in range(heads):
        gate = _dot_nt(q_ref[0, h], km_ref[0, h], precision=HIGHEST)
        _, picks = _top_mask(gate, col, n_sel, n_full, axis=1)
        idx_ref[0, h] = jnp.concatenate([p[0] for p in picks], axis=1)


def _sample_pick(q, kmean, n_sel):
    n, heads, t, hd = q.shape
    n_full = kmean.shape[2]
    return pl.pallas_call(
        functools.partial(_sample_pick_kernel, heads=heads, n_full=n_full, n_sel=n_sel),
        grid=(n,),
        in_specs=[pl.BlockSpec((1, heads, t, hd), lambda b: (b, 0, 0, 0)),
                  pl.BlockSpec((1, heads, n_full, hd), lambda b: (b, 0, 0, 0))],
        out_specs=pl.BlockSpec((1, heads, t, n_sel), lambda b: (b, 0, 0, 0)),
        out_shape=jax.ShapeDtypeStruct((n, heads, t, n_sel), jnp.int32),
        compiler_params=_cparams("parallel"),
        name="sample_pick",
    )(q, kmean)


def _sample_attn_kernel(ph_ref, q_ref, kn_ref, vn_ref, *refs, n_pg):
    del ph_ref
    tq, hd = q_ref.shape[2], q_ref.shape[3]
    k_refs, v_refs, o_ref = refs[:tq * n_pg], refs[tq * n_pg:2 * tq * n_pg], refs[2 * tq * n_pg]
    qs = (q_ref[0, 0] * hd ** -0.5).astype(BF16)
    own_ok = (lax.broadcasted_iota(jnp.int32, (tq, tq), 1)
              <= lax.broadcasted_iota(jnp.int32, (tq, tq), 0))
    s_own = jnp.where(own_ok, _dot_nt(qs, kn_ref[0, 0].astype(BF16)), MASK_VALUE)
    vn = vn_ref[0, 0].astype(BF16)
    for t in range(tq):
        pages = slice(t * n_pg, (t + 1) * n_pg)
        kt = jnp.concatenate([r[0, 0, 0] for r in k_refs[pages]], axis=1).astype(BF16)
        vt = jnp.concatenate([r[0, 0, 0] for r in v_refs[pages]], axis=1).astype(BF16)
        s_sel = _dot(qs, kt)
        m = jnp.maximum(jnp.max(s_sel, axis=1, keepdims=True), jnp.max(s_own, axis=1, keepdims=True))
        p_sel = jnp.exp(s_sel - m)
        p_own = jnp.exp(s_own - m)
        denom = jnp.sum(p_sel, axis=1, keepdims=True) + jnp.sum(p_own, axis=1, keepdims=True)
        o = (_dot_nt(p_sel.astype(BF16), vt) + _dot(p_own.astype(BF16), vn)) / denom
        o_ref[0, 0, t:t + 1, :] = o[t:t + 1]


def _sample_attn(q, k_new, v_new, cache_kt, cache_vt, layer, phys):
    n, heads, t, hd = q.shape
    page = cache_kt.shape[4]
    n_pg = phys.shape[-1]

    def page_map(b, h, ph, j):
        return (layer, ph[(b * heads + h) * (t * n_pg) + j], h, 0, 0)

    pages = [pl.BlockSpec((1, 1, 1, hd, page), functools.partial(page_map, j=j))
             for j in range(t * n_pg)]
    own = pl.BlockSpec((1, 1, t, hd), lambda b, h, ph: (b, h, 0, 0))
    return pl.pallas_call(
        functools.partial(_sample_attn_kernel, n_pg=n_pg),
        grid_spec=pltpu.PrefetchScalarGridSpec(
            num_scalar_prefetch=1, grid=(n, heads),
            in_specs=[own, own, own] + pages * 2, out_specs=own),
        out_shape=jax.ShapeDtypeStruct((n, heads, t, hd), F32),
        compiler_params=_cparams("parallel", "parallel"),
        name="sample_attn",
    )(phys.reshape(-1), q, k_new, v_new, *([cache_kt] * (t * n_pg)), *([cache_vt] * (t * n_pg)))


def _gla_kernel(q_ref, k_ref, v_ref, la_ref, gb_ref, gn_ref, o_ref, sfin_ref, st_sc, qk_sc, b_sc,
                la_sc, attn_sc, *, chunk, sub, n_chunk, heads, dk, dv):
    step = pl.program_id(1)

    @pl.when(step == 0)
    def _():
        st_sc[...] = jnp.zeros(st_sc.shape, F32)

    r_i = lax.broadcasted_iota(jnp.int32, (chunk, chunk), 0)
    c_i = lax.broadcasted_iota(jnp.int32, (chunk, chunk), 1)
    tri = r_i >= c_i
    tri_b = tri.astype(BF16)
    row_k = lax.broadcasted_iota(jnp.int32, (chunk, dk), 0)
    q_scale = dk ** -0.5

    def intra_fast(q, k, b):
        rows = []
        for i in range(chunk // sub):
            lo, hi = i * sub, (i + 1) * sub
            ref = b[lo - 1:lo, :] if i > 0 else jnp.zeros((1, dk), F32)
            qt = (q[lo:hi] * jnp.exp(b[lo:hi] - ref)).astype(BF16)
            kt = (k * jnp.exp(jnp.where(row_k < hi, ref - b, -jnp.inf))).astype(BF16)
            rows.append(_dot_nt(qt, kt))
        return jnp.where(tri, jnp.concatenate(rows, axis=0), 0.0)

    def intra_exact(q, k, b, la):
        qk_sc[...] = q
        b_sc[...] = b
        la_sc[...] = la
        col = lax.broadcasted_iota(jnp.int32, (1, chunk), 1)

        def row(t, carry):
            q_t = qk_sc[pl.ds(t, 1), :]
            la_t = la_sc[pl.ds(t, 1), :]
            before = b_sc[pl.ds(t, 1), :] - la_t
            kt = (k * jnp.exp(jnp.where(row_k < t, before - b, -jnp.inf))).astype(BF16)
            a_row = _dot_nt((q_t * jnp.exp(la_t)).astype(BF16), kt)
            k_t = jnp.sum(jnp.where(row_k == t, k, 0.0), axis=0, keepdims=True)
            diag = jnp.sum(q_t * k_t, axis=1, keepdims=True)
            attn_sc[pl.ds(t, 1), :] = jnp.where(col == t, diag, a_row)
            return carry

        lax.fori_loop(0, chunk, row, 0)
        return attn_sc[...]

    def one_chunk(ci, carry, exact):
        off = pl.multiple_of(ci * chunk, chunk)
        rows_ds = pl.ds(off, chunk)
        la = la_ref[0, rows_ds, :]
        la_hi = la.astype(BF16)
        la_mid = (la - la_hi.astype(F32)).astype(BF16)
        la_lo = (la - la_hi.astype(F32) - la_mid.astype(F32)).astype(BF16)
        b_all = _dot(tri_b, la_hi) + _dot(tri_b, la_mid) + _dot(tri_b, la_lo)
        for h in range(heads):
            kc = slice(h * dk, (h + 1) * dk)
            vc = slice(h * dv, (h + 1) * dv)
            q = q_ref[0, rows_ds, kc] * q_scale
            k = k_ref[0, rows_ds, kc]
            v = v_ref[0, rows_ds, vc]
            b = b_all[:, kc]
            st = st_sc[h]
            o = _dot_nt((q * jnp.exp(b)).astype(BF16), st.astype(BF16))
            attn = intra_exact(q, k, b, la[:, kc]) if exact else intra_fast(q, k, b)
            o = o + _dot(attn.astype(BF16), v)
            b_last = b[chunk - 1:chunk, :]
            k_dec = (k * jnp.exp(b_last - b)).astype(BF16)
            st_sc[h] = st * jnp.exp(b_last) + _dot_tn(v, k_dec)
            gb = gb_ref[0, rows_ds, vc].astype(F32)
            o_ref[0, rows_ds, vc] = (_rms(o) * gn_ref[...] * (gb * jax.nn.sigmoid(gb))).astype(BF16)
        return carry

    safe = jnp.min(la_ref[0]) * sub > GLA_FAST_MIN_LOG_DECAY

    @pl.when(safe)
    def _():
        lax.fori_loop(0, n_chunk, functools.partial(one_chunk, exact=False), 0)

    @pl.when(jnp.logical_not(safe))
    def _():
        lax.fori_loop(0, n_chunk, functools.partial(one_chunk, exact=True), 0)

    @pl.when(step == pl.num_programs(1) - 1)
    def _():
        sfin_ref[0] = st_sc[...]


def _gla_prompt(qb, kb, vb, la, gb, gla_norm_g, b0, n, *, heads, dk, dv):
    _, t, _ = qb.shape
    rows = min(GLA_STEP_ROWS, t)
    chunk = min(GLA_CHUNK, rows)
    assert t % rows == 0 and rows % chunk == 0 and chunk % GLA_SUB == 0
    kspec = pl.BlockSpec((1, rows, heads * dk), lambda b, s: (b + b0, s, 0))
    vspec = pl.BlockSpec((1, rows, heads * dv), lambda b, s: (b + b0, s, 0))
    out_spec = pl.BlockSpec((1, rows, heads * dv), lambda b, s: (b, s, 0))
    return pl.pallas_call(
        functools.partial(_gla_kernel, chunk=chunk, sub=GLA_SUB, n_chunk=rows // chunk,
                          heads=heads, dk=dk, dv=dv),
        grid=(n, t // rows),
        in_specs=[kspec, kspec, vspec, kspec, vspec, pl.BlockSpec((1, dv), lambda b, s: (0, 0))],
        out_specs=[out_spec, pl.BlockSpec((1, heads, dv, dk), lambda b, s: (b, 0, 0, 0))],
        out_shape=[jax.ShapeDtypeStruct((n, t, heads * dv), BF16),
                   jax.ShapeDtypeStruct((n, heads, dv, dk), F32)],
        scratch_shapes=[pltpu.VMEM((heads, dv, dk), F32), pltpu.VMEM((chunk, dk), F32),
                        pltpu.VMEM((chunk, dk), F32), pltpu.VMEM((chunk, dk), F32),
                        pltpu.VMEM((chunk, chunk), F32)],
        compiler_params=_cparams("parallel", "arbitrary"),
        name="gla_prompt",
    )(qb, kb, vb, la, gb, gla_norm_g)


def _gla_sample_kernel(qt_ref, kt_ref, lat_ref, v_ref, gb_ref, gn_ref, s0_ref, o_ref, sfin_ref, *, dk):
    s = s0_ref[0, 0, 0]
    qt = qt_ref[0, 0] * dk ** -0.5
    kt = kt_ref[0, 0]
    at = jnp.exp(lat_ref[0, 0])
    v = v_ref[0].astype(F32)
    outs = []
    for t in range(v.shape[0]):
        s = at[:, t:t + 1] * s + kt[:, t:t + 1] * v[t:t + 1, :]
        outs.append(jnp.sum(qt[:, t:t + 1] * s, axis=0, keepdims=True))
    o = jnp.concatenate(outs, axis=0)
    gb = gb_ref[0].astype(F32)
    o_ref[0] = (_rms(o) * gn_ref[...] * (gb * jax.nn.sigmoid(gb))).astype(BF16)
    sfin_ref[0, 0] = s


def _gla_sample(qt, kt, lat, vb, gb, gla_norm_g, state, layer, *, dk, dv):
    n, heads, _, t = qt.shape
    tspec = pl.BlockSpec((1, 1, dk, t), lambda b, h: (b, h, 0, 0))
    vspec = pl.BlockSpec((1, t, dv), lambda b, h: (b, 0, h))
    return pl.pallas_call(
        functools.partial(_gla_sample_kernel, dk=dk),
        grid=(n, heads),
        in_specs=[tspec, tspec, tspec, vspec, vspec, pl.BlockSpec((1, dv), lambda b, h: (0, 0)),
                  pl.BlockSpec((1, 1, 1, dk, dv), lambda b, h: (layer, b, h, 0, 0))],
        out_specs=[vspec, pl.BlockSpec((1, 1, dk, dv), lambda b, h: (b, h, 0, 0))],
        out_shape=[jax.ShapeDtypeStruct((n, t, heads * dv), BF16),
                   jax.ShapeDtypeStruct((n, heads, dk, dv), F32)],
        compiler_params=_cparams("parallel", "parallel"),
        name="gla_sample",
    )(qt, kt, lat, vb, gb, gla_norm_g, state)


def _merge_kernel(x_ref, oa_ref, ob_ref, g1_ref, wa_ref, wb_ref, wmg_ref, wo_ref, g2_ref, wrh_ref,
                  wrl_ref, br_ref, x2_ref, h2_ref, idx_ref, gate_ref, *, d, n_exp):
    x = x_ref[...]
    hb = (_rms(x) * g1_ref[...]).astype(BF16)
    y_a = _dot(oa_ref[...], wa_ref[...])
    y_b = _dot(ob_ref[...], wb_ref[...])
    mixed = (jax.nn.sigmoid(_dot(hb, wmg_ref[:, :d])) * y_a
             + jax.nn.sigmoid(_dot(hb, wmg_ref[:, d:])) * y_b)
    x2 = x + _dot(mixed.astype(BF16), wo_ref[...])
    x2_ref[...] = x2
    h2 = _rms(x2) * g2_ref[...]
    h2_hi = h2.astype(BF16)
    h2_ref[...] = h2_hi
    h2_lo = (h2 - h2_hi.astype(F32)).astype(BF16)
    logits = (_dot(h2_hi, wrh_ref[...]) + _dot(h2_hi, wrl_ref[...]) + _dot(h2_lo, wrh_ref[...])
              + br_ref[...])
    col = lax.broadcasted_iota(jnp.int32, logits.shape, 1)
    _, picks = _top_mask(logits, col, MOE_TOP_K, n_exp, axis=1)
    top_val = jnp.concatenate([p[1] for p in picks], axis=1)
    e = jnp.exp(top_val - top_val[:, :1])
    idx_ref[...] = jnp.concatenate([p[0] for p in picks], axis=1)
    gate_ref[...] = e / jnp.sum(e, axis=1, keepdims=True)


def _merge(x, row0, o_a, o_b, norm1_g, w_a, w_b, w_mg, w_o, norm2_g, w_router, b_router):
    n_tok, d = o_a.shape[0], x.shape[1]
    n_exp = w_router.shape[1]
    w_router_hi = w_router.astype(BF16)
    w_router_lo = (w_router - w_router_hi.astype(F32)).astype(BF16)
    tm = min(PROJ_ROWS, n_tok)
    assert n_tok % tm == 0 and row0 % tm == 0
    full = lambda a: pl.BlockSpec(a.shape, lambda i: (0,) * a.ndim)
    rows = lambda w: pl.BlockSpec((tm, w), lambda i: (i, 0))
    return pl.pallas_call(
        functools.partial(_merge_kernel, d=d, n_exp=n_exp),
        grid=(n_tok // tm,),
        in_specs=[pl.BlockSpec((tm, d), lambda i: (i + row0 // tm, 0)),
                  rows(o_a.shape[1]), rows(o_b.shape[1]), full(norm1_g), full(w_a),
                  full(w_b), full(w_mg), full(w_o), full(norm2_g), full(w_router_hi),
                  full(w_router_lo), full(b_router)],
        out_specs=[rows(d), rows(d), rows(MOE_TOP_K), rows(MOE_TOP_K)],
        out_shape=[jax.ShapeDtypeStruct((n_tok, d), F32), jax.ShapeDtypeStruct((n_tok, d), BF16),
                   jax.ShapeDtypeStruct((n_tok, MOE_TOP_K), jnp.int32),
                   jax.ShapeDtypeStruct((n_tok, MOE_TOP_K), F32)],
        compiler_params=_cparams("parallel"),
        name="merge",
    )(x, o_a, o_b, norm1_g, w_a, w_b, w_mg, w_o, norm2_g, w_router_hi, w_router_lo, b_router)


def _expert_kernel(te_ref, nt_ref, x_ref, wup_ref, bg_ref, bl_ref, wdn_ref, bdn_ref, y_ref,
                   wg_sc, wl_sc, wd_sc, t_sc, *, slab):
    i = pl.program_id(0)

    @pl.when((i == 0) | (te_ref[i] != te_ref[jnp.maximum(i - 1, 0)]))
    def _():
        for c in range(0, wup_ref.shape[2], slab):
            rows = slice(c // 2, (c + slab) // 2)
            for j in range(t_sc.shape[0]):
                lanes = slice(j * LANES, (j + 1) * LANES)
                t_sc[j] = wup_ref[0, lanes, c:c + slab].T
                wg_sc[rows, lanes] = t_sc[j, pl.ds(0, slab // 2, stride=2), :].astype(BF16)
                wl_sc[rows, lanes] = t_sc[j, pl.ds(1, slab // 2, stride=2), :].astype(BF16)
        wd_sc[...] = wdn_ref[0].astype(BF16)

    @pl.when(i < nt_ref[0])
    def _():
        x = x_ref[...]
        x_glu = jnp.minimum(_dot_nt(x, wg_sc[...]) + bg_ref[0], SWIGLU_LIMIT)
        x_lin = jnp.clip(_dot_nt(x, wl_sc[...]) + bl_ref[0], -SWIGLU_LIMIT, SWIGLU_LIMIT)
        a = x_glu * jax.nn.sigmoid(SWIGLU_ALPHA * x_glu) * (x_lin + 1.0)
        y_ref[...] = (_dot(a.astype(BF16), wd_sc[...]) + bdn_ref[0]).astype(y_ref.dtype)

    @pl.when(i >= nt_ref[0])
    def _():
        y_ref[...] = jnp.zeros(y_ref.shape, y_ref.dtype)


def _experts(buf, tile_e, n_used, w_up, b_glu, b_lin, w_dn, b_dn, *, tile_rows):
    r, d = buf.shape
    d_ff = w_dn.shape[1]
    slab = min(2 * LANES, 2 * d_ff)
    by_expert = lambda *blk: pl.BlockSpec((1,) + blk, lambda i, te, nt: (te[i], 0, 0))
    return pl.pallas_call(
        functools.partial(_expert_kernel, slab=slab),
        grid_spec=pltpu.PrefetchScalarGridSpec(
            num_scalar_prefetch=2, grid=(r // tile_rows,),
            in_specs=[pl.BlockSpec((tile_rows, d), lambda i, te, nt: (i, 0)),
                      by_expert(d, 2 * d_ff), by_expert(1, d_ff), by_expert(1, d_ff),
                      by_expert(d_ff, d), by_expert(1, d)],
            out_specs=pl.BlockSpec((tile_rows, d), lambda i, te, nt: (i, 0)),
            scratch_shapes=[pltpu.VMEM((d_ff, d), BF16), pltpu.VMEM((d_ff, d), BF16),
                            pltpu.VMEM((d_ff, d), BF16), pltpu.VMEM((d // LANES, slab, LANES), F32)]),
        out_shape=jax.ShapeDtypeStruct((r, d), BF16),
        compiler_params=_cparams("arbitrary"),
        name="experts",
    )(tile_e, n_used, buf, w_up, b_glu, b_lin, w_dn, b_dn)


def _rank_kernel(idx_ref, rank_ref, cnt_ref, seen_sc):
    @pl.when(pl.program_id(0) == 0)
    def _():
        seen_sc[...] = jnp.zeros(seen_sc.shape, F32)

    idx = idx_ref[...]
    tr, top_k = idx.shape
    lane = lax.broadcasted_iota(jnp.int32, (tr, LANES), 1)
    hits = [idx[:, k:k + 1] == lane for k in range(top_k)]
    routed = hits[0].astype(F32)
    for h in hits[1:]:
        routed = routed + h.astype(F32)
    earlier = (lax.broadcasted_iota(jnp.int32, (tr, tr), 1)
               < lax.broadcasted_iota(jnp.int32, (tr, tr), 0)).astype(BF16)
    before = _dot(earlier, routed.astype(BF16)) + seen_sc[...]
    rank_ref[...] = jnp.concatenate(
        [jnp.sum(jnp.where(h, before, 0.0), axis=1, keepdims=True) for h in hits],
        axis=1).astype(jnp.int32)
    seen_sc[...] = seen_sc[...] + jnp.sum(routed, axis=0, keepdims=True)
    cnt_ref[...] = seen_sc[...]


def _rank(top_idx):
    n_tok, top_k = top_idx.shape
    tr = min(ROUTE_ROWS, n_tok)
    assert n_tok % tr == 0
    return pl.pallas_call(
        _rank_kernel,
        grid=(n_tok // tr,),
        in_specs=[pl.BlockSpec((tr, top_k), lambda i: (i, 0))],
        out_specs=[pl.BlockSpec((tr, top_k), lambda i: (i, 0)),
                   pl.BlockSpec((1, LANES), lambda i: (0, 0))],
        out_shape=[jax.ShapeDtypeStruct((n_tok, top_k), jnp.int32),
                   jax.ShapeDtypeStruct((1, LANES), F32)],
        scratch_shapes=[pltpu.VMEM((1, LANES), F32)],
        compiler_params=_cparams("arbitrary"),
        name="moe_rank",
    )(top_idx)


def _combine_kernel(x_ref, y_ref, g_ref, o_ref):
    acc = x_ref[...]
    g = g_ref[...]
    for k in range(y_ref.shape[0]):
        acc = acc + g[:, k:k + 1] * y_ref[k].astype(F32)
    o_ref[...] = acc


def _combine(x2, y_kt, gate):
    top_k, n_tok, d = y_kt.shape
    tm = min(ROUTE_ROWS, n_tok)
    assert n_tok % tm == 0
    return pl.pallas_call(
        _combine_kernel,
        grid=(n_tok // tm,),
        in_specs=[pl.BlockSpec((tm, d), lambda i: (i, 0)),
                  pl.BlockSpec((top_k, tm, d), lambda i: (0, i, 0)),
                  pl.BlockSpec((tm, top_k), lambda i: (i, 0))],
        out_specs=pl.BlockSpec((tm, d), lambda i: (i, 0)),
        out_shape=jax.ShapeDtypeStruct((n_tok, d), F32),
        compiler_params=_cparams("parallel"),
        name="moe_combine",
    )(x2, y_kt, gate)


def _moe(x2, h2, top_idx, gate, expert_weights, tile_rows):
    n_tok, d = h2.shape
    n_exp = expert_weights[0].shape[0]
    assert n_exp <= LANES
    n_assign = n_tok * MOE_TOP_K
    rank, cnt = _rank(top_idx)
    counts = cnt[0, :n_exp].astype(jnp.int32)
    padded = (counts + tile_rows - 1) // tile_rows * tile_rows
    pend = jnp.cumsum(padded)
    experts = jnp.arange(n_exp, dtype=jnp.int32)
    first_row = jnp.sum(jnp.where(top_idx[:, :, None] == experts, pend - padded, 0), axis=-1)
    dest = (first_row + rank).reshape(-1)
    n_tiles = -(-n_assign // tile_rows) + n_exp
    tile_row0 = jnp.arange(n_tiles, dtype=jnp.int32) * tile_rows
    tile_e = jnp.minimum(jnp.sum((pend[None, :] <= tile_row0[:, None]).astype(jnp.int32), axis=1),
                         n_exp - 1)
    n_used = (pend[-1:] // tile_rows).astype(jnp.int32)
    src = jnp.zeros((n_tiles * tile_rows,), jnp.int32).at[dest].set(
        jnp.arange(n_assign, dtype=jnp.int32) // MOE_TOP_K)
    y = _experts(h2[src], tile_e, n_used, *expert_weights, tile_rows=tile_rows)
    y_kt = y[dest.reshape(n_tok, MOE_TOP_K).T]
    return _combine(x2, y_kt, gate)


def _rope_tables_t(pos, hd):
    half = hd // 2
    inv = ROPE_THETA ** (-jnp.arange(half, dtype=F32) / half)
    ang = inv[:, None] * pos.astype(F32)[None, :]
    cos, sin = jnp.cos(ang), jnp.sin(ang)
    return jnp.concatenate([cos, cos], axis=0), jnp.concatenate([-sin, sin], axis=0)


def kernel(x_prompt, x_sample, cache_k, cache_v, state_gla, page_table, norm1_g, w_in, q_norm_g,
           k_norm_g, w_gate_up, b_gate_up, gla_norm_g, w_branch_a, w_branch_b, w_merge_gate, w_out,
           norm2_g, w_router, b_router, w_up, b_up, w_down, b_down):
    depth = norm1_g.shape[0]
    assert depth == 1, "sample/prompt streams are only chained through one layer here"
    layer = 0
    n_p, t_p, d = x_prompt.shape
    n_s, t_s, _ = x_sample.shape
    heads, page, hd = cache_k.shape[2:]
    b_heads, dk, dv = state_gla.shape[2:]
    bk, bv = b_heads * dk, b_heads * dv
    aqk = heads * hd
    rank = w_gate_up.shape[1]
    past_len = page_table.shape[1] * page
    ppb = MOBA_BLOCK // page
    n_full = past_len // MOBA_BLOCK
    assert past_len % MOBA_BLOCK == 0 and n_full >= MOBA_TOPK and rank <= LANES
    assert w_in.shape[2] == 3 * aqk + 2 * bk + 2 * bv + rank

    w_qkv_t = w_in[layer][:, :3 * aqk].T.astype(BF16)
    w_rest = jnp.pad(w_in[layer][:, 3 * aqk:], ((0, 0), (0, LANES - rank))).astype(BF16)
    w_gu_p = jnp.pad(w_gate_up[layer], ((0, LANES - rank), (0, 0))).astype(BF16)
    g1 = norm1_g[layer][None, :]
    g2 = norm2_g[layer][None, :]
    qg = q_norm_g[layer][:, None]
    kg = k_norm_g[layer][:, None]
    bgu = b_gate_up[layer][None, :]
    gn = gla_norm_g[layer][None, :]
    w_a = w_branch_a[layer].astype(BF16)
    w_b = w_branch_b[layer].astype(BF16)
    w_mg = w_merge_gate[layer].astype(BF16)
    w_o = w_out[layer].astype(BF16)
    w_r = w_router[layer]
    b_r = b_router[layer][None, :]
    expert_weights = (w_up[layer], b_up[layer][:, None, 0::2], b_up[layer][:, None, 1::2],
                      w_down[layer], b_down[layer][:, None, :])
    proj = functools.partial(_project, norm1_g=g1, w_qkv_t=w_qkv_t, w_rest=w_rest, q_norm_g=qg,
                             k_norm_g=kg, w_gu_p=w_gu_p, b_gu=bgu, heads=heads, hd=hd, bk=bk, bv=bv)
    merge = functools.partial(_merge, norm1_g=g1, w_a=w_a, w_b=w_b, w_mg=w_mg, w_o=w_o,
                              norm2_g=g2, w_router=w_r, b_router=b_r)

    cos_p, sin_p = _rope_tables_t(jnp.arange(t_p), hd)
    qt, kt, vt, qb, kb, vb, la, gb = proj(x_prompt, cos_p, sin_p)
    n_part = n_p // PROMPT_SPLITS if n_p % PROMPT_SPLITS == 0 else n_p
    x_rows = x_prompt.reshape(n_p * t_p, d)
    y_parts, st_parts = [], []
    for b0 in range(0, n_p, n_part):
        o_a = _moba_prompt(qt, kt, vt, b0, n_part)
        o_b, st = _gla_prompt(qb, kb, vb, la, gb, gn, b0, n_part, heads=b_heads, dk=dk, dv=dv)
        x2, h2, idx, gate = merge(x_rows, b0 * t_p, o_a.reshape(n_part * t_p, -1),
                                  o_b.reshape(n_part * t_p, -1))
        y_parts.append(_moe(x2, h2, idx, gate, expert_weights, MOE_ROWS))
        st_parts.append(st)
    y_prompt = jnp.concatenate(y_parts).reshape(n_p, t_p, d)
    s_prompt = jnp.swapaxes(jnp.concatenate(st_parts), 2, 3)

    n_tok_s = n_s * t_s
    cos_s, sin_s = _rope_tables_t(past_len + jnp.arange(t_s), hd)
    qt_s, kt_s, vt_s, qb_s, kb_s, vb_s, la_s, gb_s = proj(
        x_sample.reshape(1, n_tok_s, d), jnp.tile(cos_s, (1, n_s)), jnp.tile(sin_s, (1, n_s)))
    per_seq = lambda a: a.reshape(heads, hd, n_s, t_s).transpose(2, 0, 3, 1)
    qa_s, ka_s, va_s = per_seq(qt_s), per_seq(kt_s), per_seq(vt_s)
    cache_kt = jnp.swapaxes(cache_k, 3, 4)
    cache_vt = jnp.swapaxes(cache_v, 3, 4)
    kmean = _sample_kmean(cache_kt, layer, page_table, n_full, ppb).transpose(0, 2, 1, 3)
    blk_idx = _sample_pick(qa_s, kmean, MOBA_TOPK)
    lpages = blk_idx[..., None] * ppb + jnp.arange(ppb, dtype=jnp.int32)
    phys = jnp.take_along_axis(page_table[:, None, None, :],
                               lpages.reshape(n_s, heads, t_s, MOBA_TOPK * ppb), axis=-1)
    o_a_s = _sample_attn(qa_s, ka_s, va_s, cache_kt, cache_vt, layer, phys)
    o_a_s = o_a_s.transpose(0, 2, 1, 3).reshape(n_tok_s, aqk).astype(BF16)
    tr = lambda a: a.reshape(n_s, t_s, b_heads, dk).transpose(0, 2, 3, 1)
    o_b_s, s_sample = _gla_sample(tr(qb_s), tr(kb_s), tr(la_s), vb_s.reshape(n_s, t_s, bv),
                                  gb_s.reshape(n_s, t_s, bv), gn, state_gla, layer, dk=dk, dv=dv)
    x2_s, h2_s, idx_s, gate_s = merge(x_sample.reshape(n_tok_s, d), 0, o_a_s,
                                      o_b_s.reshape(n_tok_s, bv))
    y_sample = _moe(x2_s, h2_s, idx_s, gate_s, expert_weights,
                    min(MOE_ROWS, n_tok_s)).reshape(n_s, t_s, d)
    k_prompt = jnp.swapaxes(kt, 2, 3)
    v_prompt = jnp.swapaxes(vt, 2, 3)
    return (y_prompt, y_sample, k_prompt[None], v_prompt[None], ka_s[None], va_s[None],
            s_prompt[None], s_sample[None])
```

```python
import functools

import jax
import jax.numpy as jnp
from jax import lax
from jax.experimental import pallas as pl
from jax.experimental.pallas import tpu as pltpu

F32 = jnp.float32
BF16 = jnp.bfloat16
HIGHEST = lax.Precision.HIGHEST

MOBA_BLOCK = 256
MOBA_TOPK = 3
ROPE_THETA = 10000.0
GLA_GATE_NORM = 16.0
NORM_EPS = 1e-6
MOE_TOP_K = 4
SWIGLU_ALPHA = 1.702
SWIGLU_LIMIT = 7.0

LANES = 128
PAD_ROWS = 16
VMEM_LIMIT_BYTES = 56 * 1024 * 1024

MASK_VALUE = -1e30
LOG2_E = 1.4426950408889634
PROJ_ROWS = 512
GLA_CHUNK = 64
GLA_SUB = 16
GLA_STEP_ROWS = 512
GLA_SEQS = 2
GLA_FAST_MIN_LOG_DECAY = -80.0
MOE_ROWS = 512
ROUTE_ROWS = 1024
PROMPT_SPLITS = 1
MOBA_GROUP = 4
SAMPLE_KMEAN_PAGES = 16


def _cparams(*sem):
    return pltpu.CompilerParams(dimension_semantics=sem, vmem_limit_bytes=VMEM_LIMIT_BYTES)


def _rms(x):
    return x * lax.rsqrt(jnp.mean(x * x, axis=-1, keepdims=True) + NORM_EPS)


def _dot(a, b):
    return jnp.dot(a, b, preferred_element_type=F32)


def _dot_nt(a, b, precision=None):
    return lax.dot_general(a, b, (((1,), (1,)), ((), ())), precision=precision,
                           preferred_element_type=F32)


def _dot_tn(a, b):
    return lax.dot_general(a, b, (((0,), (0,)), ((), ())), preferred_element_type=F32)


def _top_mask(g, pos, n_pick, n_pos, axis):
    sel = jnp.zeros(g.shape, F32)
    picks = []
    for _ in range(n_pick):
        mx = jnp.max(g, axis=axis, keepdims=True)
        cand = jnp.where((g == mx) & (mx > -jnp.inf), pos, n_pos)
        idx = jnp.min(cand, axis=axis, keepdims=True)
        pick = pos == idx
        sel = jnp.where(pick, 1.0, sel)
        g = jnp.where(pick, -jnp.inf, g)
        picks.append((idx, mx))
    return sel, picks


def _proj_kernel(x_ref, g1_ref, wqkv_ref, wrest_ref, qg_ref, kg_ref, wgu_ref, bgu_ref, cos_ref,
                 sin_ref, qt_ref, kt_ref, vt_ref, qb_ref, kb_ref, vb_ref, la_ref, gb_ref,
                 *, heads, hd, bk, bv):
    x = x_ref[0]
    hb = (_rms(x) * g1_ref[...]).astype(BF16)
    cos = cos_ref[...]
    sin = sin_ref[...]
    aqk = heads * hd

    def mm_t(lo, hi):
        return _dot_nt(wqkv_ref[lo:hi, :], hb)

    def norm_rope_t(z, g_ref, out_ref):
        for h in range(heads):
            y = z[h * hd:(h + 1) * hd, :]
            y = y * lax.rsqrt(jnp.mean(y * y, axis=0, keepdims=True) + NORM_EPS) * g_ref[...]
            rot = jnp.concatenate([y[hd // 2:], y[:hd // 2]], axis=0)
            out_ref[0, h] = y * cos + rot * sin

    norm_rope_t(mm_t(0, aqk), qg_ref, qt_ref)
    norm_rope_t(mm_t(aqk, 2 * aqk), kg_ref, kt_ref)
    zv = mm_t(2 * aqk, 3 * aqk)
    for h in range(heads):
        vt_ref[0, h] = zv[h * hd:(h + 1) * hd, :]

    def mm(lo, hi):
        return _dot(hb, wrest_ref[:, lo:hi])

    qb_ref[0] = mm(0, bk)
    kb_ref[0] = mm(bk, 2 * bk)
    vb_ref[0] = mm(2 * bk, 2 * bk + bv).astype(BF16)
    gb_ref[0] = mm(2 * bk + bv, 2 * bk + 2 * bv).astype(BF16)
    lr = mm(2 * bk + 2 * bv, 2 * bk + 2 * bv + LANES)
    u = _dot(lr.astype(BF16), wgu_ref[...]) + bgu_ref[...]
    log_sig = jnp.minimum(u, 0.0) - jnp.log(1.0 + jnp.exp(-jnp.abs(u)))
    la_ref[0] = log_sig / GLA_GATE_NORM


def _project(x, cos_t, sin_t, norm1_g, w_qkv_t, w_rest, q_norm_g, k_norm_g, w_gu_p, b_gu,
             *, heads, hd, bk, bv):
    n, t, d = x.shape
    tm = min(PROJ_ROWS, t)
    assert t % tm == 0
    full = lambda a: pl.BlockSpec(a.shape, lambda b, i: (0,) * a.ndim)
    head_out = pl.BlockSpec((1, heads, hd, tm), lambda b, i: (b, 0, 0, i))
    row_out = lambda w: pl.BlockSpec((1, tm, w), lambda b, i: (b, i, 0))
    table = pl.BlockSpec((hd, tm), lambda b, i: (0, i))
    head_shape = jax.ShapeDtypeStruct((n, heads, hd, t), F32)
    return pl.pallas_call(
        functools.partial(_proj_kernel, heads=heads, hd=hd, bk=bk, bv=bv),
        grid=(n, t // tm),
        in_specs=[pl.BlockSpec((1, tm, d), lambda b, i: (b, i, 0)),
                  full(norm1_g), full(w_qkv_t), full(w_rest), full(q_norm_g), full(k_norm_g),
                  full(w_gu_p), full(b_gu), table, table],
        out_specs=[head_out, head_out, head_out, row_out(bk), row_out(bk), row_out(bv),
                   row_out(bk), row_out(bv)],
        out_shape=[head_shape, head_shape, head_shape,
                   jax.ShapeDtypeStruct((n, t, bk), F32),
                   jax.ShapeDtypeStruct((n, t, bk), F32),
                   jax.ShapeDtypeStruct((n, t, bv), BF16),
                   jax.ShapeDtypeStruct((n, t, bk), F32),
                   jax.ShapeDtypeStruct((n, t, bv), BF16)],
        compiler_params=_cparams("parallel", "parallel"),
        name="proj",
    )(x, norm1_g, w_qkv_t, w_rest, q_norm_g, k_norm_g, w_gu_p, b_gu, cos_t, sin_t)


def _moba_kernel(qt_ref, kt_ref, vt_ref, o_ref, k_sc, vt_sc, kmean_sc, qs_sc, bias_sc, m_sc, acc_sc,
                 sa_sc, sb_sc, ma_sc, mb_sc, *, nb, blk, hd, hp, grp):
    c = pl.program_id(2)
    nq = hp * blk
    vr = hd + PAD_ROWS
    n_groups = nb // grp
    cols = [slice(i * blk, (i + 1) * blk) for i in range(hp)]

    @pl.when(c == 0)
    def _():
        row = lax.broadcasted_iota(jnp.int32, (LANES - hd, blk), 0)
        ones_row = (lax.broadcasted_iota(jnp.int32, (PAD_ROWS, blk), 0) == 0).astype(BF16)

        def stage_kv(j, carry):
            off = pl.multiple_of(j * blk, blk)
            place = (row == j % grp).astype(F32)
            for i in range(hp):
                k_nat = jnp.concatenate([kt_ref[0, i, :, pl.ds(off, blk)], place], axis=0).T
                k_sc[i, pl.ds(off, blk), :] = k_nat.astype(BF16)
                kmean_sc[i, pl.ds(j, 1), :] = jnp.mean(k_nat, axis=0, keepdims=True)
                vt_sc[i * vr:i * vr + hd, pl.ds(off, blk)] = vt_ref[0, i, :, pl.ds(off, blk)].astype(BF16)
                vt_sc[i * vr + hd:(i + 1) * vr, pl.ds(off, blk)] = ones_row
            return carry

        lax.fori_loop(0, nb, stage_kv, 0)

    zero_rows = jnp.zeros((LANES - hd, blk), F32)
    q_pad = [jnp.concatenate([qt_ref[0, i], zero_rows], axis=0) for i in range(hp)]
    gate = jnp.concatenate(
        [jnp.dot(kmean_sc[i], q_pad[i], precision=HIGHEST, preferred_element_type=F32)
         for i in range(hp)], axis=1)
    blk_row = lax.broadcasted_iota(jnp.int32, (nb, nq), 0)
    sel, _ = _top_mask(jnp.where(blk_row < c, gate, -jnp.inf), blk_row, min(MOBA_TOPK, nb - 1),
                       nb, axis=0)
    bias = (sel - 1.0) * -MASK_VALUE
    pad = jnp.zeros((PAD_ROWS - grp, blk), F32)
    for g in range(n_groups):
        for i in range(hp):
            bias_sc[g, i] = jnp.concatenate([bias[g * grp:(g + 1) * grp, cols[i]], pad],
                                            axis=0).astype(BF16)
    for i in range(hp):
        qs_sc[i] = (q_pad[i] * (hd ** -0.5 * LOG2_E)).astype(BF16)

    def scores(off, n_keys):
        return jnp.concatenate([_dot(k_sc[i, pl.ds(off, n_keys), :], qs_sc[i]) for i in range(hp)],
                               axis=1)

    def update(s, m_blk, off, n_keys, first):
        m_new = m_blk if first else jnp.maximum(m_sc[...], m_blk)
        pb = jnp.exp2(s - m_new).astype(BF16)
        if not first:
            alpha = jnp.exp2(m_sc[...] - m_new)
        m_sc[...] = m_new
        for i in range(hp):
            pv = _dot(vt_sc[i * vr:(i + 1) * vr, pl.ds(off, n_keys)], pb[:, cols[i]])
            acc_sc[:, cols[i]] = pv if first else alpha[:, cols[i]] * acc_sc[:, cols[i]] + pv

    own_off = pl.multiple_of(c * blk, blk)
    causal = (lax.broadcasted_iota(jnp.int32, (blk, nq), 0)
              <= lax.broadcasted_iota(jnp.int32, (blk, nq), 1) % blk)
    s_own = jnp.where(causal, scores(own_off, blk), MASK_VALUE)
    update(s_own, jnp.max(s_own, axis=0, keepdims=True), own_off, blk, True)

    n_grp = (c + grp - 1) // grp

    def stage(g, s_buf, mx_buf):
        g = jnp.minimum(g, n_groups - 1)
        for i in range(hp):
            qs_sc[i, hd:hd + PAD_ROWS, :] = bias_sc[g, i]
        s = scores(pl.multiple_of(g * (grp * blk), grp * blk), grp * blk)
        s_buf[...] = s
        mx_buf[...] = jnp.max(s, axis=0, keepdims=True)

    def consume(g, s_buf, mx_buf):
        update(s_buf[...], mx_buf[...], pl.multiple_of(g * (grp * blk), grp * blk), grp * blk, False)

    @pl.when(n_grp > 0)
    def _():
        stage(0, sa_sc, ma_sc)

    def body(u, carry):
        stage(2 * u + 1, sb_sc, mb_sc)
        consume(2 * u, sa_sc, ma_sc)
        stage(2 * u + 2, sa_sc, ma_sc)
        consume(2 * u + 1, sb_sc, mb_sc)
        return carry

    lax.fori_loop(0, n_grp // 2, body, 0)

    @pl.when(n_grp % 2 == 1)
    def _():
        consume(n_grp - 1, sa_sc, ma_sc)

    out_t = acc_sc[:hd, :] / acc_sc[hd:hd + 1, :]
    out_t = jnp.concatenate([out_t[:, cols[i]] for i in range(hp)], axis=0)
    o_ref[0] = out_t.T.astype(BF16)


def _moba_prompt(qt, kt, vt, b0, n):
    _, heads, hd, t = qt.shape
    blk = MOBA_BLOCK
    assert t % blk == 0 and LANES % hd == 0
    hp = LANES // hd
    assert heads % hp == 0
    nb = t // blk
    grp = MOBA_GROUP if nb % MOBA_GROUP == 0 else 1
    nq = hp * blk
    kv_spec = pl.BlockSpec((1, hp, hd, t), lambda b, g, c: (b + b0, g, 0, 0))
    return pl.pallas_call(
        functools.partial(_moba_kernel, nb=nb, blk=blk, hd=hd, hp=hp, grp=grp),
        grid=(n, heads // hp, nb),
        in_specs=[pl.BlockSpec((1, hp, hd, blk), lambda b, g, c: (b + b0, g, 0, c)), kv_spec, kv_spec],
        out_specs=pl.BlockSpec((1, blk, hp * hd), lambda b, g, c: (b, c, g)),
        out_shape=jax.ShapeDtypeStruct((n, t, heads * hd), BF16),
        scratch_shapes=[pltpu.VMEM((hp, t, LANES), BF16),
                        pltpu.VMEM((hp * (hd + PAD_ROWS), t), BF16),
                        pltpu.VMEM((hp, nb, LANES), F32),
                        pltpu.VMEM((hp, LANES, blk), BF16),
                        pltpu.VMEM((nb // grp, hp, PAD_ROWS, blk), BF16),
                        pltpu.VMEM((1, nq), F32),
                        pltpu.VMEM((hd + PAD_ROWS, nq), F32),
                        pltpu.VMEM((grp * blk, nq), F32), pltpu.VMEM((grp * blk, nq), F32),
                        pltpu.VMEM((1, nq), F32), pltpu.VMEM((1, nq), F32)],
        compiler_params=_cparams("parallel", "parallel", "arbitrary"),
        name="moba_prompt",
    )(qt, kt, vt)


def _kmean_pages_kernel(pt_ref, *refs, n_pg, ppb, rows):
    del pt_ref
    out_ref = refs[n_pg]
    for b in range(n_pg // ppb):
        acc = refs[b * ppb][0, 0]
        for j in range(1, ppb):
            acc = acc + refs[b * ppb + j][0, 0]
        out_ref[0, b] = jnp.sum(acc, axis=-1) / rows


def _sample_kmean(cache_kt, layer, page_table, n_full, ppb):
    _, _, heads, hd, page = cache_kt.shape
    n, n_pages = page_table.shape
    n_pg = SAMPLE_KMEAN_PAGES
    assert (n_full * ppb) % n_pg == 0 and n_pg % ppb == 0

    def page_map(b, g, pt, j):
        return (layer, pt[b * n_pages + g * n_pg + j], 0, 0, 0)

    specs = [pl.BlockSpec((1, 1, heads, hd, page), functools.partial(page_map, j=j))
             for j in range(n_pg)]
    return pl.pallas_call(
        functools.partial(_kmean_pages_kernel, n_pg=n_pg, ppb=ppb, rows=float(ppb * page)),
        grid_spec=pltpu.PrefetchScalarGridSpec(
            num_scalar_prefetch=1, grid=(n, n_full * ppb // n_pg), in_specs=specs,
            out_specs=pl.BlockSpec((1, n_pg // ppb, heads, hd), lambda b, g, pt: (b, g, 0, 0))),
        out_shape=jax.ShapeDtypeStruct((n, n_full, heads, hd), F32),
        compiler_params=_cparams("parallel", "arbitrary"),
        name="sample_kmean",
    )(page_table.reshape(-1), *([cache_kt] * n_pg))


def _sample_pick_kernel(q_ref, km_ref, idx_ref, *, heads, n_full, n_sel):
    col = lax.broadcasted_iota(jnp.int32, (q_ref.shape[2], n_full), 1)
    for h in range(heads):
        gate = _dot_nt(q_ref[0, h], km_ref[0, h], precision=HIGHEST)
        _, picks = _top_mask(gate, col, n_sel, n_full, axis=1)
        idx_ref[0, h] = jnp.concatenate([p[0] for p in picks], axis=1)


def _sample_pick(q, kmean, n_sel):
    n, heads, t, hd = q.shape
    n_full = kmean.shape[2]
    return pl.pallas_call(
        functools.partial(_sample_pick_kernel, heads=heads, n_full=n_full, n_sel=n_sel),
        grid=(n,),
        in_specs=[pl.BlockSpec((1, heads, t, hd), lambda b: (b, 0, 0, 0)),
                  pl.BlockSpec((1, heads, n_full, hd), lambda b: (b, 0, 0, 0))],
        out_specs=pl.BlockSpec((1, heads, t, n_sel), lambda b: (b, 0, 0, 0)),
        out_shape=jax.ShapeDtypeStruct((n, heads, t, n_sel), jnp.int32),
        compiler_params=_cparams("parallel"),
        name="sample_pick",
    )(q, kmean)


def _sample_attn_kernel(ph_ref, q_ref, kn_ref, vn_ref, *refs, n_pg):
    del ph_ref
    tq, hd = q_ref.shape[2], q_ref.shape[3]
    k_refs, v_refs, o_ref = refs[:tq * n_pg], refs[tq * n_pg:2 * tq * n_pg], refs[2 * tq * n_pg]
    qs = (q_ref[0, 0] * hd ** -0.5).astype(BF16)
    own_ok = (lax.broadcasted_iota(jnp.int32, (tq, tq), 1)
              <= lax.broadcasted_iota(jnp.int32, (tq, tq), 0))
    s_own = jnp.where(own_ok, _dot_nt(qs, kn_ref[0, 0].astype(BF16)), MASK_VALUE)
    vn = vn_ref[0, 0].astype(BF16)
    for t in range(tq):
        pages = slice(t * n_pg, (t + 1) * n_pg)
        kt = jnp.concatenate([r[0, 0, 0] for r in k_refs[pages]], axis=1).astype(BF16)
        vt = jnp.concatenate([r[0, 0, 0] for r in v_refs[pages]], axis=1).astype(BF16)
        s_sel = _dot(qs, kt)
        m = jnp.maximum(jnp.max(s_sel, axis=1, keepdims=True), jnp.max(s_own, axis=1, keepdims=True))
        p_sel = jnp.exp(s_sel - m)
        p_own = jnp.exp(s_own - m)
        denom = jnp.sum(p_sel, axis=1, keepdims=True) + jnp.sum(p_own, axis=1, keepdims=True)
        o = (_dot_nt(p_sel.astype(BF16), vt) + _dot(p_own.astype(BF16), vn)) / denom
        o_ref[0, 0, t:t + 1, :] = o[t:t + 1]


def _sample_attn(q, k_new, v_new, cache_kt, cache_vt, layer, phys):
    n, heads, t, hd = q.shape
    page = cache_kt.shape[4]
    n_pg = phys.shape[-1]

    def page_map(b, h, ph, j):
        return (layer, ph[(b * heads + h) * (t * n_pg) + j], h, 0, 0)

    pages = [pl.BlockSpec((1, 1, 1, hd, page), functools.partial(page_map, j=j))
             for j in range(t * n_pg)]
    own = pl.BlockSpec((1, 1, t, hd), lambda b, h, ph: (b, h, 0, 0))
    return pl.pallas_call(
        functools.partial(_sample_attn_kernel, n_pg=n_pg),
        grid_spec=pltpu.PrefetchScalarGridSpec(
            num_scalar_prefetch=1, grid=(n, heads),
            in_specs=[own, own, own] + pages * 2, out_specs=own),
        out_shape=jax.ShapeDtypeStruct((n, heads, t, hd), F32),
        compiler_params=_cparams("parallel", "parallel"),
        name="sample_attn",
    )(phys.reshape(-1), q, k_new, v_new, *([cache_kt] * (t * n_pg)), *([cache_vt] * (t * n_pg)))


def _gla_kernel(q_ref, k_ref, v_ref, la_ref, gb_ref, gn_ref, o_ref, sfin_ref, st_sc, qk_sc, b_sc,
                la_sc, attn_sc, *, chunk, sub, n_chunk, n_seq, heads, dk, dv):
    step = pl.program_id(1)

    @pl.when(step == 0)
    def _():
        st_sc[...] = jnp.zeros(st_sc.shape, F32)

    r_i = lax.broadcasted_iota(jnp.int32, (chunk, chunk), 0)
    c_i = lax.broadcasted_iota(jnp.int32, (chunk, chunk), 1)
    tri = r_i >= c_i
    tri_b = tri.astype(BF16)
    row_k = lax.broadcasted_iota(jnp.int32, (chunk, dk), 0)
    q_scale = dk ** -0.5

    def intra_fast(q, k, b):
        rows = []
        for i in range(chunk // sub):
            lo, hi = i * sub, (i + 1) * sub
            ref = b[lo - 1:lo, :] if i > 0 else jnp.zeros((1, dk), F32)
            qt = (q[lo:hi] * jnp.exp(b[lo:hi] - ref)).astype(BF16)
            kt = (k * jnp.exp(jnp.where(row_k < hi, ref - b, -jnp.inf))).astype(BF16)
            rows.append(_dot_nt(qt, kt))
        return jnp.where(tri, jnp.concatenate(rows, axis=0), 0.0)

    def intra_exact(q, k, b, la):
        qk_sc[...] = q
        b_sc[...] = b
        la_sc[...] = la
        col = lax.broadcasted_iota(jnp.int32, (1, chunk), 1)

        def row(t, carry):
            q_t = qk_sc[pl.ds(t, 1), :]
            la_t = la_sc[pl.ds(t, 1), :]
            before = b_sc[pl.ds(t, 1), :] - la_t
            kt = (k * jnp.exp(jnp.where(row_k < t, before - b, -jnp.inf))).astype(BF16)
            a_row = _dot_nt((q_t * jnp.exp(la_t)).astype(BF16), kt)
            k_t = jnp.sum(jnp.where(row_k == t, k, 0.0), axis=0, keepdims=True)
            diag = jnp.sum(q_t * k_t, axis=1, keepdims=True)
            attn_sc[pl.ds(t, 1), :] = jnp.where(col == t, diag, a_row)
            return carry

        lax.fori_loop(0, chunk, row, 0)
        return attn_sc[...]

    def one_chunk(ci, carry, exact):
        off = pl.multiple_of(ci * chunk, chunk)
        rows_ds = pl.ds(off, chunk)
        for s in range(n_seq):
            la = la_ref[s, rows_ds, :]
            la_hi = la.astype(BF16)
            la_mid = (la - la_hi.astype(F32)).astype(BF16)
            la_lo = (la - la_hi.astype(F32) - la_mid.astype(F32)).astype(BF16)
            b_all = _dot(tri_b, la_hi) + _dot(tri_b, la_mid) + _dot(tri_b, la_lo)
            for h in range(heads):
                kc = slice(h * dk, (h + 1) * dk)
                vc = slice(h * dv, (h + 1) * dv)
                q = q_ref[s, rows_ds, kc] * q_scale
                k = k_ref[s, rows_ds, kc]
                v = v_ref[s, rows_ds, vc]
                b = b_all[:, kc]
                st = st_sc[s, h]
                o = _dot_nt((q * jnp.exp(b)).astype(BF16), st.astype(BF16))
                attn = intra_exact(q, k, b, la[:, kc]) if exact else intra_fast(q, k, b)
                o = o + _dot(attn.astype(BF16), v)
                b_last = b[chunk - 1:chunk, :]
                k_dec = (k * jnp.exp(b_last - b)).astype(BF16)
                st_sc[s, h] = st * jnp.exp(b_last) + _dot_tn(v, k_dec)
                gb = gb_ref[s, rows_ds, vc].astype(F32)
                o_ref[s, rows_ds, vc] = (_rms(o) * gn_ref[...] * (gb * jax.nn.sigmoid(gb))).astype(BF16)
        return carry

    safe = jnp.min(la_ref[...]) * sub > GLA_FAST_MIN_LOG_DECAY

    @pl.when(safe)
    def _():
        lax.fori_loop(0, n_chunk, functools.partial(one_chunk, exact=False), 0)

    @pl.when(jnp.logical_not(safe))
    def _():
        lax.fori_loop(0, n_chunk, functools.partial(one_chunk, exact=True), 0)

    @pl.when(step == pl.num_programs(1) - 1)
    def _():
        sfin_ref[...] = st_sc[...]


def _gla_prompt(qb, kb, vb, la, gb, gla_norm_g, b0, n, *, heads, dk, dv):
    _, t, _ = qb.shape
    rows = min(GLA_STEP_ROWS, t)
    chunk = min(GLA_CHUNK, rows)
    n_seq = GLA_SEQS if n % GLA_SEQS == 0 and b0 % GLA_SEQS == 0 else 1
    assert t % rows == 0 and rows % chunk == 0 and chunk % GLA_SUB == 0
    kspec = pl.BlockSpec((n_seq, rows, heads * dk), lambda b, s: (b + b0 // n_seq, s, 0))
    vspec = pl.BlockSpec((n_seq, rows, heads * dv), lambda b, s: (b + b0 // n_seq, s, 0))
    out_spec = pl.BlockSpec((n_seq, rows, heads * dv), lambda b, s: (b, s, 0))
    return pl.pallas_call(
        functools.partial(_gla_kernel, chunk=chunk, sub=GLA_SUB, n_chunk=rows // chunk,
                          n_seq=n_seq, heads=heads, dk=dk, dv=dv),
        grid=(n // n_seq, t // rows),
        in_specs=[kspec, kspec, vspec, kspec, vspec, pl.BlockSpec((1, dv), lambda b, s: (0, 0))],
        out_specs=[out_spec, pl.BlockSpec((n_seq, heads, dv, dk), lambda b, s: (b, 0, 0, 0))],
        out_shape=[jax.ShapeDtypeStruct((n, t, heads * dv), BF16),
                   jax.ShapeDtypeStruct((n, heads, dv, dk), F32)],
        scratch_shapes=[pltpu.VMEM((n_seq, heads, dv, dk), F32), pltpu.VMEM((chunk, dk), F32),
                        pltpu.VMEM((chunk, dk), F32), pltpu.VMEM((chunk, dk), F32),
                        pltpu.VMEM((chunk, chunk), F32)],
        compiler_params=_cparams("parallel", "arbitrary"),
        name="gla_prompt",
    )(qb, kb, vb, la, gb, gla_norm_g)


def _gla_sample_kernel(qt_ref, kt_ref, lat_ref, v_ref, gb_ref, gn_ref, s0_ref, o_ref, sfin_ref, *, dk):
    s = s0_ref[0, 0, 0]
    qt = qt_ref[0, 0] * dk ** -0.5
    kt = kt_ref[0, 0]
    at = jnp.exp(lat_ref[0, 0])
    v = v_ref[0].astype(F32)
    outs = []
    for t in range(v.shape[0]):
        s = at[:, t:t + 1] * s + kt[:, t:t + 1] * v[t:t + 1, :]
        outs.append(jnp.sum(qt[:, t:t + 1] * s, axis=0, keepdims=True))
    o = jnp.concatenate(outs, axis=0)
    gb = gb_ref[0].astype(F32)
    o_ref[0] = (_rms(o) * gn_ref[...] * (gb * jax.nn.sigmoid(gb))).astype(BF16)
    sfin_ref[0, 0] = s


def _gla_sample(qt, kt, lat, vb, gb, gla_norm_g, state, layer, *, dk, dv):
    n, heads, _, t = qt.shape
    tspec = pl.BlockSpec((1, 1, dk, t), lambda b, h: (b, h, 0, 0))
    vspec = pl.BlockSpec((1, t, dv), lambda b, h: (b, 0, h))
    return pl.pallas_call(
        functools.partial(_gla_sample_kernel, dk=dk),
        grid=(n, heads),
        in_specs=[tspec, tspec, tspec, vspec, vspec, pl.BlockSpec((1, dv), lambda b, h: (0, 0)),
                  pl.BlockSpec((1, 1, 1, dk, dv), lambda b, h: (layer, b, h, 0, 0))],
        out_specs=[vspec, pl.BlockSpec((1, 1, dk, dv), lambda b, h: (b, h, 0, 0))],
        out_shape=[jax.ShapeDtypeStruct((n, t, heads * dv), BF16),
                   jax.ShapeDtypeStruct((n, heads, dk, dv), F32)],
        compiler_params=_cparams("parallel", "parallel"),
        name="gla_sample",
    )(qt, kt, lat, vb, gb, gla_norm_g, state)


def _merge_kernel(x_ref, oa_ref, ob_ref, g1_ref, wa_ref, wb_ref, wmg_ref, wo_ref, g2_ref, wrh_ref,
                  wrl_ref, br_ref, x2_ref, h2_ref, idx_ref, gate_ref, *, d, n_exp):
    x = x_ref[...]
    hb = (_rms(x) * g1_ref[...]).astype(BF16)
    y_a = _dot(oa_ref[...], wa_ref[...])
    y_b = _dot(ob_ref[...], wb_ref[...])
    mixed = (jax.nn.sigmoid(_dot(hb, wmg_ref[:, :d])) * y_a
             + jax.nn.sigmoid(_dot(hb, wmg_ref[:, d:])) * y_b)
    x2 = x + _dot(mixed.astype(BF16), wo_ref[...])
    x2_ref[...] = x2
    h2 = _rms(x2) * g2_ref[...]
    h2_hi = h2.astype(BF16)
    h2_ref[...] = h2_hi
    h2_lo = (h2 - h2_hi.astype(F32)).astype(BF16)
    logits = (_dot(h2_hi, wrh_ref[...]) + _dot(h2_hi, wrl_ref[...]) + _dot(h2_lo, wrh_ref[...])
              + br_ref[...])
    col = lax.broadcasted_iota(jnp.int32, logits.shape, 1)
    _, picks = _top_mask(logits, col, MOE_TOP_K, n_exp, axis=1)
    top_val = jnp.concatenate([p[1] for p in picks], axis=1)
    e = jnp.exp(top_val - top_val[:, :1])
    idx_ref[...] = jnp.concatenate([p[0] for p in picks], axis=1)
    gate_ref[...] = e / jnp.sum(e, axis=1, keepdims=True)


def _merge(x, row0, o_a, o_b, norm1_g, w_a, w_b, w_mg, w_o, norm2_g, w_router, b_router):
    n_tok, d = o_a.shape[0], x.shape[1]
    n_exp = w_router.shape[1]
    w_router_hi = w_router.astype(BF16)
    w_router_lo = (w_router - w_router_hi.astype(F32)).astype(BF16)
    tm = min(PROJ_ROWS, n_tok)
    assert n_tok % tm == 0 and row0 % tm == 0
    full = lambda a: pl.BlockSpec(a.shape, lambda i: (0,) * a.ndim)
    rows = lambda w: pl.BlockSpec((tm, w), lambda i: (i, 0))
    return pl.pallas_call(
        functools.partial(_merge_kernel, d=d, n_exp=n_exp),
        grid=(n_tok // tm,),
        in_specs=[pl.BlockSpec((tm, d), lambda i: (i + row0 // tm, 0)),
                  rows(o_a.shape[1]), rows(o_b.shape[1]), full(norm1_g), full(w_a),
                  full(w_b), full(w_mg), full(w_o), full(norm2_g), full(w_router_hi),
                  full(w_router_lo), full(b_router)],
        out_specs=[rows(d), rows(d), rows(MOE_TOP_K), rows(MOE_TOP_K)],
        out_shape=[jax.ShapeDtypeStruct((n_tok, d), F32), jax.ShapeDtypeStruct((n_tok, d), BF16),
                   jax.ShapeDtypeStruct((n_tok, MOE_TOP_K), jnp.int32),
                   jax.ShapeDtypeStruct((n_tok, MOE_TOP_K), F32)],
        compiler_params=_cparams("parallel"),
        name="merge",
    )(x, o_a, o_b, norm1_g, w_a, w_b, w_mg, w_o, norm2_g, w_router_hi, w_router_lo, b_router)


def _expert_kernel(te_ref, nt_ref, x_ref, wup_ref, bg_ref, bl_ref, wdn_ref, bdn_ref, y_ref,
                   wg_sc, wl_sc, wd_sc, t_sc, *, slab):
    i = pl.program_id(0)

    @pl.when((i == 0) | (te_ref[i] != te_ref[jnp.maximum(i - 1, 0)]))
    def _():
        for c in range(0, wup_ref.shape[2], slab):
            rows = slice(c // 2, (c + slab) // 2)
            for j in range(t_sc.shape[0]):
                lanes = slice(j * LANES, (j + 1) * LANES)
                t_sc[j] = wup_ref[0, lanes, c:c + slab].T
                wg_sc[rows, lanes] = t_sc[j, pl.ds(0, slab // 2, stride=2), :].astype(BF16)
                wl_sc[rows, lanes] = t_sc[j, pl.ds(1, slab // 2, stride=2), :].astype(BF16)
        wd_sc[...] = wdn_ref[0].astype(BF16)

    @pl.when(i < nt_ref[0])
    def _():
        x = x_ref[...]
        x_glu = jnp.minimum(_dot_nt(x, wg_sc[...]) + bg_ref[0], SWIGLU_LIMIT)
        x_lin = jnp.clip(_dot_nt(x, wl_sc[...]) + bl_ref[0], -SWIGLU_LIMIT, SWIGLU_LIMIT)
        a = x_glu * jax.nn.sigmoid(SWIGLU_ALPHA * x_glu) * (x_lin + 1.0)
        y_ref[...] = (_dot(a.astype(BF16), wd_sc[...]) + bdn_ref[0]).astype(y_ref.dtype)

    @pl.when(i >= nt_ref[0])
    def _():
        y_ref[...] = jnp.zeros(y_ref.shape, y_ref.dtype)


def _experts(buf, tile_e, n_used, w_up, b_glu, b_lin, w_dn, b_dn, *, tile_rows):
    r, d = buf.shape
    d_ff = w_dn.shape[1]
    slab = min(2 * LANES, 2 * d_ff)
    by_expert = lambda *blk: pl.BlockSpec((1,) + blk, lambda i, te, nt: (te[i], 0, 0))
    return pl.pallas_call(
        functools.partial(_expert_kernel, slab=slab),
        grid_spec=pltpu.PrefetchScalarGridSpec(
            num_scalar_prefetch=2, grid=(r // tile_rows,),
            in_specs=[pl.BlockSpec((tile_rows, d), lambda i, te, nt: (i, 0)),
                      by_expert(d, 2 * d_ff), by_expert(1, d_ff), by_expert(1, d_ff),
                      by_expert(d_ff, d), by_expert(1, d)],
            out_specs=pl.BlockSpec((tile_rows, d), lambda i, te, nt: (i, 0)),
            scratch_shapes=[pltpu.VMEM((d_ff, d), BF16), pltpu.VMEM((d_ff, d), BF16),
                            pltpu.VMEM((d_ff, d), BF16), pltpu.VMEM((d // LANES, slab, LANES), F32)]),
        out_shape=jax.ShapeDtypeStruct((r, d), BF16),
        compiler_params=_cparams("arbitrary"),
        name="experts",
    )(tile_e, n_used, buf, w_up, b_glu, b_lin, w_dn, b_dn)


def _rank_kernel(idx_ref, rank_ref, cnt_ref, seen_sc):
    @pl.when(pl.program_id(0) == 0)
    def _():
        seen_sc[...] = jnp.zeros(seen_sc.shape, F32)

    idx = idx_ref[...]
    tr, top_k = idx.shape
    lane = lax.broadcasted_iota(jnp.int32, (tr, LANES), 1)
    hits = [idx[:, k:k + 1] == lane for k in range(top_k)]
    routed = hits[0].astype(F32)
    for h in hits[1:]:
        routed = routed + h.astype(F32)
    earlier = (lax.broadcasted_iota(jnp.int32, (tr, tr), 1)
               < lax.broadcasted_iota(jnp.int32, (tr, tr), 0)).astype(BF16)
    before = _dot(earlier, routed.astype(BF16)) + seen_sc[...]
    rank_ref[...] = jnp.concatenate(
        [jnp.sum(jnp.where(h, before, 0.0), axis=1, keepdims=True) for h in hits],
        axis=1).astype(jnp.int32)
    seen_sc[...] = seen_sc[...] + jnp.sum(routed, axis=0, keepdims=True)
    cnt_ref[...] = seen_sc[...]


def _rank(top_idx):
    n_tok, top_k = top_idx.shape
    tr = min(ROUTE_ROWS, n_tok)
    assert n_tok % tr == 0
    return pl.pallas_call(
        _rank_kernel,
        grid=(n_tok // tr,),
        in_specs=[pl.BlockSpec((tr, top_k), lambda i: (i, 0))],
        out_specs=[pl.BlockSpec((tr, top_k), lambda i: (i, 0)),
                   pl.BlockSpec((1, LANES), lambda i: (0, 0))],
        out_shape=[jax.ShapeDtypeStruct((n_tok, top_k), jnp.int32),
                   jax.ShapeDtypeStruct((1, LANES), F32)],
        scratch_shapes=[pltpu.VMEM((1, LANES), F32)],
        compiler_params=_cparams("arbitrary"),
        name="moe_rank",
    )(top_idx)


def _combine_kernel(x_ref, y_ref, g_ref, o_ref):
    acc = x_ref[...]
    g = g_ref[...]
    for k in range(y_ref.shape[0]):
        acc = acc + g[:, k:k + 1] * y_ref[k].astype(F32)
    o_ref[...] = acc


def _combine(x2, y_kt, gate):
    top_k, n_tok, d = y_kt.shape
    tm = min(ROUTE_ROWS, n_tok)
    assert n_tok % tm == 0
    return pl.pallas_call(
        _combine_kernel,
        grid=(n_tok // tm,),
        in_specs=[pl.BlockSpec((tm, d), lambda i: (i, 0)),
                  pl.BlockSpec((top_k, tm, d), lambda i: (0, i, 0)),
                  pl.BlockSpec((tm, top_k), lambda i: (i, 0))],
        out_specs=pl.BlockSpec((tm, d), lambda i: (i, 0)),
        out_shape=jax.ShapeDtypeStruct((n_tok, d), F32),
        compiler_params=_cparams("parallel"),
        name="moe_combine",
    )(x2, y_kt, gate)


def _moe(x2, h2, top_idx, gate, expert_weights, tile_rows):
    n_tok, d = h2.shape
    n_exp = expert_weights[0].shape[0]
    assert n_exp <= LANES
    n_assign = n_tok * MOE_TOP_K
    rank, cnt = _rank(top_idx)
    counts = cnt[0, :n_exp].astype(jnp.int32)
    padded = (counts + tile_rows - 1) // tile_rows * tile_rows
    pend = jnp.cumsum(padded)
    experts = jnp.arange(n_exp, dtype=jnp.int32)
    first_row = jnp.sum(jnp.where(top_idx[:, :, None] == experts, pend - padded, 0), axis=-1)
    dest = (first_row + rank).reshape(-1)
    n_tiles = -(-n_assign // tile_rows) + n_exp
    tile_row0 = jnp.arange(n_tiles, dtype=jnp.int32) * tile_rows
    tile_e = jnp.minimum(jnp.sum((pend[None, :] <= tile_row0[:, None]).astype(jnp.int32), axis=1),
                         n_exp - 1)
    n_used = (pend[-1:] // tile_rows).astype(jnp.int32)
    src = jnp.zeros((n_tiles * tile_rows,), jnp.int32).at[dest].set(
        jnp.arange(n_assign, dtype=jnp.int32) // MOE_TOP_K)
    y = _experts(h2[src], tile_e, n_used, *expert_weights, tile_rows=tile_rows)
    y_kt = y[dest.reshape(n_tok, MOE_TOP_K).T]
    return _combine(x2, y_kt, gate)


def _rope_tables_t(pos, hd):
    half = hd // 2
    inv = ROPE_THETA ** (-jnp.arange(half, dtype=F32) / half)
    ang = inv[:, None] * pos.astype(F32)[None, :]
    cos, sin = jnp.cos(ang), jnp.sin(ang)
    return jnp.concatenate([cos, cos], axis=0), jnp.concatenate([-sin, sin], axis=0)


def kernel(x_prompt, x_sample, cache_k, cache_v, state_gla, page_table, norm1_g, w_in, q_norm_g,
           k_norm_g, w_gate_up, b_gate_up, gla_norm_g, w_branch_a, w_branch_b, w_merge_gate, w_out,
           norm2_g, w_router, b_router, w_up, b_up, w_down, b_down):
    depth = norm1_g.shape[0]
    assert depth == 1, "sample/prompt streams are only chained through one layer here"
    layer = 0
    n_p, t_p, d = x_prompt.shape
    n_s, t_s, _ = x_sample.shape
    heads, page, hd = cache_k.shape[2:]
    b_heads, dk, dv = state_gla.shape[2:]
    bk, bv = b_heads * dk, b_heads * dv
    aqk = heads * hd
    rank = w_gate_up.shape[1]
    past_len = page_table.shape[1] * page
    ppb = MOBA_BLOCK // page
    n_full = past_len // MOBA_BLOCK
    assert past_len % MOBA_BLOCK == 0 and n_full >= MOBA_TOPK and rank <= LANES
    assert w_in.shape[2] == 3 * aqk + 2 * bk + 2 * bv + rank

    w_qkv_t = w_in[layer][:, :3 * aqk].T.astype(BF16)
    w_rest = jnp.pad(w_in[layer][:, 3 * aqk:], ((0, 0), (0, LANES - rank))).astype(BF16)
    w_gu_p = jnp.pad(w_gate_up[layer], ((0, LANES - rank), (0, 0))).astype(BF16)
    g1 = norm1_g[layer][None, :]
    g2 = norm2_g[layer][None, :]
    qg = q_norm_g[layer][:, None]
    kg = k_norm_g[layer][:, None]
    bgu = b_gate_up[layer][None, :]
    gn = gla_norm_g[layer][None, :]
    w_a = w_branch_a[layer].astype(BF16)
    w_b = w_branch_b[layer].astype(BF16)
    w_mg = w_merge_gate[layer].astype(BF16)
    w_o = w_out[layer].astype(BF16)
    w_r = w_router[layer]
    b_r = b_router[layer][None, :]
    expert_weights = (w_up[layer], b_up[layer][:, None, 0::2], b_up[layer][:, None, 1::2],
                      w_down[layer], b_down[layer][:, None, :])
    proj = functools.partial(_project, norm1_g=g1, w_qkv_t=w_qkv_t, w_rest=w_rest, q_norm_g=qg,
                             k_norm_g=kg, w_gu_p=w_gu_p, b_gu=bgu, heads=heads, hd=hd, bk=bk, bv=bv)
    merge = functools.partial(_merge, norm1_g=g1, w_a=w_a, w_b=w_b, w_mg=w_mg, w_o=w_o,
                              norm2_g=g2, w_router=w_r, b_router=b_r)

    cos_p, sin_p = _rope_tables_t(jnp.arange(t_p), hd)
    qt, kt, vt, qb, kb, vb, la, gb = proj(x_prompt, cos_p, sin_p)
    n_part = n_p // PROMPT_SPLITS if n_p % PROMPT_SPLITS == 0 else n_p
    x_rows = x_prompt.reshape(n_p * t_p, d)
    y_parts, st_parts = [], []
    for b0 in range(0, n_p, n_part):
        o_a = _moba_prompt(qt, kt, vt, b0, n_part)
        o_b, st = _gla_prompt(qb, kb, vb, la, gb, gn, b0, n_part, heads=b_heads, dk=dk, dv=dv)
        x2, h2, idx, gate = merge(x_rows, b0 * t_p, o_a.reshape(n_part * t_p, -1),
                                  o_b.reshape(n_part * t_p, -1))
        y_parts.append(_moe(x2, h2, idx, gate, expert_weights, MOE_ROWS))
        st_parts.append(st)
    y_prompt = jnp.concatenate(y_parts).reshape(n_p, t_p, d)
    s_prompt = jnp.swapaxes(jnp.concatenate(st_parts), 2, 3)

    n_tok_s = n_s * t_s
    cos_s, sin_s = _rope_tables_t(past_len + jnp.arange(t_s), hd)
    qt_s, kt_s, vt_s, qb_s, kb_s, vb_s, la_s, gb_s = proj(
        x_sample.reshape(1, n_tok_s, d), jnp.tile(cos_s, (1, n_s)), jnp.tile(sin_s, (1, n_s)))
    per_seq = lambda a: a.reshape(heads, hd, n_s, t_s).transpose(2, 0, 3, 1)
    qa_s, ka_s, va_s = per_seq(qt_s), per_seq(kt_s), per_seq(vt_s)
    cache_kt = jnp.swapaxes(cache_k, 3, 4)
    cache_vt = jnp.swapaxes(cache_v, 3, 4)
    kmean = _sample_kmean(cache_kt, layer, page_table, n_full, ppb).transpose(0, 2, 1, 3)
    blk_idx = _sample_pick(qa_s, kmean, MOBA_TOPK)
    lpages = blk_idx[..., None] * ppb + jnp.arange(ppb, dtype=jnp.int32)
    phys = jnp.take_along_axis(page_table[:, None, None, :],
                               lpages.reshape(n_s, heads, t_s, MOBA_TOPK * ppb), axis=-1)
    o_a_s = _sample_attn(qa_s, ka_s, va_s, cache_kt, cache_vt, layer, phys)
    o_a_s = o_a_s.transpose(0, 2, 1, 3).reshape(n_tok_s, aqk).astype(BF16)
    tr = lambda a: a.reshape(n_s, t_s, b_heads, dk).transpose(0, 2, 3, 1)
    o_b_s, s_sample = _gla_sample(tr(qb_s), tr(kb_s), tr(la_s), vb_s.reshape(n_s, t_s, bv),
                                  gb_s.reshape(n_s, t_s, bv), gn, state_gla, layer, dk=dk, dv=dv)
    x2_s, h2_s, idx_s, gate_s = merge(x_sample.reshape(n_tok_s, d), 0, o_a_s,
                                      o_b_s.reshape(n_tok_s, bv))
    y_sample = _moe(x2_s, h2_s, idx_s, gate_s, expert_weights,
                    min(MOE_ROWS, n_tok_s)).reshape(n_s, t_s, d)
    k_prompt = jnp.swapaxes(kt, 2, 3)
    v_prompt = jnp.swapaxes(vt, 2, 3)
    return (y_prompt, y_sample, k_prompt[None], v_prompt[None], ka_s[None], va_s[None],
            s_prompt[None], s_sample[None])
```

```python
import functools

import jax
import jax.numpy as jnp
from jax import lax
from jax.experimental import pallas as pl
from jax.experimental.pallas import tpu as pltpu

F32 = jnp.float32
BF16 = jnp.bfloat16
HIGHEST = lax.Precision.HIGHEST

MOBA_BLOCK = 256
MOBA_TOPK = 3
ROPE_THETA = 10000.0
GLA_GATE_NORM = 16.0
NORM_EPS = 1e-6
MOE_TOP_K = 4
SWIGLU_ALPHA = 1.702
SWIGLU_LIMIT = 7.0

LANES = 128
PAD_ROWS = 16
VMEM_LIMIT_BYTES = 56 * 1024 * 1024

MASK_VALUE = -1e30
LOG2_E = 1.4426950408889634
PROJ_ROWS = 512
GLA_CHUNK = 64
GLA_SUB = 16
GLA_STEP_ROWS = 512
GLA_SEQS = 2
GLA_FAST_MIN_LOG_DECAY = -80.0
MOE_ROWS = 512
ROUTE_ROWS = 1024
PROMPT_SPLITS = 1
MOBA_GROUP = 4
SAMPLE_KMEAN_PAGES = 16


def _cparams(*sem):
    return pltpu.CompilerParams(dimension_semantics=sem, vmem_limit_bytes=VMEM_LIMIT_BYTES)


def _rms(x):
    return x * lax.rsqrt(jnp.mean(x * x, axis=-1, keepdims=True) + NORM_EPS)


def _dot(a, b):
    return jnp.dot(a, b, preferred_element_type=F32)


def _dot_nt(a, b, precision=None):
    return lax.dot_general(a, b, (((1,), (1,)), ((), ())), precision=precision,
                           preferred_element_type=F32)


def _dot_tn(a, b):
    return lax.dot_general(a, b, (((0,), (0,)), ((), ())), preferred_element_type=F32)


def _top_mask(g, pos, n_pick, n_pos, axis):
    sel = jnp.zeros(g.shape, F32)
    picks = []
    for _ in range(n_pick):
        mx = jnp.max(g, axis=axis, keepdims=True)
        cand = jnp.where((g == mx) & (mx > -jnp.inf), pos, n_pos)
        idx = jnp.min(cand, axis=axis, keepdims=True)
        pick = pos == idx
        sel = jnp.where(pick, 1.0, sel)
        g = jnp.where(pick, -jnp.inf, g)
        picks.append((idx, mx))
    return sel, picks


def _proj_kernel(x_ref, g1_ref, wqkv_ref, wrest_ref, qg_ref, kg_ref, wgu_ref, bgu_ref, cos_ref,
                 sin_ref, qt_ref, kt_ref, vt_ref, qb_ref, kb_ref, vb_ref, la_ref, gb_ref,
                 *, heads, hd, bk, bv):
    x = x_ref[0]
    hb = (_rms(x) * g1_ref[...]).astype(BF16)
    cos = cos_ref[...]
    sin = sin_ref[...]
    aqk = heads * hd

    def mm_t(lo, hi):
        return _dot_nt(wqkv_ref[lo:hi, :], hb)

    def norm_rope_t(z, g_ref, out_ref):
        for h in range(heads):
            y = z[h * hd:(h + 1) * hd, :]
            y = y * lax.rsqrt(jnp.mean(y * y, axis=0, keepdims=True) + NORM_EPS) * g_ref[...]
            rot = jnp.concatenate([y[hd // 2:], y[:hd // 2]], axis=0)
            out_ref[0, h] = y * cos + rot * sin

    norm_rope_t(mm_t(0, aqk), qg_ref, qt_ref)
    norm_rope_t(mm_t(aqk, 2 * aqk), kg_ref, kt_ref)
    zv = mm_t(2 * aqk, 3 * aqk)
    for h in range(heads):
        vt_ref[0, h] = zv[h * hd:(h + 1) * hd, :]

    def mm(lo, hi):
        return _dot(hb, wrest_ref[:, lo:hi])

    qb_ref[0] = mm(0, bk)
    kb_ref[0] = mm(bk, 2 * bk)
    vb_ref[0] = mm(2 * bk, 2 * bk + bv).astype(BF16)
    gb_ref[0] = mm(2 * bk + bv, 2 * bk + 2 * bv).astype(BF16)
    lr = mm(2 * bk + 2 * bv, 2 * bk + 2 * bv + LANES)
    u = _dot(lr.astype(BF16), wgu_ref[...]) + bgu_ref[...]
    log_sig = jnp.minimum(u, 0.0) - jnp.log(1.0 + jnp.exp(-jnp.abs(u)))
    la_ref[0] = log_sig / GLA_GATE_NORM


def _project(x, cos_t, sin_t, norm1_g, w_qkv_t, w_rest, q_norm_g, k_norm_g, w_gu_p, b_gu,
             *, heads, hd, bk, bv):
    n, t, d = x.shape
    tm = min(PROJ_ROWS, t)
    assert t % tm == 0
    full = lambda a: pl.BlockSpec(a.shape, lambda b, i: (0,) * a.ndim)
    head_out = pl.BlockSpec((1, heads, hd, tm), lambda b, i: (b, 0, 0, i))
    row_out = lambda w: pl.BlockSpec((1, tm, w), lambda b, i: (b, i, 0))
    table = pl.BlockSpec((hd, tm), lambda b, i: (0, i))
    head_shape = jax.ShapeDtypeStruct((n, heads, hd, t), F32)
    return pl.pallas_call(
        functools.partial(_proj_kernel, heads=heads, hd=hd, bk=bk, bv=bv),
        grid=(n, t // tm),
        in_specs=[pl.BlockSpec((1, tm, d), lambda b, i: (b, i, 0)),
                  full(norm1_g), full(w_qkv_t), full(w_rest), full(q_norm_g), full(k_norm_g),
                  full(w_gu_p), full(b_gu), table, table],
        out_specs=[head_out, head_out, head_out, row_out(bk), row_out(bk), row_out(bv),
                   row_out(bk), row_out(bv)],
        out_shape=[head_shape, head_shape, head_shape,
                   jax.ShapeDtypeStruct((n, t, bk), F32),
                   jax.ShapeDtypeStruct((n, t, bk), F32),
                   jax.ShapeDtypeStruct((n, t, bv), BF16),
                   jax.ShapeDtypeStruct((n, t, bk), F32),
                   jax.ShapeDtypeStruct((n, t, bv), BF16)],
        compiler_params=_cparams("parallel", "parallel"),
        name="proj",
    )(x, norm1_g, w_qkv_t, w_rest, q_norm_g, k_norm_g, w_gu_p, b_gu, cos_t, sin_t)


def _moba_kernel(qt_ref, kt_ref, vt_ref, o_ref, k_sc, vt_sc, kmean_sc, qs_sc, bias_sc, m_sc, acc_sc,
                 sa_sc, sb_sc, ma_sc, mb_sc, *, nb, blk, hd, hp, grp):
    c = pl.program_id(2)
    nq = hp * blk
    vr = hd + PAD_ROWS
    n_groups = nb // grp
    cols = [slice(i * blk, (i + 1) * blk) for i in range(hp)]

    @pl.when(c == 0)
    def _():
        row = lax.broadcasted_iota(jnp.int32, (LANES - hd, blk), 0)
        ones_row = (lax.broadcasted_iota(jnp.int32, (PAD_ROWS, blk), 0) == 0).astype(BF16)

        def stage_kv(j, carry):
            off = pl.multiple_of(j * blk, blk)
            place = (row == j % grp).astype(F32)
            for i in range(hp):
                k_nat = jnp.concatenate([kt_ref[0, i, :, pl.ds(off, blk)], place], axis=0).T
                k_sc[i, pl.ds(off, blk), :] = k_nat.astype(BF16)
                kmean_sc[i, pl.ds(j, 1), :] = jnp.mean(k_nat, axis=0, keepdims=True)
                vt_sc[i * vr:i * vr + hd, pl.ds(off, blk)] = vt_ref[0, i, :, pl.ds(off, blk)].astype(BF16)
                vt_sc[i * vr + hd:(i + 1) * vr, pl.ds(off, blk)] = ones_row
            return carry

        lax.fori_loop(0, nb, stage_kv, 0)

    zero_rows = jnp.zeros((LANES - hd, blk), F32)
    q_pad = [jnp.concatenate([qt_ref[0, i], zero_rows], axis=0) for i in range(hp)]
    gate = jnp.concatenate(
        [jnp.dot(kmean_sc[i], q_pad[i], precision=HIGHEST, preferred_element_type=F32)
         for i in range(hp)], axis=1)
    blk_row = lax.broadcasted_iota(jnp.int32, (nb, nq), 0)
    sel, _ = _top_mask(jnp.where(blk_row < c, gate, -jnp.inf), blk_row, min(MOBA_TOPK, nb - 1),
                       nb, axis=0)
    bias = (sel - 1.0) * -MASK_VALUE
    pad = jnp.zeros((PAD_ROWS - grp, blk), F32)
    for g in range(n_groups):
        for i in range(hp):
            bias_sc[g, i] = jnp.concatenate([bias[g * grp:(g + 1) * grp, cols[i]], pad],
                                            axis=0).astype(BF16)
    for i in range(hp):
        qs_sc[i] = (q_pad[i] * (hd ** -0.5 * LOG2_E)).astype(BF16)

    def scores(off, n_keys):
        return jnp.concatenate([_dot(k_sc[i, pl.ds(off, n_keys), :], qs_sc[i]) for i in range(hp)],
                               axis=1)

    def update(s, m_blk, off, n_keys, first):
        m_new = m_blk if first else jnp.maximum(m_sc[...], m_blk)
        pb = jnp.exp2(s - m_new).astype(BF16)
        if not first:
            alpha = jnp.exp2(m_sc[...] - m_new)
        m_sc[...] = m_new
        for i in range(hp):
            pv = _dot(vt_sc[i * vr:(i + 1) * vr, pl.ds(off, n_keys)], pb[:, cols[i]])
            acc_sc[:, cols[i]] = pv if first else alpha[:, cols[i]] * acc_sc[:, cols[i]] + pv

    own_off = pl.multiple_of(c * blk, blk)
    causal = (lax.broadcasted_iota(jnp.int32, (blk, nq), 0)
              <= lax.broadcasted_iota(jnp.int32, (blk, nq), 1) % blk)
    s_own = jnp.where(causal, scores(own_off, blk), MASK_VALUE)

    n_grp = (c + grp - 1) // grp

    def stage(g, s_buf, mx_buf):
        g = jnp.minimum(g, n_groups - 1)
        for i in range(hp):
            qs_sc[i, hd:hd + PAD_ROWS, :] = bias_sc[g, i]
        s = scores(pl.multiple_of(g * (grp * blk), grp * blk), grp * blk)
        s_buf[...] = s
        mx_buf[...] = jnp.max(s, axis=0, keepdims=True)

    def consume(g, s_buf, mx_buf):
        update(s_buf[...], mx_buf[...], pl.multiple_of(g * (grp * blk), grp * blk), grp * blk, False)

    stage(0, sa_sc, ma_sc)
    update(s_own, jnp.max(s_own, axis=0, keepdims=True), own_off, blk, True)

    def body(u, carry):
        stage(2 * u + 1, sb_sc, mb_sc)
        consume(2 * u, sa_sc, ma_sc)
        stage(2 * u + 2, sa_sc, ma_sc)
        consume(2 * u + 1, sb_sc, mb_sc)
        return carry

    lax.fori_loop(0, n_grp // 2, body, 0)

    @pl.when(n_grp % 2 == 1)
    def _():
        consume(n_grp - 1, sa_sc, ma_sc)

    out_t = acc_sc[:hd, :] / acc_sc[hd:hd + 1, :]
    out_t = jnp.concatenate([out_t[:, cols[i]] for i in range(hp)], axis=0)
    o_ref[0] = out_t.T.astype(BF16)


def _moba_prompt(qt, kt, vt, b0, n):
    _, heads, hd, t = qt.shape
    blk = MOBA_BLOCK
    assert t % blk == 0 and LANES % hd == 0
    hp = LANES // hd
    assert heads % hp == 0
    nb = t // blk
    grp = MOBA_GROUP if nb % MOBA_GROUP == 0 else 1
    nq = hp * blk
    kv_spec = pl.BlockSpec((1, hp, hd, t), lambda b, g, c: (b + b0, g, 0, 0))
    return pl.pallas_call(
        functools.partial(_moba_kernel, nb=nb, blk=blk, hd=hd, hp=hp, grp=grp),
        grid=(n, heads // hp, nb),
        in_specs=[pl.BlockSpec((1, hp, hd, blk), lambda b, g, c: (b + b0, g, 0, c)), kv_spec, kv_spec],
        out_specs=pl.BlockSpec((1, blk, hp * hd), lambda b, g, c: (b, c, g)),
        out_shape=jax.ShapeDtypeStruct((n, t, heads * hd), BF16),
        scratch_shapes=[pltpu.VMEM((hp, t, LANES), BF16),
                        pltpu.VMEM((hp * (hd + PAD_ROWS), t), BF16),
                        pltpu.VMEM((hp, nb, LANES), F32),
                        pltpu.VMEM((hp, LANES, blk), BF16),
                        pltpu.VMEM((nb // grp, hp, PAD_ROWS, blk), BF16),
                        pltpu.VMEM((1, nq), F32),
                        pltpu.VMEM((hd + PAD_ROWS, nq), F32),
                        pltpu.VMEM((grp * blk, nq), F32), pltpu.VMEM((grp * blk, nq), F32),
                        pltpu.VMEM((1, nq), F32), pltpu.VMEM((1, nq), F32)],
        compiler_params=_cparams("parallel", "parallel", "arbitrary"),
        name="moba_prompt",
    )(qt, kt, vt)


def _kmean_pages_kernel(pt_ref, *refs, n_pg, ppb, rows):
    del pt_ref
    out_ref = refs[n_pg]
    for b in range(n_pg // ppb):
        acc = refs[b * ppb][0, 0]
        for j in range(1, ppb):
            acc = acc + refs[b * ppb + j][0, 0]
        out_ref[0, b] = jnp.sum(acc, axis=-1) / rows


def _sample_kmean(cache_kt, layer, page_table, n_full, ppb):
    _, _, heads, hd, page = cache_kt.shape
    n, n_pages = page_table.shape
    n_pg = SAMPLE_KMEAN_PAGES
    assert (n_full * ppb) % n_pg == 0 and n_pg % ppb == 0

    def page_map(b, g, pt, j):
        return (layer, pt[b * n_pages + g * n_pg + j], 0, 0, 0)

    specs = [pl.BlockSpec((1, 1, heads, hd, page), functools.partial(page_map, j=j))
             for j in range(n_pg)]
    return pl.pallas_call(
        functools.partial(_kmean_pages_kernel, n_pg=n_pg, ppb=ppb, rows=float(ppb * page)),
        grid_spec=pltpu.PrefetchScalarGridSpec(
            num_scalar_prefetch=1, grid=(n, n_full * ppb // n_pg), in_specs=specs,
            out_specs=pl.BlockSpec((1, n_pg // ppb, heads, hd), lambda b, g, pt: (b, g, 0, 0))),
        out_shape=jax.ShapeDtypeStruct((n, n_full, heads, hd), F32),
        compiler_params=_cparams("parallel", "arbitrary"),
        name="sample_kmean",
    )(page_table.reshape(-1), *([cache_kt] * n_pg))


def _sample_pick_kernel(q_ref, km_ref, idx_ref, *, heads, n_full, n_sel):
    col = lax.broadcasted_iota(jnp.int32, (q_ref.shape[2], n_full), 1)
    for h in range(heads):
        gate = _dot_nt(q_ref[0, h], km_ref[0, h], precision=HIGHEST)
        _, picks = _top_mask(gate, col, n_sel, n_full, axis=1)
        idx_ref[0, h] = jnp.concatenate([p[0] for p in picks], axis=1)


def _sample_pick(q, kmean, n_sel):
    n, heads, t, hd = q.shape
    n_full = kmean.shape[2]
    return pl.pallas_call(
        functools.partial(_sample_pick_kernel, heads=heads, n_full=n_full, n_sel=n_sel),
        grid=(n,),
        in_specs=[pl.BlockSpec((1, heads, t, hd), lambda b: (b, 0, 0, 0)),
                  pl.BlockSpec((1, heads, n_full, hd), lambda b: (b, 0, 0, 0))],
        out_specs=pl.BlockSpec((1, heads, t, n_sel), lambda b: (b, 0, 0, 0)),
        out_shape=jax.ShapeDtypeStruct((n, heads, t, n_sel), jnp.int32),
        compiler_params=_cparams("parallel"),
        name="sample_pick",
    )(q, kmean)


def _sample_attn_kernel(ph_ref, q_ref, kn_ref, vn_ref, *refs, n_pg):
    del ph_ref
    tq, hd = q_ref.shape[2], q_ref.shape[3]
    k_refs, v_refs, o_ref = refs[:tq * n_pg], refs[tq * n_pg:2 * tq * n_pg], refs[2 * tq * n_pg]
    qs = (q_ref[0, 0] * hd ** -0.5).astype(BF16)
    own_ok = (lax.broadcasted_iota(jnp.int32, (tq, tq), 1)
              <= lax.broadcasted_iota(jnp.int32, (tq, tq), 0))
    s_own = jnp.where(own_ok, _dot_nt(qs, kn_ref[0, 0].astype(BF16)), MASK_VALUE)
    vn = vn_ref[0, 0].astype(BF16)
    for t in range(tq):
        pages = slice(t * n_pg, (t + 1) * n_pg)
        kt = jnp.concatenate([r[0, 0, 0] for r in k_refs[pages]], axis=1).astype(BF16)
        vt = jnp.concatenate([r[0, 0, 0] for r in v_refs[pages]], axis=1).astype(BF16)
        s_sel = _dot(qs, kt)
        m = jnp.maximum(jnp.max(s_sel, axis=1, keepdims=True), jnp.max(s_own, axis=1, keepdims=True))
        p_sel = jnp.exp(s_sel - m)
        p_own = jnp.exp(s_own - m)
        denom = jnp.sum(p_sel, axis=1, keepdims=True) + jnp.sum(p_own, axis=1, keepdims=True)
        o = (_dot_nt(p_sel.astype(BF16), vt) + _dot(p_own.astype(BF16), vn)) / denom
        o_ref[0, 0, t:t + 1, :] = o[t:t + 1]


def _sample_attn(q, k_new, v_new, cache_kt, cache_vt, layer, phys):
    n, heads, t, hd = q.shape
    page = cache_kt.shape[4]
    n_pg = phys.shape[-1]

    def page_map(b, h, ph, j):
        return (layer, ph[(b * heads + h) * (t * n_pg) + j], h, 0, 0)

    pages = [pl.BlockSpec((1, 1, 1, hd, page), functools.partial(page_map, j=j))
             for j in range(t * n_pg)]
    own = pl.BlockSpec((1, 1, t, hd), lambda b, h, ph: (b, h, 0, 0))
    return pl.pallas_call(
        functools.partial(_sample_attn_kernel, n_pg=n_pg),
        grid_spec=pltpu.PrefetchScalarGridSpec(
            num_scalar_prefetch=1, grid=(n, heads),
            in_specs=[own, own, own] + pages * 2, out_specs=own),
        out_shape=jax.ShapeDtypeStruct((n, heads, t, hd), F32),
        compiler_params=_cparams("parallel", "parallel"),
        name="sample_attn",
    )(phys.reshape(-1), q, k_new, v_new, *([cache_kt] * (t * n_pg)), *([cache_vt] * (t * n_pg)))


def _gla_kernel(q_ref, k_ref, v_ref, la_ref, gb_ref, gn_ref, o_ref, sfin_ref, st_sc, qk_sc, b_sc,
                la_sc, attn_sc, *, chunk, sub, n_chunk, n_seq, heads, dk, dv):
    step = pl.program_id(1)

    @pl.when(step == 0)
    def _():
        st_sc[...] = jnp.zeros(st_sc.shape, F32)

    r_i = lax.broadcasted_iota(jnp.int32, (chunk, chunk), 0)
    c_i = lax.broadcasted_iota(jnp.int32, (chunk, chunk), 1)
    tri = r_i >= c_i
    tri_b = tri.astype(BF16)
    row_k = lax.broadcasted_iota(jnp.int32, (chunk, dk), 0)
    q_scale = dk ** -0.5

    def intra_fast(q, k, b):
        rows = []
        for i in range(chunk // sub):
            lo, hi = i * sub, (i + 1) * sub
            ref = b[lo - 1:lo, :] if i > 0 else jnp.zeros((1, dk), F32)
            qt = (q[lo:hi] * jnp.exp(b[lo:hi] - ref)).astype(BF16)
            kt = (k * jnp.exp(jnp.where(row_k < hi, ref - b, -jnp.inf))).astype(BF16)
            rows.append(_dot_nt(qt, kt))
        return jnp.where(tri, jnp.concatenate(rows, axis=0), 0.0)

    def intra_exact(q, k, b, la):
        qk_sc[...] = q
        b_sc[...] = b
        la_sc[...] = la
        col = lax.broadcasted_iota(jnp.int32, (1, chunk), 1)

        def row(t, carry):
            q_t = qk_sc[pl.ds(t, 1), :]
            la_t = la_sc[pl.ds(t, 1), :]
            before = b_sc[pl.ds(t, 1), :] - la_t
            kt = (k * jnp.exp(jnp.where(row_k < t, before - b, -jnp.inf))).astype(BF16)
            a_row = _dot_nt((q_t * jnp.exp(la_t)).astype(BF16), kt)
            k_t = jnp.sum(jnp.where(row_k == t, k, 0.0), axis=0, keepdims=True)
            diag = jnp.sum(q_t * k_t, axis=1, keepdims=True)
            attn_sc[pl.ds(t, 1), :] = jnp.where(col == t, diag, a_row)
            return carry

        lax.fori_loop(0, chunk, row, 0)
        return attn_sc[...]

    def one_chunk(ci, carry, exact):
        off = pl.multiple_of(ci * chunk, chunk)
        rows_ds = pl.ds(off, chunk)
        for s in range(n_seq):
            la = la_ref[s, rows_ds, :]
            la_hi = la.astype(BF16)
            la_mid = (la - la_hi.astype(F32)).astype(BF16)
            la_lo = (la - la_hi.astype(F32) - la_mid.astype(F32)).astype(BF16)
            b_all = _dot(tri_b, la_hi) + _dot(tri_b, la_mid) + _dot(tri_b, la_lo)
            for h in range(heads):
                kc = slice(h * dk, (h + 1) * dk)
                vc = slice(h * dv, (h + 1) * dv)
                q = q_ref[s, rows_ds, kc] * q_scale
                k = k_ref[s, rows_ds, kc]
                v = v_ref[s, rows_ds, vc]
                b = b_all[:, kc]
                st = st_sc[s, h]
                o = _dot_nt((q * jnp.exp(b)).astype(BF16), st.astype(BF16))
                attn = intra_exact(q, k, b, la[:, kc]) if exact else intra_fast(q, k, b)
                o = o + _dot(attn.astype(BF16), v)
                b_last = b[chunk - 1:chunk, :]
                k_dec = (k * jnp.exp(b_last - b)).astype(BF16)
                st_sc[s, h] = st * jnp.exp(b_last) + _dot_tn(v, k_dec)
                gb = gb_ref[s, rows_ds, vc].astype(F32)
                o_ref[s, rows_ds, vc] = (_rms(o) * gn_ref[...] * (gb * jax.nn.sigmoid(gb))).astype(BF16)
        return carry

    safe = jnp.min(la_ref[...]) * sub > GLA_FAST_MIN_LOG_DECAY

    @pl.when(safe)
    def _():
        lax.fori_loop(0, n_chunk, functools.partial(one_chunk, exact=False), 0)

    @pl.when(jnp.logical_not(safe))
    def _():
        lax.fori_loop(0, n_chunk, functools.partial(one_chunk, exact=True), 0)

    @pl.when(step == pl.num_programs(1) - 1)
    def _():
        sfin_ref[...] = st_sc[...]


def _gla_prompt(qb, kb, vb, la, gb, gla_norm_g, b0, n, *, heads, dk, dv):
    _, t, _ = qb.shape
    rows = min(GLA_STEP_ROWS, t)
    chunk = min(GLA_CHUNK, rows)
    n_seq = GLA_SEQS if n % GLA_SEQS == 0 and b0 % GLA_SEQS == 0 else 1
    assert t % rows == 0 and rows % chunk == 0 and chunk % GLA_SUB == 0
    kspec = pl.BlockSpec((n_seq, rows, heads * dk), lambda b, s: (b + b0 // n_seq, s, 0))
    vspec = pl.BlockSpec((n_seq, rows, heads * dv), lambda b, s: (b + b0 // n_seq, s, 0))
    out_spec = pl.BlockSpec((n_seq, rows, heads * dv), lambda b, s: (b, s, 0))
    return pl.pallas_call(
        functools.partial(_gla_kernel, chunk=chunk, sub=GLA_SUB, n_chunk=rows // chunk,
                          n_seq=n_seq, heads=heads, dk=dk, dv=dv),
        grid=(n // n_seq, t // rows),
        in_specs=[kspec, kspec, vspec, kspec, vspec, pl.BlockSpec((1, dv), lambda b, s: (0, 0))],
        out_specs=[out_spec, pl.BlockSpec((n_seq, heads, dv, dk), lambda b, s: (b, 0, 0, 0))],
        out_shape=[jax.ShapeDtypeStruct((n, t, heads * dv), BF16),
                   jax.ShapeDtypeStruct((n, heads, dv, dk), F32)],
        scratch_shapes=[pltpu.VMEM((n_seq, heads, dv, dk), F32), pltpu.VMEM((chunk, dk), F32),
                        pltpu.VMEM((chunk, dk), F32), pltpu.VMEM((chunk, dk), F32),
                        pltpu.VMEM((chunk, chunk), F32)],
        compiler_params=_cparams("parallel", "arbitrary"),
        name="gla_prompt",
    )(qb, kb, vb, la, gb, gla_norm_g)


def _gla_sample_kernel(qt_ref, kt_ref, lat_ref, v_ref, gb_ref, gn_ref, s0_ref, o_ref, sfin_ref,
                       *, heads, dk, dv):
    for h in range(heads):
        vc = slice(h * dv, (h + 1) * dv)
        s = s0_ref[0, 0, h]
        qt = qt_ref[0, h] * dk ** -0.5
        kt = kt_ref[0, h]
        at = jnp.exp(lat_ref[0, h])
        v = v_ref[0, :, vc].astype(F32)
        outs = []
        for t in range(v.shape[0]):
            s = at[:, t:t + 1] * s + kt[:, t:t + 1] * v[t:t + 1, :]
            outs.append(jnp.sum(qt[:, t:t + 1] * s, axis=0, keepdims=True))
        o = jnp.concatenate(outs, axis=0)
        gb = gb_ref[0, :, vc].astype(F32)
        o_ref[0, :, vc] = (_rms(o) * gn_ref[...] * (gb * jax.nn.sigmoid(gb))).astype(BF16)
        sfin_ref[0, h] = s


def _gla_sample(qt, kt, lat, vb, gb, gla_norm_g, state, layer, *, dk, dv):
    n, heads, _, t = qt.shape
    tspec = pl.BlockSpec((1, heads, dk, t), lambda b: (b, 0, 0, 0))
    vspec = pl.BlockSpec((1, t, heads * dv), lambda b: (b, 0, 0))
    return pl.pallas_call(
        functools.partial(_gla_sample_kernel, heads=heads, dk=dk, dv=dv),
        grid=(n,),
        in_specs=[tspec, tspec, tspec, vspec, vspec, pl.BlockSpec((1, dv), lambda b: (0, 0)),
                  pl.BlockSpec((1, 1, heads, dk, dv), lambda b: (layer, b, 0, 0, 0))],
        out_specs=[vspec, pl.BlockSpec((1, heads, dk, dv), lambda b: (b, 0, 0, 0))],
        out_shape=[jax.ShapeDtypeStruct((n, t, heads * dv), BF16),
                   jax.ShapeDtypeStruct((n, heads, dk, dv), F32)],
        compiler_params=_cparams("parallel"),
        name="gla_sample",
    )(qt, kt, lat, vb, gb, gla_norm_g, state)


def _merge_kernel(x_ref, oa_ref, ob_ref, g1_ref, wa_ref, wb_ref, wmg_ref, wo_ref, g2_ref, wrh_ref,
                  wrl_ref, br_ref, x2_ref, h2_ref, idx_ref, gate_ref, *, d, n_exp):
    x = x_ref[...]
    hb = (_rms(x) * g1_ref[...]).astype(BF16)
    y_a = _dot(oa_ref[...], wa_ref[...])
    y_b = _dot(ob_ref[...], wb_ref[...])
    mixed = (jax.nn.sigmoid(_dot(hb, wmg_ref[:, :d])) * y_a
             + jax.nn.sigmoid(_dot(hb, wmg_ref[:, d:])) * y_b)
    x2 = x + _dot(mixed.astype(BF16), wo_ref[...])
    x2_ref[...] = x2
    h2 = _rms(x2) * g2_ref[...]
    h2_hi = h2.astype(BF16)
    h2_ref[...] = h2_hi
    h2_lo = (h2 - h2_hi.astype(F32)).astype(BF16)
    logits = (_dot(h2_hi, wrh_ref[...]) + _dot(h2_hi, wrl_ref[...]) + _dot(h2_lo, wrh_ref[...])
              + br_ref[...])
    col = lax.broadcasted_iota(jnp.int32, logits.shape, 1)
    _, picks = _top_mask(logits, col, MOE_TOP_K, n_exp, axis=1)
    top_val = jnp.concatenate([p[1] for p in picks], axis=1)
    e = jnp.exp(top_val - top_val[:, :1])
    idx_ref[...] = jnp.concatenate([p[0] for p in picks], axis=1)
    gate_ref[...] = e / jnp.sum(e, axis=1, keepdims=True)


def _merge(x, row0, o_a, o_b, norm1_g, w_a, w_b, w_mg, w_o, norm2_g, w_router, b_router):
    n_tok, d = o_a.shape[0], x.shape[1]
    n_exp = w_router.shape[1]
    w_router_hi = w_router.astype(BF16)
    w_router_lo = (w_router - w_router_hi.astype(F32)).astype(BF16)
    tm = min(PROJ_ROWS, n_tok)
    assert n_tok % tm == 0 and row0 % tm == 0
    full = lambda a: pl.BlockSpec(a.shape, lambda i: (0,) * a.ndim)
    rows = lambda w: pl.BlockSpec((tm, w), lambda i: (i, 0))
    return pl.pallas_call(
        functools.partial(_merge_kernel, d=d, n_exp=n_exp),
        grid=(n_tok // tm,),
        in_specs=[pl.BlockSpec((tm, d), lambda i: (i + row0 // tm, 0)),
                  rows(o_a.shape[1]), rows(o_b.shape[1]), full(norm1_g), full(w_a),
                  full(w_b), full(w_mg), full(w_o), full(norm2_g), full(w_router_hi),
                  full(w_router_lo), full(b_router)],
        out_specs=[rows(d), rows(d), rows(MOE_TOP_K), rows(MOE_TOP_K)],
        out_shape=[jax.ShapeDtypeStruct((n_tok, d), F32), jax.ShapeDtypeStruct((n_tok, d), BF16),
                   jax.ShapeDtypeStruct((n_tok, MOE_TOP_K), jnp.int32),
                   jax.ShapeDtypeStruct((n_tok, MOE_TOP_K), F32)],
        compiler_params=_cparams("parallel"),
        name="merge",
    )(x, o_a, o_b, norm1_g, w_a, w_b, w_mg, w_o, norm2_g, w_router_hi, w_router_lo, b_router)


def _expert_kernel(te_ref, nt_ref, x_ref, wup_ref, bg_ref, bl_ref, wdn_ref, bdn_ref, y_ref,
                   wg_sc, wl_sc, wd_sc, t_sc, *, slab):
    i = pl.program_id(0)

    @pl.when((i == 0) | (te_ref[i] != te_ref[jnp.maximum(i - 1, 0)]))
    def _():
        for c in range(0, wup_ref.shape[2], slab):
            rows = slice(c // 2, (c + slab) // 2)
            for j in range(t_sc.shape[0]):
                lanes = slice(j * LANES, (j + 1) * LANES)
                t_sc[j] = wup_ref[0, lanes, c:c + slab].T
                wg_sc[rows, lanes] = t_sc[j, pl.ds(0, slab // 2, stride=2), :].astype(BF16)
                wl_sc[rows, lanes] = t_sc[j, pl.ds(1, slab // 2, stride=2), :].astype(BF16)
        wd_sc[...] = wdn_ref[0].astype(BF16)

    @pl.when(i < nt_ref[0])
    def _():
        x = x_ref[...]
        x_glu = jnp.minimum(_dot_nt(x, wg_sc[...]) + bg_ref[0], SWIGLU_LIMIT)
        x_lin = jnp.clip(_dot_nt(x, wl_sc[...]) + bl_ref[0], -SWIGLU_LIMIT, SWIGLU_LIMIT)
        a = x_glu * jax.nn.sigmoid(SWIGLU_ALPHA * x_glu) * (x_lin + 1.0)
        y_ref[...] = (_dot(a.astype(BF16), wd_sc[...]) + bdn_ref[0]).astype(y_ref.dtype)

    @pl.when(i >= nt_ref[0])
    def _():
        y_ref[...] = jnp.zeros(y_ref.shape, y_ref.dtype)


def _experts(buf, tile_e, n_used, w_up, b_glu, b_lin, w_dn, b_dn, *, tile_rows):
    r, d = buf.shape
    d_ff = w_dn.shape[1]
    slab = min(2 * LANES, 2 * d_ff)
    by_expert = lambda *blk: pl.BlockSpec((1,) + blk, lambda i, te, nt: (te[i], 0, 0))
    return pl.pallas_call(
        functools.partial(_expert_kernel, slab=slab),
        grid_spec=pltpu.PrefetchScalarGridSpec(
            num_scalar_prefetch=2, grid=(r // tile_rows,),
            in_specs=[pl.BlockSpec((tile_rows, d), lambda i, te, nt: (i, 0)),
                      by_expert(d, 2 * d_ff), by_expert(1, d_ff), by_expert(1, d_ff),
                      by_expert(d_ff, d), by_expert(1, d)],
            out_specs=pl.BlockSpec((tile_rows, d), lambda i, te, nt: (i, 0)),
            scratch_shapes=[pltpu.VMEM((d_ff, d), BF16), pltpu.VMEM((d_ff, d), BF16),
                            pltpu.VMEM((d_ff, d), BF16), pltpu.VMEM((d // LANES, slab, LANES), F32)]),
        out_shape=jax.ShapeDtypeStruct((r, d), BF16),
        compiler_params=_cparams("arbitrary"),
        name="experts",
    )(tile_e, n_used, buf, w_up, b_glu, b_lin, w_dn, b_dn)


def _rank_kernel(idx_ref, rank_ref, cnt_ref, seen_sc):
    @pl.when(pl.program_id(0) == 0)
    def _():
        seen_sc[...] = jnp.zeros(seen_sc.shape, F32)

    idx = idx_ref[...]
    tr, top_k = idx.shape
    lane = lax.broadcasted_iota(jnp.int32, (tr, LANES), 1)
    hits = [idx[:, k:k + 1] == lane for k in range(top_k)]
    routed = hits[0].astype(F32)
    for h in hits[1:]:
        routed = routed + h.astype(F32)
    earlier = (lax.broadcasted_iota(jnp.int32, (tr, tr), 1)
               < lax.broadcasted_iota(jnp.int32, (tr, tr), 0)).astype(BF16)
    before = _dot(earlier, routed.astype(BF16)) + seen_sc[...]
    rank_ref[...] = jnp.concatenate(
        [jnp.sum(jnp.where(h, before, 0.0), axis=1, keepdims=True) for h in hits],
        axis=1).astype(jnp.int32)
    seen_sc[...] = seen_sc[...] + jnp.sum(routed, axis=0, keepdims=True)
    cnt_ref[...] = seen_sc[...]


def _rank(top_idx):
    n_tok, top_k = top_idx.shape
    tr = min(ROUTE_ROWS, n_tok)
    assert n_tok % tr == 0
    return pl.pallas_call(
        _rank_kernel,
        grid=(n_tok // tr,),
        in_specs=[pl.BlockSpec((tr, top_k), lambda i: (i, 0))],
        out_specs=[pl.BlockSpec((tr, top_k), lambda i: (i, 0)),
                   pl.BlockSpec((1, LANES), lambda i: (0, 0))],
        out_shape=[jax.ShapeDtypeStruct((n_tok, top_k), jnp.int32),
                   jax.ShapeDtypeStruct((1, LANES), F32)],
        scratch_shapes=[pltpu.VMEM((1, LANES), F32)],
        compiler_params=_cparams("arbitrary"),
        name="moe_rank",
    )(top_idx)


def _combine_kernel(x_ref, y_ref, g_ref, o_ref):
    acc = x_ref[...]
    g = g_ref[...]
    for k in range(y_ref.shape[0]):
        acc = acc + g[:, k:k + 1] * y_ref[k].astype(F32)
    o_ref[...] = acc


def _combine(x2, y_kt, gate):
    top_k, n_tok, d = y_kt.shape
    tm = min(ROUTE_ROWS, n_tok)
    assert n_tok % tm == 0
    return pl.pallas_call(
        _combine_kernel,
        grid=(n_tok // tm,),
        in_specs=[pl.BlockSpec((tm, d), lambda i: (i, 0)),
                  pl.BlockSpec((top_k, tm, d), lambda i: (0, i, 0)),
                  pl.BlockSpec((tm, top_k), lambda i: (i, 0))],
        out_specs=pl.BlockSpec((tm, d), lambda i: (i, 0)),
        out_shape=jax.ShapeDtypeStruct((n_tok, d), F32),
        compiler_params=_cparams("parallel"),
        name="moe_combine",
    )(x2, y_kt, gate)


def _moe(x2, h2, top_idx, gate, expert_weights, tile_rows):
    n_tok, d = h2.shape
    n_exp = expert_weights[0].shape[0]
    assert n_exp <= LANES
    n_assign = n_tok * MOE_TOP_K
    rank, cnt = _rank(top_idx)
    counts = cnt[0, :n_exp].astype(jnp.int32)
    padded = (counts + tile_rows - 1) // tile_rows * tile_rows
    pend = jnp.cumsum(padded)
    experts = jnp.arange(n_exp, dtype=jnp.int32)
    first_row = jnp.sum(jnp.where(top_idx[:, :, None] == experts, pend - padded, 0), axis=-1)
    dest = (first_row + rank).reshape(-1)
    n_tiles = -(-n_assign // tile_rows) + n_exp
    tile_row0 = jnp.arange(n_tiles, dtype=jnp.int32) * tile_rows
    tile_e = jnp.minimum(jnp.sum((pend[None, :] <= tile_row0[:, None]).astype(jnp.int32), axis=1),
                         n_exp - 1)
    n_used = (pend[-1:] // tile_rows).astype(jnp.int32)
    src = jnp.zeros((n_tiles * tile_rows,), jnp.int32).at[dest].set(
        jnp.arange(n_assign, dtype=jnp.int32) // MOE_TOP_K)
    y = _experts(h2[src], tile_e, n_used, *expert_weights, tile_rows=tile_rows)
    y_kt = y[dest.reshape(n_tok, MOE_TOP_K).T]
    return _combine(x2, y_kt, gate)


def _rope_tables_t(pos, hd):
    half = hd // 2
    inv = ROPE_THETA ** (-jnp.arange(half, dtype=F32) / half)
    ang = inv[:, None] * pos.astype(F32)[None, :]
    cos, sin = jnp.cos(ang), jnp.sin(ang)
    return jnp.concatenate([cos, cos], axis=0), jnp.concatenate([-sin, sin], axis=0)


def kernel(x_prompt, x_sample, cache_k, cache_v, state_gla, page_table, norm1_g, w_in, q_norm_g,
           k_norm_g, w_gate_up, b_gate_up, gla_norm_g, w_branch_a, w_branch_b, w_merge_gate, w_out,
           norm2_g, w_router, b_router, w_up, b_up, w_down, b_down):
    depth = norm1_g.shape[0]
    assert depth == 1, "sample/prompt streams are only chained through one layer here"
    layer = 0
    n_p, t_p, d = x_prompt.shape
    n_s, t_s, _ = x_sample.shape
    heads, page, hd = cache_k.shape[2:]
    b_heads, dk, dv = state_gla.shape[2:]
    bk, bv = b_heads * dk, b_heads * dv
    aqk = heads * hd
    rank = w_gate_up.shape[1]
    past_len = page_table.shape[1] * page
    ppb = MOBA_BLOCK // page
    n_full = past_len // MOBA_BLOCK
    assert past_len % MOBA_BLOCK == 0 and n_full >= MOBA_TOPK and rank <= LANES
    assert w_in.shape[2] == 3 * aqk + 2 * bk + 2 * bv + rank

    w_qkv_t = w_in[layer][:, :3 * aqk].T.astype(BF16)
    w_rest = jnp.pad(w_in[layer][:, 3 * aqk:], ((0, 0), (0, LANES - rank))).astype(BF16)
    w_gu_p = jnp.pad(w_gate_up[layer], ((0, LANES - rank), (0, 0))).astype(BF16)
    g1 = norm1_g[layer][None, :]
    g2 = norm2_g[layer][None, :]
    qg = q_norm_g[layer][:, None]
    kg = k_norm_g[layer][:, None]
    bgu = b_gate_up[layer][None, :]
    gn = gla_norm_g[layer][None, :]
    w_a = w_branch_a[layer].astype(BF16)
    w_b = w_branch_b[layer].astype(BF16)
    w_mg = w_merge_gate[layer].astype(BF16)
    w_o = w_out[layer].astype(BF16)
    w_r = w_router[layer]
    b_r = b_router[layer][None, :]
    expert_weights = (w_up[layer], b_up[layer][:, None, 0::2], b_up[layer][:, None, 1::2],
                      w_down[layer], b_down[layer][:, None, :])
    proj = functools.partial(_project, norm1_g=g1, w_qkv_t=w_qkv_t, w_rest=w_rest, q_norm_g=qg,
                             k_norm_g=kg, w_gu_p=w_gu_p, b_gu=bgu, heads=heads, hd=hd, bk=bk, bv=bv)
    merge = functools.partial(_merge, norm1_g=g1, w_a=w_a, w_b=w_b, w_mg=w_mg, w_o=w_o,
                              norm2_g=g2, w_router=w_r, b_router=b_r)

    cos_p, sin_p = _rope_tables_t(jnp.arange(t_p), hd)
    qt, kt, vt, qb, kb, vb, la, gb = proj(x_prompt, cos_p, sin_p)
    n_part = n_p // PROMPT_SPLITS if n_p % PROMPT_SPLITS == 0 else n_p
    x_rows = x_prompt.reshape(n_p * t_p, d)
    y_parts, st_parts = [], []
    for b0 in range(0, n_p, n_part):
        o_a = _moba_prompt(qt, kt, vt, b0, n_part)
        o_b, st = _gla_prompt(qb, kb, vb, la, gb, gn, b0, n_part, heads=b_heads, dk=dk, dv=dv)
        x2, h2, idx, gate = merge(x_rows, b0 * t_p, o_a.reshape(n_part * t_p, -1),
                                  o_b.reshape(n_part * t_p, -1))
        y_parts.append(_moe(x2, h2, idx, gate, expert_weights, MOE_ROWS))
        st_parts.append(st)
    y_prompt = jnp.concatenate(y_parts).reshape(n_p, t_p, d)
    s_prompt = jnp.swapaxes(jnp.concatenate(st_parts), 2, 3)

    n_tok_s = n_s * t_s
    cos_s, sin_s = _rope_tables_t(past_len + jnp.arange(t_s), hd)
    qt_s, kt_s, vt_s, qb_s, kb_s, vb_s, la_s, gb_s = proj(
        x_sample.reshape(1, n_tok_s, d), jnp.tile(cos_s, (1, n_s)), jnp.tile(sin_s, (1, n_s)))
    per_seq = lambda a: a.reshape(heads, hd, n_s, t_s).transpose(2, 0, 3, 1)
    qa_s, ka_s, va_s = per_seq(qt_s), per_seq(kt_s), per_seq(vt_s)
    cache_kt = jnp.swapaxes(cache_k, 3, 4)
    cache_vt = jnp.swapaxes(cache_v, 3, 4)
    kmean = _sample_kmean(cache_kt, layer, page_table, n_full, ppb).transpose(0, 2, 1, 3)
    blk_idx = _sample_pick(qa_s, kmean, MOBA_TOPK)
    lpages = blk_idx[..., None] * ppb + jnp.arange(ppb, dtype=jnp.int32)
    phys = jnp.take_along_axis(page_table[:, None, None, :],
                               lpages.reshape(n_s, heads, t_s, MOBA_TOPK * ppb), axis=-1)
    o_a_s = _sample_attn(qa_s, ka_s, va_s, cache_kt, cache_vt, layer, phys)
    o_a_s = o_a_s.transpose(0, 2, 1, 3).reshape(n_tok_s, aqk).astype(BF16)
    tr = lambda a: a.reshape(n_s, t_s, b_heads, dk).transpose(0, 2, 3, 1)
    o_b_s, s_sample = _gla_sample(tr(qb_s), tr(kb_s), tr(la_s), vb_s.reshape(n_s, t_s, bv),
                                  gb_s.reshape(n_s, t_s, bv), gn, state_gla, layer, dk=dk, dv=dv)
    x2_s, h2_s, idx_s, gate_s = merge(x_sample.reshape(n_tok_s, d), 0, o_a_s,
                                      o_b_s.reshape(n_tok_s, bv))
    y_sample = _moe(x2_s, h2_s, idx_s, gate_s, expert_weights,
                    min(MOE_ROWS, n_tok_s)).reshape(n_s, t_s, d)
    k_prompt = jnp.swapaxes(kt, 2, 3)
    v_prompt = jnp.swapaxes(vt, 2, 3)
    return (y_prompt, y_sample, k_prompt[None], v_prompt[None], ka_s[None], va_s[None],
            s_prompt[None], s_sample[None])
```

```python
import functools

import jax
import jax.numpy as jnp
from jax import lax
from jax.experimental import pallas as pl
from jax.experimental.pallas import tpu as pltpu

F32 = jnp.float32
BF16 = jnp.bfloat16
HIGHEST = lax.Precision.HIGHEST

MOBA_BLOCK = 256
MOBA_TOPK = 3
ROPE_THETA = 10000.0
GLA_GATE_NORM = 16.0
NORM_EPS = 1e-6
MOE_TOP_K = 4
SWIGLU_ALPHA = 1.702
SWIGLU_LIMIT = 7.0

LANES = 128
PAD_ROWS = 16
VMEM_LIMIT_BYTES = 56 * 1024 * 1024

MASK_VALUE = -1e30
LOG2_E = 1.4426950408889634
PROJ_ROWS = 512
GLA_CHUNK = 64
GLA_SUB = 16
GLA_STEP_ROWS = 512
GLA_SEQS = 2
GLA_FAST_MIN_LOG_DECAY = -80.0
MOE_ROWS = 512
ROUTE_ROWS = 1024
PROMPT_SPLITS = 1
MOBA_GROUP = 4
SAMPLE_KMEAN_PAGES = 16


def _cparams(*sem):
    return pltpu.CompilerParams(dimension_semantics=sem, vmem_limit_bytes=VMEM_LIMIT_BYTES)


def _rms(x):
    return x * lax.rsqrt(jnp.mean(x * x, axis=-1, keepdims=True) + NORM_EPS)


def _dot(a, b):
    return jnp.dot(a, b, preferred_element_type=F32)


def _dot_nt(a, b, precision=None):
    return lax.dot_general(a, b, (((1,), (1,)), ((), ())), precision=precision,
                           preferred_element_type=F32)


def _dot_tn(a, b):
    return lax.dot_general(a, b, (((0,), (0,)), ((), ())), preferred_element_type=F32)


def _top_mask(g, pos, n_pick, n_pos, axis):
    sel = jnp.zeros(g.shape, F32)
    picks = []
    for _ in range(n_pick):
        mx = jnp.max(g, axis=axis, keepdims=True)
        cand = jnp.where((g == mx) & (mx > -jnp.inf), pos, n_pos)
        idx = jnp.min(cand, axis=axis, keepdims=True)
        pick = pos == idx
        sel = jnp.where(pick, 1.0, sel)
        g = jnp.where(pick, -jnp.inf, g)
        picks.append((idx, mx))
    return sel, picks


def _proj_kernel(x_ref, g1_ref, wqkv_ref, wrest_ref, qg_ref, kg_ref, wgu_ref, bgu_ref, cos_ref,
                 sin_ref, qt_ref, kt_ref, vt_ref, qb_ref, kb_ref, vb_ref, la_ref, gb_ref,
                 *, heads, hd, bk, bv):
    x = x_ref[0]
    hb = (_rms(x) * g1_ref[...]).astype(BF16)
    cos = cos_ref[...]
    sin = sin_ref[...]
    aqk = heads * hd

    def mm_t(lo, hi):
        return _dot_nt(wqkv_ref[lo:hi, :], hb)

    def norm_rope_t(z, g_ref, out_ref):
        for h in range(heads):
            y = z[h * hd:(h + 1) * hd, :]
            y = y * lax.rsqrt(jnp.mean(y * y, axis=0, keepdims=True) + NORM_EPS) * g_ref[...]
            rot = jnp.concatenate([y[hd // 2:], y[:hd // 2]], axis=0)
            out_ref[0, h] = y * cos + rot * sin

    norm_rope_t(mm_t(0, aqk), qg_ref, qt_ref)
    norm_rope_t(mm_t(aqk, 2 * aqk), kg_ref, kt_ref)
    zv = mm_t(2 * aqk, 3 * aqk)
    for h in range(heads):
        vt_ref[0, h] = zv[h * hd:(h + 1) * hd, :]

    def mm(lo, hi):
        return _dot(hb, wrest_ref[:, lo:hi])

    qb_ref[0] = mm(0, bk)
    kb_ref[0] = mm(bk, 2 * bk)
    vb_ref[0] = mm(2 * bk, 2 * bk + bv).astype(BF16)
    gb_ref[0] = mm(2 * bk + bv, 2 * bk + 2 * bv).astype(BF16)
    lr = mm(2 * bk + 2 * bv, 2 * bk + 2 * bv + LANES)
    u = _dot(lr.astype(BF16), wgu_ref[...]) + bgu_ref[...]
    log_sig = jnp.minimum(u, 0.0) - jnp.log(1.0 + jnp.exp(-jnp.abs(u)))
    la_ref[0] = log_sig / GLA_GATE_NORM


def _project(x, cos_t, sin_t, norm1_g, w_qkv_t, w_rest, q_norm_g, k_norm_g, w_gu_p, b_gu,
             *, heads, hd, bk, bv):
    n, t, d = x.shape
    tm = min(PROJ_ROWS, t)
    assert t % tm == 0
    full = lambda a: pl.BlockSpec(a.shape, lambda b, i: (0,) * a.ndim)
    head_out = pl.BlockSpec((1, heads, hd, tm), lambda b, i: (b, 0, 0, i))
    row_out = lambda w: pl.BlockSpec((1, tm, w), lambda b, i: (b, i, 0))
    table = pl.BlockSpec((hd, tm), lambda b, i: (0, i))
    head_shape = jax.ShapeDtypeStruct((n, heads, hd, t), F32)
    return pl.pallas_call(
        functools.partial(_proj_kernel, heads=heads, hd=hd, bk=bk, bv=bv),
        grid=(n, t // tm),
        in_specs=[pl.BlockSpec((1, tm, d), lambda b, i: (b, i, 0)),
                  full(norm1_g), full(w_qkv_t), full(w_rest), full(q_norm_g), full(k_norm_g),
                  full(w_gu_p), full(b_gu), table, table],
        out_specs=[head_out, head_out, head_out, row_out(bk), row_out(bk), row_out(bv),
                   row_out(bk), row_out(bv)],
        out_shape=[head_shape, head_shape, head_shape,
                   jax.ShapeDtypeStruct((n, t, bk), F32),
                   jax.ShapeDtypeStruct((n, t, bk), F32),
                   jax.ShapeDtypeStruct((n, t, bv), BF16),
                   jax.ShapeDtypeStruct((n, t, bk), F32),
                   jax.ShapeDtypeStruct((n, t, bv), BF16)],
        compiler_params=_cparams("parallel", "parallel"),
        name="proj",
    )(x, norm1_g, w_qkv_t, w_rest, q_norm_g, k_norm_g, w_gu_p, b_gu, cos_t, sin_t)


def _moba_kernel(qt_ref, kt_ref, vt_ref, o_ref, k_sc, vt_sc, kmean_sc, qs_sc, bias_sc, m_sc, acc_sc,
                 sa_sc, sb_sc, ma_sc, mb_sc, *, nb, blk, hd, hp, grp):
    c = pl.program_id(2)
    nq = hp * blk
    vr = hd + PAD_ROWS
    n_groups = nb // grp
    cols = [slice(i * blk, (i + 1) * blk) for i in range(hp)]

    @pl.when(c == 0)
    def _():
        row = lax.broadcasted_iota(jnp.int32, (LANES - hd, blk), 0)
        ones_row = (lax.broadcasted_iota(jnp.int32, (PAD_ROWS, blk), 0) == 0).astype(BF16)

        def stage_kv(j, carry):
            off = pl.multiple_of(j * blk, blk)
            place = (row == j % grp).astype(F32)
            for i in range(hp):
                k_nat = jnp.concatenate([kt_ref[0, i, :, pl.ds(off, blk)], place], axis=0).T
                k_sc[i, pl.ds(off, blk), :] = k_nat.astype(BF16)
                kmean_sc[i, pl.ds(j, 1), :] = jnp.mean(k_nat, axis=0, keepdims=True)
                vt_sc[i * vr:i * vr + hd, pl.ds(off, blk)] = vt_ref[0, i, :, pl.ds(off, blk)].astype(BF16)
                vt_sc[i * vr + hd:(i + 1) * vr, pl.ds(off, blk)] = ones_row
            return carry

        lax.fori_loop(0, nb, stage_kv, 0)

    zero_rows = jnp.zeros((LANES - hd, blk), F32)
    q_pad = [jnp.concatenate([qt_ref[0, i], zero_rows], axis=0) for i in range(hp)]
    gate = jnp.concatenate(
        [jnp.dot(kmean_sc[i], q_pad[i], precision=HIGHEST, preferred_element_type=F32)
         for i in range(hp)], axis=1)
    blk_row = lax.broadcasted_iota(jnp.int32, (nb, nq), 0)
    sel, _ = _top_mask(jnp.where(blk_row < c, gate, -jnp.inf), blk_row, min(MOBA_TOPK, nb - 1),
                       nb, axis=0)
    bias = (sel - 1.0) * -MASK_VALUE
    pad = jnp.zeros((PAD_ROWS - grp, blk), F32)
    for g in range(n_groups):
        for i in range(hp):
            bias_sc[g, i] = jnp.concatenate([bias[g * grp:(g + 1) * grp, cols[i]], pad],
                                            axis=0).astype(BF16)
    for i in range(hp):
        qs_sc[i] = (q_pad[i] * (hd ** -0.5 * LOG2_E)).astype(BF16)

    def scores(off, n_keys):
        return jnp.concatenate([_dot(k_sc[i, pl.ds(off, n_keys), :], qs_sc[i]) for i in range(hp)],
                               axis=1)

    def update(s, m_blk, off, n_keys, first):
        m_new = m_blk if first else jnp.maximum(m_sc[...], m_blk)
        pb = jnp.exp2(s - m_new).astype(BF16)
        if not first:
            alpha = jnp.exp2(m_sc[...] - m_new)
        m_sc[...] = m_new
        for i in range(hp):
            pv = _dot(vt_sc[i * vr:(i + 1) * vr, pl.ds(off, n_keys)], pb[:, cols[i]])
            acc_sc[:, cols[i]] = pv if first else alpha[:, cols[i]] * acc_sc[:, cols[i]] + pv

    own_off = pl.multiple_of(c * blk, blk)
    causal = (lax.broadcasted_iota(jnp.int32, (blk, nq), 0)
              <= lax.broadcasted_iota(jnp.int32, (blk, nq), 1) % blk)
    s_own = jnp.where(causal, scores(own_off, blk), MASK_VALUE)

    n_grp = (c + grp - 1) // grp

    def stage(g, s_buf, mx_buf):
        g = jnp.minimum(g, n_groups - 1)
        for i in range(hp):
            qs_sc[i, hd:hd + PAD_ROWS, :] = bias_sc[g, i]
        s = scores(pl.multiple_of(g * (grp * blk), grp * blk), grp * blk)
        s_buf[...] = s
        mx_buf[...] = jnp.max(s, axis=0, keepdims=True)

    def consume(g, s_buf, mx_buf):
        update(s_buf[...], mx_buf[...], pl.multiple_of(g * (grp * blk), grp * blk), grp * blk, False)

    stage(0, sa_sc, ma_sc)
    update(s_own, jnp.max(s_own, axis=0, keepdims=True), own_off, blk, True)

    def body(u, carry):
        stage(2 * u + 1, sb_sc, mb_sc)
        consume(2 * u, sa_sc, ma_sc)
        stage(2 * u + 2, sa_sc, ma_sc)
        consume(2 * u + 1, sb_sc, mb_sc)
        return carry

    lax.fori_loop(0, n_grp // 2, body, 0)

    @pl.when(n_grp % 2 == 1)
    def _():
        consume(n_grp - 1, sa_sc, ma_sc)

    out_t = acc_sc[:hd, :] / acc_sc[hd:hd + 1, :]
    out_t = jnp.concatenate([out_t[:, cols[i]] for i in range(hp)], axis=0)
    o_ref[0] = out_t.T.astype(BF16)


def _moba_prompt(qt, kt, vt, b0, n):
    _, heads, hd, t = qt.shape
    blk = MOBA_BLOCK
    assert t % blk == 0 and LANES % hd == 0
    hp = LANES // hd
    assert heads % hp == 0
    nb = t // blk
    grp = MOBA_GROUP if nb % MOBA_GROUP == 0 else 1
    nq = hp * blk
    kv_spec = pl.BlockSpec((1, hp, hd, t), lambda b, g, c: (b + b0, g, 0, 0))
    return pl.pallas_call(
        functools.partial(_moba_kernel, nb=nb, blk=blk, hd=hd, hp=hp, grp=grp),
        grid=(n, heads // hp, nb),
        in_specs=[pl.BlockSpec((1, hp, hd, blk), lambda b, g, c: (b + b0, g, 0, c)), kv_spec, kv_spec],
        out_specs=pl.BlockSpec((1, blk, hp * hd), lambda b, g, c: (b, c, g)),
        out_shape=jax.ShapeDtypeStruct((n, t, heads * hd), BF16),
        scratch_shapes=[pltpu.VMEM((hp, t, LANES), BF16),
                        pltpu.VMEM((hp * (hd + PAD_ROWS), t), BF16),
                        pltpu.VMEM((hp, nb, LANES), F32),
                        pltpu.VMEM((hp, LANES, blk), BF16),
                        pltpu.VMEM((nb // grp, hp, PAD_ROWS, blk), BF16),
                        pltpu.VMEM((1, nq), F32),
                        pltpu.VMEM((hd + PAD_ROWS, nq), F32),
                        pltpu.VMEM((grp * blk, nq), F32), pltpu.VMEM((grp * blk, nq), F32),
                        pltpu.VMEM((1, nq), F32), pltpu.VMEM((1, nq), F32)],
        compiler_params=_cparams("parallel", "parallel", "arbitrary"),
        name="moba_prompt",
    )(qt, kt, vt)


def _kmean_pages_kernel(pt_ref, *refs, n_pg, ppb, rows):
    del pt_ref
    out_ref = refs[n_pg]
    for b in range(n_pg // ppb):
        acc = refs[b * ppb][0, 0]
        for j in range(1, ppb):
            acc = acc + refs[b * ppb + j][0, 0]
        out_ref[0, b] = jnp.sum(acc, axis=-1) / rows


def _sample_kmean(cache_kt, layer, page_table, n_full, ppb):
    _, _, heads, hd, page = cache_kt.shape
    n, n_pages = page_table.shape
    n_pg = SAMPLE_KMEAN_PAGES
    assert (n_full * ppb) % n_pg == 0 and n_pg % ppb == 0

    def page_map(b, g, pt, j):
        return (layer, pt[b * n_pages + g * n_pg + j], 0, 0, 0)

    specs = [pl.BlockSpec((1, 1, heads, hd, page), functools.partial(page_map, j=j))
             for j in range(n_pg)]
    return pl.pallas_call(
        functools.partial(_kmean_pages_kernel, n_pg=n_pg, ppb=ppb, rows=float(ppb * page)),
        grid_spec=pltpu.PrefetchScalarGridSpec(
            num_scalar_prefetch=1, grid=(n, n_full * ppb // n_pg), in_specs=specs,
            out_specs=pl.BlockSpec((1, n_pg // ppb, heads, hd), lambda b, g, pt: (b, g, 0, 0))),
        out_shape=jax.ShapeDtypeStruct((n, n_full, heads, hd), F32),
        compiler_params=_cparams("parallel", "arbitrary"),
        name="sample_kmean",
    )(page_table.reshape(-1), *([cache_kt] * n_pg))


def _sample_pick_kernel(q_ref, km_ref, idx_ref, *, heads, n_full, n_sel):
    col = lax.broadcasted_iota(jnp.int32, (q_ref.shape[2], n_full), 1)
    for h in range(heads):
        gate = _dot_nt(q_ref[0, h], km_ref[0, h], precision=HIGHEST)
        _, picks = _top_mask(gate, col, n_sel, n_full, axis=1)
        idx_ref[0, h] = jnp.concatenate([p[0] for p in picks], axis=1)


def _sample_pick(q, kmean, n_sel):
    n, heads, t, hd = q.shape
    n_full = kmean.shape[2]
    return pl.pallas_call(
        functools.partial(_sample_pick_kernel, heads=heads, n_full=n_full, n_sel=n_sel),
        grid=(n,),
        in_specs=[pl.BlockSpec((1, heads, t, hd), lambda b: (b, 0, 0, 0)),
                  pl.BlockSpec((1, heads, n_full, hd), lambda b: (b, 0, 0, 0))],
        out_specs=pl.BlockSpec((1, heads, t, n_sel), lambda b: (b, 0, 0, 0)),
        out_shape=jax.ShapeDtypeStruct((n, heads, t, n_sel), jnp.int32),
        compiler_params=_cparams("parallel"),
        name="sample_pick",
    )(q, kmean)


def _sample_attn_kernel(ph_ref, q_ref, kn_ref, vn_ref, *refs, n_pg):
    del ph_ref
    tq, hd = q_ref.shape[2], q_ref.shape[3]
    k_refs, v_refs, o_ref = refs[:tq * n_pg], refs[tq * n_pg:2 * tq * n_pg], refs[2 * tq * n_pg]
    qs = (q_ref[0, 0] * hd ** -0.5).astype(BF16)
    own_ok = (lax.broadcasted_iota(jnp.int32, (tq, tq), 1)
              <= lax.broadcasted_iota(jnp.int32, (tq, tq), 0))
    s_own = jnp.where(own_ok, _dot_nt(qs, kn_ref[0, 0].astype(BF16)), MASK_VALUE)
    vn = vn_ref[0, 0].astype(BF16)
    for t in range(tq):
        pages = slice(t * n_pg, (t + 1) * n_pg)
        kt = jnp.concatenate([r[0, 0, 0] for r in k_refs[pages]], axis=1).astype(BF16)
        vt = jnp.concatenate([r[0, 0, 0] for r in v_refs[pages]], axis=1).astype(BF16)
        s_sel = _dot(qs, kt)
        m = jnp.maximum(jnp.max(s_sel, axis=1, keepdims=True), jnp.max(s_own, axis=1, keepdims=True))
        p_sel = jnp.exp(s_sel - m)
        p_own = jnp.exp(s_own - m)
        denom = jnp.sum(p_sel, axis=1, keepdims=True) + jnp.sum(p_own, axis=1, keepdims=True)
        o = (_dot_nt(p_sel.astype(BF16), vt) + _dot(p_own.astype(BF16), vn)) / denom
        o_ref[0, 0, t:t + 1, :] = o[t:t + 1]


def _sample_attn(q, k_new, v_new, cache_kt, cache_vt, layer, phys):
    n, heads, t, hd = q.shape
    page = cache_kt.shape[4]
    n_pg = phys.shape[-1]

    def page_map(b, h, ph, j):
        return (layer, ph[(b * heads + h) * (t * n_pg) + j], h, 0, 0)

    pages = [pl.BlockSpec((1, 1, 1, hd, page), functools.partial(page_map, j=j))
             for j in range(t * n_pg)]
    own = pl.BlockSpec((1, 1, t, hd), lambda b, h, ph: (b, h, 0, 0))
    return pl.pallas_call(
        functools.partial(_sample_attn_kernel, n_pg=n_pg),
        grid_spec=pltpu.PrefetchScalarGridSpec(
            num_scalar_prefetch=1, grid=(n, heads),
            in_specs=[own, own, own] + pages * 2, out_specs=own),
        out_shape=jax.ShapeDtypeStruct((n, heads, t, hd), F32),
        compiler_params=_cparams("parallel", "parallel"),
        name="sample_attn",
    )(phys.reshape(-1), q, k_new, v_new, *([cache_kt] * (t * n_pg)), *([cache_vt] * (t * n_pg)))


def _gla_kernel(q_ref, k_ref, v_ref, la_ref, gb_ref, gn_ref, o_ref, sfin_ref, st_sc, qk_sc, b_sc,
                la_sc, attn_sc, *, chunk, sub, n_chunk, n_seq, heads, dk, dv):
    step = pl.program_id(1)

    @pl.when(step == 0)
    def _():
        st_sc[...] = jnp.zeros(st_sc.shape, F32)

    r_i = lax.broadcasted_iota(jnp.int32, (chunk, chunk), 0)
    c_i = lax.broadcasted_iota(jnp.int32, (chunk, chunk), 1)
    tri = r_i >= c_i
    tri_b = tri.astype(BF16)
    row_k = lax.broadcasted_iota(jnp.int32, (chunk, dk), 0)
    q_scale = dk ** -0.5

    def intra_fast(q, k, b):
        rows = []
        for i in range(chunk // sub):
            lo, hi = i * sub, (i + 1) * sub
            ref = b[lo - 1:lo, :] if i > 0 else jnp.zeros((1, dk), F32)
            qt = (q[lo:hi] * jnp.exp(b[lo:hi] - ref)).astype(BF16)
            kt = (k * jnp.exp(jnp.where(row_k < hi, ref - b, -jnp.inf))).astype(BF16)
            rows.append(_dot_nt(qt, kt))
        return jnp.where(tri, jnp.concatenate(rows, axis=0), 0.0)

    def intra_exact(q, k, b, la):
        qk_sc[...] = q
        b_sc[...] = b
        la_sc[...] = la
        col = lax.broadcasted_iota(jnp.int32, (1, chunk), 1)

        def row(t, carry):
            q_t = qk_sc[pl.ds(t, 1), :]
            la_t = la_sc[pl.ds(t, 1), :]
            before = b_sc[pl.ds(t, 1), :] - la_t
            kt = (k * jnp.exp(jnp.where(row_k < t, before - b, -jnp.inf))).astype(BF16)
            a_row = _dot_nt((q_t * jnp.exp(la_t)).astype(BF16), kt)
            k_t = jnp.sum(jnp.where(row_k == t, k, 0.0), axis=0, keepdims=True)
            diag = jnp.sum(q_t * k_t, axis=1, keepdims=True)
            attn_sc[pl.ds(t, 1), :] = jnp.where(col == t, diag, a_row)
            return carry

        lax.fori_loop(0, chunk, row, 0)
        return attn_sc[...]

    def one_chunk(ci, carry, exact):
        off = pl.multiple_of(ci * chunk, chunk)
        rows_ds = pl.ds(off, chunk)
        for s in range(n_seq):
            la = la_ref[s, rows_ds, :]
            la_hi = la.astype(BF16)
            la_mid = (la - la_hi.astype(F32)).astype(BF16)
            la_lo = (la - la_hi.astype(F32) - la_mid.astype(F32)).astype(BF16)
            b_all = _dot(tri_b, la_hi) + _dot(tri_b, la_mid) + _dot(tri_b, la_lo)
            for h in range(heads):
                kc = slice(h * dk, (h + 1) * dk)
                vc = slice(h * dv, (h + 1) * dv)
                q = q_ref[s, rows_ds, kc] * q_scale
                k = k_ref[s, rows_ds, kc]
                v = v_ref[s, rows_ds, vc]
                b = b_all[:, kc]
                st = st_sc[s, h]
                o = _dot_nt((q * jnp.exp(b)).astype(BF16), st.astype(BF16))
                attn = intra_exact(q, k, b, la[:, kc]) if exact else intra_fast(q, k, b)
                o = o + _dot(attn.astype(BF16), v)
                b_last = b[chunk - 1:chunk, :]
                k_dec = (k * jnp.exp(b_last - b)).astype(BF16)
                st_sc[s, h] = st * jnp.exp(b_last) + _dot_tn(v, k_dec)
                gb = gb_ref[s, rows_ds, vc].astype(F32)
                o_ref[s, rows_ds, vc] = (_rms(o) * gn_ref[...] * (gb * jax.nn.sigmoid(gb))).astype(BF16)
        return carry

    safe = jnp.min(la_ref[...]) * sub > GLA_FAST_MIN_LOG_DECAY

    @pl.when(safe)
    def _():
        lax.fori_loop(0, n_chunk, functools.partial(one_chunk, exact=False), 0)

    @pl.when(jnp.logical_not(safe))
    def _():
        lax.fori_loop(0, n_chunk, functools.partial(one_chunk, exact=True), 0)

    @pl.when(step == pl.num_programs(1) - 1)
    def _():
        sfin_ref[...] = st_sc[...]


def _gla_prompt(qb, kb, vb, la, gb, gla_norm_g, b0, n, *, heads, dk, dv):
    _, t, _ = qb.shape
    rows = min(GLA_STEP_ROWS, t)
    chunk = min(GLA_CHUNK, rows)
    n_seq = GLA_SEQS if n % GLA_SEQS == 0 and b0 % GLA_SEQS == 0 else 1
    assert t % rows == 0 and rows % chunk == 0 and chunk % GLA_SUB == 0
    kspec = pl.BlockSpec((n_seq, rows, heads * dk), lambda b, s: (b + b0 // n_seq, s, 0))
    vspec = pl.BlockSpec((n_seq, rows, heads * dv), lambda b, s: (b + b0 // n_seq, s, 0))
    out_spec = pl.BlockSpec((n_seq, rows, heads * dv), lambda b, s: (b, s, 0))
    return pl.pallas_call(
        functools.partial(_gla_kernel, chunk=chunk, sub=GLA_SUB, n_chunk=rows // chunk,
                          n_seq=n_seq, heads=heads, dk=dk, dv=dv),
        grid=(n // n_seq, t // rows),
        in_specs=[kspec, kspec, vspec, kspec, vspec, pl.BlockSpec((1, dv), lambda b, s: (0, 0))],
        out_specs=[out_spec, pl.BlockSpec((n_seq, heads, dv, dk), lambda b, s: (b, 0, 0, 0))],
        out_shape=[jax.ShapeDtypeStruct((n, t, heads * dv), BF16),
                   jax.ShapeDtypeStruct((n, heads, dv, dk), F32)],
        scratch_shapes=[pltpu.VMEM((n_seq, heads, dv, dk), F32), pltpu.VMEM((chunk, dk), F32),
                        pltpu.VMEM((chunk, dk), F32), pltpu.VMEM((chunk, dk), F32),
                        pltpu.VMEM((chunk, chunk), F32)],
        compiler_params=_cparams("parallel", "arbitrary"),
        name="gla_prompt",
    )(qb, kb, vb, la, gb, gla_norm_g)


def _gla_sample_kernel(qt_ref, kt_ref, lat_ref, v_ref, gb_ref, gn_ref, s0_ref, o_ref, sfin_ref,
                       *, heads, dk, dv):
    for h in range(heads):
        vc = slice(h * dv, (h + 1) * dv)
        s = s0_ref[0, 0, h]
        qt = qt_ref[0, h] * dk ** -0.5
        kt = kt_ref[0, h]
        at = jnp.exp(lat_ref[0, h])
        v = v_ref[0, :, vc].astype(F32)
        outs = []
        for t in range(v.shape[0]):
            s = at[:, t:t + 1] * s + kt[:, t:t + 1] * v[t:t + 1, :]
            outs.append(jnp.sum(qt[:, t:t + 1] * s, axis=0, keepdims=True))
        o = jnp.concatenate(outs, axis=0)
        gb = gb_ref[0, :, vc].astype(F32)
        o_ref[0, :, vc] = (_rms(o) * gn_ref[...] * (gb * jax.nn.sigmoid(gb))).astype(BF16)
        sfin_ref[0, h] = s


def _gla_sample(qt, kt, lat, vb, gb, gla_norm_g, state, layer, *, dk, dv):
    n, heads, _, t = qt.shape
    tspec = pl.BlockSpec((1, heads, dk, t), lambda b: (b, 0, 0, 0))
    vspec = pl.BlockSpec((1, t, heads * dv), lambda b: (b, 0, 0))
    return pl.pallas_call(
        functools.partial(_gla_sample_kernel, heads=heads, dk=dk, dv=dv),
        grid=(n,),
        in_specs=[tspec, tspec, tspec, vspec, vspec, pl.BlockSpec((1, dv), lambda b: (0, 0)),
                  pl.BlockSpec((1, 1, heads, dk, dv), lambda b: (layer, b, 0, 0, 0))],
        out_specs=[vspec, pl.BlockSpec((1, heads, dk, dv), lambda b: (b, 0, 0, 0))],
        out_shape=[jax.ShapeDtypeStruct((n, t, heads * dv), BF16),
                   jax.ShapeDtypeStruct((n, heads, dk, dv), F32)],
        compiler_params=_cparams("parallel"),
        name="gla_sample",
    )(qt, kt, lat, vb, gb, gla_norm_g, state)


def _merge_kernel(x_ref, oa_ref, ob_ref, g1_ref, wa_ref, wb_ref, wmg_ref, wo_ref, g2_ref, wrh_ref,
                  wrl_ref, br_ref, x2_ref, h2_ref, idx_ref, gate_ref, *, d, n_exp):
    x = x_ref[...]
    hb = (_rms(x) * g1_ref[...]).astype(BF16)
    y_a = _dot(oa_ref[...], wa_ref[...])
    y_b = _dot(ob_ref[...], wb_ref[...])
    mixed = (jax.nn.sigmoid(_dot(hb, wmg_ref[:, :d])) * y_a
             + jax.nn.sigmoid(_dot(hb, wmg_ref[:, d:])) * y_b)
    x2 = x + _dot(mixed.astype(BF16), wo_ref[...])
    x2_ref[...] = x2
    h2 = _rms(x2) * g2_ref[...]
    h2_hi = h2.astype(BF16)
    h2_ref[...] = h2_hi
    h2_lo = (h2 - h2_hi.astype(F32)).astype(BF16)
    logits = (_dot(h2_hi, wrh_ref[...]) + _dot(h2_hi, wrl_ref[...]) + _dot(h2_lo, wrh_ref[...])
              + br_ref[...])
    col = lax.broadcasted_iota(jnp.int32, logits.shape, 1)
    _, picks = _top_mask(logits, col, MOE_TOP_K, n_exp, axis=1)
    top_val = jnp.concatenate([p[1] for p in picks], axis=1)
    e = jnp.exp(top_val - top_val[:, :1])
    idx_ref[...] = jnp.concatenate([p[0] for p in picks], axis=1)
    gate_ref[...] = e / jnp.sum(e, axis=1, keepdims=True)


def _merge(x, row0, o_a, o_b, norm1_g, w_a, w_b, w_mg, w_o, norm2_g, w_router, b_router):
    n_tok, d = o_a.shape[0], x.shape[1]
    n_exp = w_router.shape[1]
    w_router_hi = w_router.astype(BF16)
    w_router_lo = (w_router - w_router_hi.astype(F32)).astype(BF16)
    tm = min(PROJ_ROWS, n_tok)
    assert n_tok % tm == 0 and row0 % tm == 0
    full = lambda a: pl.BlockSpec(a.shape, lambda i: (0,) * a.ndim)
    rows = lambda w: pl.BlockSpec((tm, w), lambda i: (i, 0))
    return pl.pallas_call(
        functools.partial(_merge_kernel, d=d, n_exp=n_exp),
        grid=(n_tok // tm,),
        in_specs=[pl.BlockSpec((tm, d), lambda i: (i + row0 // tm, 0)),
                  rows(o_a.shape[1]), rows(o_b.shape[1]), full(norm1_g), full(w_a),
                  full(w_b), full(w_mg), full(w_o), full(norm2_g), full(w_router_hi),
                  full(w_router_lo), full(b_router)],
        out_specs=[rows(d), rows(d), rows(MOE_TOP_K), rows(MOE_TOP_K)],
        out_shape=[jax.ShapeDtypeStruct((n_tok, d), F32), jax.ShapeDtypeStruct((n_tok, d), BF16),
                   jax.ShapeDtypeStruct((n_tok, MOE_TOP_K), jnp.int32),
                   jax.ShapeDtypeStruct((n_tok, MOE_TOP_K), F32)],
        compiler_params=_cparams("parallel"),
        name="merge",
    )(x, o_a, o_b, norm1_g, w_a, w_b, w_mg, w_o, norm2_g, w_router_hi, w_router_lo, b_router)


def _expert_kernel(te_ref, nt_ref, x_ref, wup_ref, bg_ref, bl_ref, wdn_ref, bdn_ref, y_ref,
                   wg_sc, wl_sc, wd_sc, t_sc, *, slab):
    i = pl.program_id(0)

    @pl.when((i == 0) | (te_ref[i] != te_ref[jnp.maximum(i - 1, 0)]))
    def _():
        for c in range(0, wup_ref.shape[2], slab):
            rows = slice(c // 2, (c + slab) // 2)
            for j in range(t_sc.shape[0]):
                lanes = slice(j * LANES, (j + 1) * LANES)
                t_sc[j] = wup_ref[0, lanes, c:c + slab].T
                wg_sc[rows, lanes] = t_sc[j, pl.ds(0, slab // 2, stride=2), :].astype(BF16)
                wl_sc[rows, lanes] = t_sc[j, pl.ds(1, slab // 2, stride=2), :].astype(BF16)
        wd_sc[...] = wdn_ref[0].astype(BF16)

    @pl.when(i < nt_ref[0])
    def _():
        x = x_ref[...]
        x_glu = jnp.minimum(_dot_nt(x, wg_sc[...]) + bg_ref[0], SWIGLU_LIMIT)
        x_lin = jnp.clip(_dot_nt(x, wl_sc[...]) + bl_ref[0], -SWIGLU_LIMIT, SWIGLU_LIMIT)
        a = x_glu * jax.nn.sigmoid(SWIGLU_ALPHA * x_glu) * (x_lin + 1.0)
        y_ref[...] = (_dot(a.astype(BF16), wd_sc[...]) + bdn_ref[0]).astype(y_ref.dtype)

    @pl.when(i >= nt_ref[0])
    def _():
        y_ref[...] = jnp.zeros(y_ref.shape, y_ref.dtype)


def _experts(buf, tile_e, n_used, w_up, b_glu, b_lin, w_dn, b_dn, *, tile_rows):
    r, d = buf.shape
    d_ff = w_dn.shape[1]
    slab = min(2 * LANES, 2 * d_ff)
    by_expert = lambda *blk: pl.BlockSpec((1,) + blk, lambda i, te, nt: (te[i], 0, 0))
    return pl.pallas_call(
        functools.partial(_expert_kernel, slab=slab),
        grid_spec=pltpu.PrefetchScalarGridSpec(
            num_scalar_prefetch=2, grid=(r // tile_rows,),
            in_specs=[pl.BlockSpec((tile_rows, d), lambda i, te, nt: (i, 0)),
                      by_expert(d, 2 * d_ff), by_expert(1, d_ff), by_expert(1, d_ff),
                      by_expert(d_ff, d), by_expert(1, d)],
            out_specs=pl.BlockSpec((tile_rows, d), lambda i, te, nt: (i, 0)),
            scratch_shapes=[pltpu.VMEM((d_ff, d), BF16), pltpu.VMEM((d_ff, d), BF16),
                            pltpu.VMEM((d_ff, d), BF16), pltpu.VMEM((d // LANES, slab, LANES), F32)]),
        out_shape=jax.ShapeDtypeStruct((r, d), BF16),
        compiler_params=_cparams("arbitrary"),
        name="experts",
    )(tile_e, n_used, buf, w_up, b_glu, b_lin, w_dn, b_dn)


def _rank_kernel(idx_ref, rank_ref, cnt_ref, seen_sc):
    @pl.when(pl.program_id(0) == 0)
    def _():
        seen_sc[...] = jnp.zeros(seen_sc.shape, F32)

    idx = idx_ref[...]
    tr, top_k = idx.shape
    lane = lax.broadcasted_iota(jnp.int32, (tr, LANES), 1)
    hits = [idx[:, k:k + 1] == lane for k in range(top_k)]
    routed = hits[0].astype(F32)
    for h in hits[1:]:
        routed = routed + h.astype(F32)
    earlier = (lax.broadcasted_iota(jnp.int32, (tr, tr), 1)
               < lax.broadcasted_iota(jnp.int32, (tr, tr), 0)).astype(BF16)
    before = _dot(earlier, routed.astype(BF16)) + seen_sc[...]
    rank_ref[...] = jnp.concatenate(
        [jnp.sum(jnp.where(h, before, 0.0), axis=1, keepdims=True) for h in hits],
        axis=1).astype(jnp.int32)
    seen_sc[...] = seen_sc[...] + jnp.sum(routed, axis=0, keepdims=True)
    cnt_ref[...] = seen_sc[...]


def _rank(top_idx):
    n_tok, top_k = top_idx.shape
    tr = min(ROUTE_ROWS, n_tok)
    assert n_tok % tr == 0
    return pl.pallas_call(
        _rank_kernel,
        grid=(n_tok // tr,),
        in_specs=[pl.BlockSpec((tr, top_k), lambda i: (i, 0))],
        out_specs=[pl.BlockSpec((tr, top_k), lambda i: (i, 0)),
                   pl.BlockSpec((1, LANES), lambda i: (0, 0))],
        out_shape=[jax.ShapeDtypeStruct((n_tok, top_k), jnp.int32),
                   jax.ShapeDtypeStruct((1, LANES), F32)],
        scratch_shapes=[pltpu.VMEM((1, LANES), F32)],
        compiler_params=_cparams("arbitrary"),
        name="moe_rank",
    )(top_idx)


def _combine_kernel(x_ref, y_ref, g_ref, o_ref):
    acc = x_ref[...]
    g = g_ref[...]
    for k in range(y_ref.shape[0]):
        acc = acc + g[:, k:k + 1] * y_ref[k].astype(F32)
    o_ref[...] = acc


def _combine(x2, y_kt, gate):
    top_k, n_tok, d = y_kt.shape
    tm = min(ROUTE_ROWS, n_tok)
    assert n_tok % tm == 0
    return pl.pallas_call(
        _combine_kernel,
        grid=(n_tok // tm,),
        in_specs=[pl.BlockSpec((tm, d), lambda i: (i, 0)),
                  pl.BlockSpec((top_k, tm, d), lambda i: (0, i, 0)),
                  pl.BlockSpec((tm, top_k), lambda i: (i, 0))],
        out_specs=pl.BlockSpec((tm, d), lambda i: (i, 0)),
        out_shape=jax.ShapeDtypeStruct((n_tok, d), F32),
        compiler_params=_cparams("parallel"),
        name="moe_combine",
    )(x2, y_kt, gate)


def _moe(x2, h2, top_idx, gate, expert_weights, tile_rows):
    n_tok, d = h2.shape
    n_exp = expert_weights[0].shape[0]
    assert n_exp <= LANES
    n_assign = n_tok * MOE_TOP_K
    rank, cnt = _rank(top_idx)
    counts = cnt[0, :n_exp].astype(jnp.int32)
    padded = (counts + tile_rows - 1) // tile_rows * tile_rows
    pend = jnp.cumsum(padded)
    experts = jnp.arange(n_exp, dtype=jnp.int32)
    first_row = jnp.sum(jnp.where(top_idx[:, :, None] == experts, pend - padded, 0), axis=-1)
    dest = (first_row + rank).reshape(-1)
    n_tiles = -(-n_assign // tile_rows) + n_exp
    tile_row0 = jnp.arange(n_tiles, dtype=jnp.int32) * tile_rows
    tile_e = jnp.minimum(jnp.sum((pend[None, :] <= tile_row0[:, None]).astype(jnp.int32), axis=1),
                         n_exp - 1)
    n_used = (pend[-1:] // tile_rows).astype(jnp.int32)
    src = jnp.zeros((n_tiles * tile_rows,), jnp.int32).at[dest].set(
        jnp.arange(n_assign, dtype=jnp.int32) // MOE_TOP_K,
        unique_indices=True, mode='promise_in_bounds')
    y = _experts(h2[src], tile_e, n_used, *expert_weights, tile_rows=tile_rows)
    y_kt = y[dest.reshape(n_tok, MOE_TOP_K).T]
    return _combine(x2, y_kt, gate)


def _rope_tables_t(pos, hd):
    half = hd // 2
    inv = ROPE_THETA ** (-jnp.arange(half, dtype=F32) / half)
    ang = inv[:, None] * pos.astype(F32)[None, :]
    cos, sin = jnp.cos(ang), jnp.sin(ang)
    return jnp.concatenate([cos, cos], axis=0), jnp.concatenate([-sin, sin], axis=0)


def kernel(x_prompt, x_sample, cache_k, cache_v, state_gla, page_table, norm1_g, w_in, q_norm_g,
           k_norm_g, w_gate_up, b_gate_up, gla_norm_g, w_branch_a, w_branch_b, w_merge_gate, w_out,
           norm2_g, w_router, b_router, w_up, b_up, w_down, b_down):
    depth = norm1_g.shape[0]
    assert depth == 1, "sample/prompt streams are only chained through one layer here"
    layer = 0
    n_p, t_p, d = x_prompt.shape
    n_s, t_s, _ = x_sample.shape
    heads, page, hd = cache_k.shape[2:]
    b_heads, dk, dv = state_gla.shape[2:]
    bk, bv = b_heads * dk, b_heads * dv
    aqk = heads * hd
    rank = w_gate_up.shape[1]
    past_len = page_table.shape[1] * page
    ppb = MOBA_BLOCK // page
    n_full = past_len // MOBA_BLOCK
    assert past_len % MOBA_BLOCK == 0 and n_full >= MOBA_TOPK and rank <= LANES
    assert w_in.shape[2] == 3 * aqk + 2 * bk + 2 * bv + rank

    w_qkv_t = w_in[layer][:, :3 * aqk].T.astype(BF16)
    w_rest = jnp.pad(w_in[layer][:, 3 * aqk:], ((0, 0), (0, LANES - rank))).astype(BF16)
    w_gu_p = jnp.pad(w_gate_up[layer], ((0, LANES - rank), (0, 0))).astype(BF16)
    g1 = norm1_g[layer][None, :]
    g2 = norm2_g[layer][None, :]
    qg = q_norm_g[layer][:, None]
    kg = k_norm_g[layer][:, None]
    bgu = b_gate_up[layer][None, :]
    gn = gla_norm_g[layer][None, :]
    w_a = w_branch_a[layer].astype(BF16)
    w_b = w_branch_b[layer].astype(BF16)
    w_mg = w_merge_gate[layer].astype(BF16)
    w_o = w_out[layer].astype(BF16)
    w_r = w_router[layer]
    b_r = b_router[layer][None, :]
    expert_weights = (w_up[layer], b_up[layer][:, None, 0::2], b_up[layer][:, None, 1::2],
                      w_down[layer], b_down[layer][:, None, :])
    proj = functools.partial(_project, norm1_g=g1, w_qkv_t=w_qkv_t, w_rest=w_rest, q_norm_g=qg,
                             k_norm_g=kg, w_gu_p=w_gu_p, b_gu=bgu, heads=heads, hd=hd, bk=bk, bv=bv)
    merge = functools.partial(_merge, norm1_g=g1, w_a=w_a, w_b=w_b, w_mg=w_mg, w_o=w_o,
                              norm2_g=g2, w_router=w_r, b_router=b_r)

    cos_p, sin_p = _rope_tables_t(jnp.arange(t_p), hd)
    qt, kt, vt, qb, kb, vb, la, gb = proj(x_prompt, cos_p, sin_p)
    n_part = n_p // PROMPT_SPLITS if n_p % PROMPT_SPLITS == 0 else n_p
    x_rows = x_prompt.reshape(n_p * t_p, d)
    y_parts, st_parts = [], []
    for b0 in range(0, n_p, n_part):
        o_a = _moba_prompt(qt, kt, vt, b0, n_part)
        o_b, st = _gla_prompt(qb, kb, vb, la, gb, gn, b0, n_part, heads=b_heads, dk=dk, dv=dv)
        x2, h2, idx, gate = merge(x_rows, b0 * t_p, o_a.reshape(n_part * t_p, -1),
                                  o_b.reshape(n_part * t_p, -1))
        y_parts.append(_moe(x2, h2, idx, gate, expert_weights, MOE_ROWS))
        st_parts.append(st)
    y_prompt = jnp.concatenate(y_parts).reshape(n_p, t_p, d)
    s_prompt = jnp.swapaxes(jnp.concatenate(st_parts), 2, 3)

    n_tok_s = n_s * t_s
    cos_s, sin_s = _rope_tables_t(past_len + jnp.arange(t_s), hd)
    qt_s, kt_s, vt_s, qb_s, kb_s, vb_s, la_s, gb_s = proj(
        x_sample.reshape(1, n_tok_s, d), jnp.tile(cos_s, (1, n_s)), jnp.tile(sin_s, (1, n_s)))
    per_seq = lambda a: a.reshape(heads, hd, n_s, t_s).transpose(2, 0, 3, 1)
    qa_s, ka_s, va_s = per_seq(qt_s), per_seq(kt_s), per_seq(vt_s)
    cache_kt = jnp.swapaxes(cache_k, 3, 4)
    cache_vt = jnp.swapaxes(cache_v, 3, 4)
    kmean = _sample_kmean(cache_kt, layer, page_table, n_full, ppb).transpose(0, 2, 1, 3)
    blk_idx = _sample_pick(qa_s, kmean, MOBA_TOPK)
    lpages = blk_idx[..., None] * ppb + jnp.arange(ppb, dtype=jnp.int32)
    phys = jnp.take_along_axis(page_table[:, None, None, :],
                               lpages.reshape(n_s, heads, t_s, MOBA_TOPK * ppb), axis=-1)
    o_a_s = _sample_attn(qa_s, ka_s, va_s, cache_kt, cache_vt, layer, phys)
    o_a_s = o_a_s.transpose(0, 2, 1, 3).reshape(n_tok_s, aqk).astype(BF16)
    tr = lambda a: a.reshape(n_s, t_s, b_heads, dk).transpose(0, 2, 3, 1)
    o_b_s, s_sample = _gla_sample(tr(qb_s), tr(kb_s), tr(la_s), vb_s.reshape(n_s, t_s, bv),
                                  gb_s.reshape(n_s, t_s, bv), gn, state_gla, layer, dk=dk, dv=dv)
    x2_s, h2_s, idx_s, gate_s = merge(x_sample.reshape(n_tok_s, d), 0, o_a_s,
                                      o_b_s.reshape(n_tok_s, bv))
    y_sample = _moe(x2_s, h2_s, idx_s, gate_s, expert_weights,
                    min(MOE_ROWS, n_tok_s)).reshape(n_s, t_s, d)
    k_prompt = jnp.swapaxes(kt, 2, 3)
    v_prompt = jnp.swapaxes(vt, 2, 3)
    return (y_prompt, y_sample, k_prompt[None], v_prompt[None], ka_s[None], va_s[None],
            s_prompt[None], s_sample[None])
```

```python
import functools

import jax
import jax.numpy as jnp
from jax import lax
from jax.experimental import pallas as pl
from jax.experimental.pallas import tpu as pltpu

F32 = jnp.float32
BF16 = jnp.bfloat16
HIGHEST = lax.Precision.HIGHEST

MOBA_BLOCK = 256
MOBA_TOPK = 3
ROPE_THETA = 10000.0
GLA_GATE_NORM = 16.0
NORM_EPS = 1e-6
MOE_TOP_K = 4
SWIGLU_ALPHA = 1.702
SWIGLU_LIMIT = 7.0

LANES = 128
PAD_ROWS = 16
VMEM_LIMIT_BYTES = 56 * 1024 * 1024

MASK_VALUE = -1e30
LOG2_E = 1.4426950408889634
PROJ_ROWS = 512
GLA_CHUNK = 128
GLA_SUB = 32
GLA_STEP_ROWS = 512
GLA_SEQS = 2
GLA_FAST_MIN_LOG_DECAY = -80.0
MOE_ROWS = 512
ROUTE_ROWS = 1024
PROMPT_SPLITS = 1
MOBA_GROUP = 4
SAMPLE_KMEAN_PAGES = 16


def _cparams(*sem):
    return pltpu.CompilerParams(dimension_semantics=sem, vmem_limit_bytes=VMEM_LIMIT_BYTES)


def _rms(x):
    return x * lax.rsqrt(jnp.mean(x * x, axis=-1, keepdims=True) + NORM_EPS)


def _dot(a, b):
    return jnp.dot(a, b, preferred_element_type=F32)


def _dot_nt(a, b, precision=None):
    return lax.dot_general(a, b, (((1,), (1,)), ((), ())), precision=precision,
                           preferred_element_type=F32)


def _dot_tn(a, b):
    return lax.dot_general(a, b, (((0,), (0,)), ((), ())), preferred_element_type=F32)


def _top_mask(g, pos, n_pick, n_pos, axis):
    sel = jnp.zeros(g.shape, F32)
    picks = []
    for _ in range(n_pick):
        mx = jnp.max(g, axis=axis, keepdims=True)
        cand = jnp.where((g == mx) & (mx > -jnp.inf), pos, n_pos)
        idx = jnp.min(cand, axis=axis, keepdims=True)
        pick = pos == idx
        sel = jnp.where(pick, 1.0, sel)
        g = jnp.where(pick, -jnp.inf, g)
        picks.append((idx, mx))
    return sel, picks


def _proj_kernel(x_ref, g1_ref, wqkv_ref, wrest_ref, qg_ref, kg_ref, wgu_ref, bgu_ref, cos_ref,
                 sin_ref, qt_ref, kt_ref, vt_ref, qb_ref, kb_ref, vb_ref, la_ref, gb_ref,
                 *, heads, hd, bk, bv):
    x = x_ref[0]
    hb = (_rms(x) * g1_ref[...]).astype(BF16)
    cos = cos_ref[...]
    sin = sin_ref[...]
    aqk = heads * hd

    def mm_t(lo, hi):
        return _dot_nt(wqkv_ref[lo:hi, :], hb)

    def norm_rope_t(z, g_ref, out_ref):
        for h in range(heads):
            y = z[h * hd:(h + 1) * hd, :]
            y = y * lax.rsqrt(jnp.mean(y * y, axis=0, keepdims=True) + NORM_EPS) * g_ref[...]
            rot = jnp.concatenate([y[hd // 2:], y[:hd // 2]], axis=0)
            out_ref[0, h] = y * cos + rot * sin

    norm_rope_t(mm_t(0, aqk), qg_ref, qt_ref)
    norm_rope_t(mm_t(aqk, 2 * aqk), kg_ref, kt_ref)
    zv = mm_t(2 * aqk, 3 * aqk)
    for h in range(heads):
        vt_ref[0, h] = zv[h * hd:(h + 1) * hd, :]

    def mm(lo, hi):
        return _dot(hb, wrest_ref[:, lo:hi])

    qb_ref[0] = mm(0, bk)
    kb_ref[0] = mm(bk, 2 * bk)
    vb_ref[0] = mm(2 * bk, 2 * bk + bv).astype(BF16)
    gb_ref[0] = mm(2 * bk + bv, 2 * bk + 2 * bv).astype(BF16)
    lr = mm(2 * bk + 2 * bv, 2 * bk + 2 * bv + LANES)
    u = _dot(lr.astype(BF16), wgu_ref[...]) + bgu_ref[...]
    log_sig = jnp.minimum(u, 0.0) - jnp.log(1.0 + jnp.exp(-jnp.abs(u)))
    la_ref[0] = log_sig / GLA_GATE_NORM


def _project(x, cos_t, sin_t, norm1_g, w_qkv_t, w_rest, q_norm_g, k_norm_g, w_gu_p, b_gu,
             *, heads, hd, bk, bv):
    n, t, d = x.shape
    tm = min(PROJ_ROWS, t)
    assert t % tm == 0
    full = lambda a: pl.BlockSpec(a.shape, lambda b, i: (0,) * a.ndim)
    head_out = pl.BlockSpec((1, heads, hd, tm), lambda b, i: (b, 0, 0, i))
    row_out = lambda w: pl.BlockSpec((1, tm, w), lambda b, i: (b, i, 0))
    table = pl.BlockSpec((hd, tm), lambda b, i: (0, i))
    head_shape = jax.ShapeDtypeStruct((n, heads, hd, t), F32)
    return pl.pallas_call(
        functools.partial(_proj_kernel, heads=heads, hd=hd, bk=bk, bv=bv),
        grid=(n, t // tm),
        in_specs=[pl.BlockSpec((1, tm, d), lambda b, i: (b, i, 0)),
                  full(norm1_g), full(w_qkv_t), full(w_rest), full(q_norm_g), full(k_norm_g),
                  full(w_gu_p), full(b_gu), table, table],
        out_specs=[head_out, head_out, head_out, row_out(bk), row_out(bk), row_out(bv),
                   row_out(bk), row_out(bv)],
        out_shape=[head_shape, head_shape, head_shape,
                   jax.ShapeDtypeStruct((n, t, bk), F32),
                   jax.ShapeDtypeStruct((n, t, bk), F32),
                   jax.ShapeDtypeStruct((n, t, bv), BF16),
                   jax.ShapeDtypeStruct((n, t, bk), F32),
                   jax.ShapeDtypeStruct((n, t, bv), BF16)],
        compiler_params=_cparams("parallel", "parallel"),
        name="proj",
    )(x, norm1_g, w_qkv_t, w_rest, q_norm_g, k_norm_g, w_gu_p, b_gu, cos_t, sin_t)


def _moba_kernel(qt_ref, kt_ref, vt_ref, o_ref, k_sc, vt_sc, kmean_sc, qs_sc, bias_sc, m_sc, acc_sc,
                 sa_sc, sb_sc, ma_sc, mb_sc, *, nb, blk, hd, hp, grp):
    c = pl.program_id(2)
    nq = hp * blk
    vr = hd + PAD_ROWS
    n_groups = nb // grp
    cols = [slice(i * blk, (i + 1) * blk) for i in range(hp)]

    @pl.when(c == 0)
    def _():
        row = lax.broadcasted_iota(jnp.int32, (LANES - hd, blk), 0)
        ones_row = (lax.broadcasted_iota(jnp.int32, (PAD_ROWS, blk), 0) == 0).astype(BF16)

        def stage_kv(j, carry):
            off = pl.multiple_of(j * blk, blk)
            place = (row == j % grp).astype(F32)
            for i in range(hp):
                k_nat = jnp.concatenate([kt_ref[0, i, :, pl.ds(off, blk)], place], axis=0).T
                k_sc[i, pl.ds(off, blk), :] = k_nat.astype(BF16)
                kmean_sc[i, pl.ds(j, 1), :] = jnp.mean(k_nat, axis=0, keepdims=True)
                vt_sc[i * vr:i * vr + hd, pl.ds(off, blk)] = vt_ref[0, i, :, pl.ds(off, blk)].astype(BF16)
                vt_sc[i * vr + hd:(i + 1) * vr, pl.ds(off, blk)] = ones_row
            return carry

        lax.fori_loop(0, nb, stage_kv, 0)

    zero_rows = jnp.zeros((LANES - hd, blk), F32)
    q_pad = [jnp.concatenate([qt_ref[0, i], zero_rows], axis=0) for i in range(hp)]
    gate = jnp.concatenate(
        [jnp.dot(kmean_sc[i], q_pad[i], precision=HIGHEST, preferred_element_type=F32)
         for i in range(hp)], axis=1)
    blk_row = lax.broadcasted_iota(jnp.int32, (nb, nq), 0)
    sel, _ = _top_mask(jnp.where(blk_row < c, gate, -jnp.inf), blk_row, min(MOBA_TOPK, nb - 1),
                       nb, axis=0)
    bias = (sel - 1.0) * -MASK_VALUE
    pad = jnp.zeros((PAD_ROWS - grp, blk), F32)
    for g in range(n_groups):
        for i in range(hp):
            bias_sc[g, i] = jnp.concatenate([bias[g * grp:(g + 1) * grp, cols[i]], pad],
                                            axis=0).astype(BF16)
    for i in range(hp):
        qs_sc[i] = (q_pad[i] * (hd ** -0.5 * LOG2_E)).astype(BF16)

    def scores(off, n_keys):
        return jnp.concatenate([_dot(k_sc[i, pl.ds(off, n_keys), :], qs_sc[i]) for i in range(hp)],
                               axis=1)

    def update(s, m_blk, off, n_keys, first):
        m_new = m_blk if first else jnp.maximum(m_sc[...], m_blk)
        pb = jnp.exp2(s - m_new).astype(BF16)
        if not first:
            alpha = jnp.exp2(m_sc[...] - m_new)
        m_sc[...] = m_new
        for i in range(hp):
            pv = _dot(vt_sc[i * vr:(i + 1) * vr, pl.ds(off, n_keys)], pb[:, cols[i]])
            acc_sc[:, cols[i]] = pv if first else alpha[:, cols[i]] * acc_sc[:, cols[i]] + pv

    own_off = pl.multiple_of(c * blk, blk)
    causal = (lax.broadcasted_iota(jnp.int32, (blk, nq), 0)
              <= lax.broadcasted_iota(jnp.int32, (blk, nq), 1) % blk)
    s_own = jnp.where(causal, scores(own_off, blk), MASK_VALUE)

    n_grp = (c + grp - 1) // grp

    def stage(g, s_buf, mx_buf):
        g = jnp.minimum(g, n_groups - 1)
        for i in range(hp):
            qs_sc[i, hd:hd + PAD_ROWS, :] = bias_sc[g, i]
        s = scores(pl.multiple_of(g * (grp * blk), grp * blk), grp * blk)
        s_buf[...] = s
        mx_buf[...] = jnp.max(s, axis=0, keepdims=True)

    def consume(g, s_buf, mx_buf):
        update(s_buf[...], mx_buf[...], pl.multiple_of(g * (grp * blk), grp * blk), grp * blk, False)

    stage(0, sa_sc, ma_sc)
    update(s_own, jnp.max(s_own, axis=0, keepdims=True), own_off, blk, True)

    def body(u, carry):
        stage(2 * u + 1, sb_sc, mb_sc)
        consume(2 * u, sa_sc, ma_sc)
        stage(2 * u + 2, sa_sc, ma_sc)
        consume(2 * u + 1, sb_sc, mb_sc)
        return carry

    lax.fori_loop(0, n_grp // 2, body, 0)

    @pl.when(n_grp % 2 == 1)
    def _():
        consume(n_grp - 1, sa_sc, ma_sc)

    out_t = acc_sc[:hd, :] / acc_sc[hd:hd + 1, :]
    out_t = jnp.concatenate([out_t[:, cols[i]] for i in range(hp)], axis=0)
    o_ref[0] = out_t.T.astype(BF16)


def _moba_prompt(qt, kt, vt, b0, n):
    _, heads, hd, t = qt.shape
    blk = MOBA_BLOCK
    assert t % blk == 0 and LANES % hd == 0
    hp = LANES // hd
    assert heads % hp == 0
    nb = t // blk
    grp = MOBA_GROUP if nb % MOBA_GROUP == 0 else 1
    nq = hp * blk
    kv_spec = pl.BlockSpec((1, hp, hd, t), lambda b, g, c: (b + b0, g, 0, 0))
    return pl.pallas_call(
        functools.partial(_moba_kernel, nb=nb, blk=blk, hd=hd, hp=hp, grp=grp),
        grid=(n, heads // hp, nb),
        in_specs=[pl.BlockSpec((1, hp, hd, blk), lambda b, g, c: (b + b0, g, 0, c)), kv_spec, kv_spec],
        out_specs=pl.BlockSpec((1, blk, hp * hd), lambda b, g, c: (b, c, g)),
        out_shape=jax.ShapeDtypeStruct((n, t, heads * hd), BF16),
        scratch_shapes=[pltpu.VMEM((hp, t, LANES), BF16),
                        pltpu.VMEM((hp * (hd + PAD_ROWS), t), BF16),
                        pltpu.VMEM((hp, nb, LANES), F32),
                        pltpu.VMEM((hp, LANES, blk), BF16),
                        pltpu.VMEM((nb // grp, hp, PAD_ROWS, blk), BF16),
                        pltpu.VMEM((1, nq), F32),
                        pltpu.VMEM((hd + PAD_ROWS, nq), F32),
                        pltpu.VMEM((grp * blk, nq), F32), pltpu.VMEM((grp * blk, nq), F32),
                        pltpu.VMEM((1, nq), F32), pltpu.VMEM((1, nq), F32)],
        compiler_params=_cparams("parallel", "parallel", "arbitrary"),
        name="moba_prompt",
    )(qt, kt, vt)


def _kmean_pages_kernel(pt_ref, *refs, n_pg, ppb, rows):
    del pt_ref
    out_ref = refs[n_pg]
    for b in range(n_pg // ppb):
        acc = refs[b * ppb][0, 0]
        for j in range(1, ppb):
            acc = acc + refs[b * ppb + j][0, 0]
        out_ref[0, b] = jnp.sum(acc, axis=-1) / rows


def _sample_kmean(cache_kt, layer, page_table, n_full, ppb):
    _, _, heads, hd, page = cache_kt.shape
    n, n_pages = page_table.shape
    n_pg = SAMPLE_KMEAN_PAGES
    assert (n_full * ppb) % n_pg == 0 and n_pg % ppb == 0

    def page_map(b, g, pt, j):
        return (layer, pt[b * n_pages + g * n_pg + j], 0, 0, 0)

    specs = [pl.BlockSpec((1, 1, heads, hd, page), functools.partial(page_map, j=j))
             for j in range(n_pg)]
    return pl.pallas_call(
        functools.partial(_kmean_pages_kernel, n_pg=n_pg, ppb=ppb, rows=float(ppb * page)),
        grid_spec=pltpu.PrefetchScalarGridSpec(
            num_scalar_prefetch=1, grid=(n, n_full * ppb // n_pg), in_specs=specs,
            out_specs=pl.BlockSpec((1, n_pg // ppb, heads, hd), lambda b, g, pt: (b, g, 0, 0))),
        out_shape=jax.ShapeDtypeStruct((n, n_full, heads, hd), F32),
        compiler_params=_cparams("parallel", "arbitrary"),
        name="sample_kmean",
    )(page_table.reshape(-1), *([cache_kt] * n_pg))


def _sample_pick_kernel(q_ref, km_ref, idx_ref, *, heads, n_full, n_sel):
    col = lax.broadcasted_iota(jnp.int32, (q_ref.shape[2], n_full), 1)
    for h in range(heads):
        gate = _dot_nt(q_ref[0, h], km_ref[0, h], precision=HIGHEST)
        _, picks = _top_mask(gate, col, n_sel, n_full, axis=1)
        idx_ref[0, h] = jnp.concatenate([p[0] for p in picks], axis=1)


def _sample_pick(q, kmean, n_sel):
    n, heads, t, hd = q.shape
    n_full = kmean.shape[2]
    return pl.pallas_call(
        functools.partial(_sample_pick_kernel, heads=heads, n_full=n_full, n_sel=n_sel),
        grid=(n,),
        in_specs=[pl.BlockSpec((1, heads, t, hd), lambda b: (b, 0, 0, 0)),
                  pl.BlockSpec((1, heads, n_full, hd), lambda b: (b, 0, 0, 0))],
        out_specs=pl.BlockSpec((1, heads, t, n_sel), lambda b: (b, 0, 0, 0)),
        out_shape=jax.ShapeDtypeStruct((n, heads, t, n_sel), jnp.int32),
        compiler_params=_cparams("parallel"),
        name="sample_pick",
    )(q, kmean)


def _sample_attn_kernel(ph_ref, q_ref, kn_ref, vn_ref, *refs, n_pg):
    del ph_ref
    tq, hd = q_ref.shape[2], q_ref.shape[3]
    k_refs, v_refs, o_ref = refs[:tq * n_pg], refs[tq * n_pg:2 * tq * n_pg], refs[2 * tq * n_pg]
    qs = (q_ref[0, 0] * hd ** -0.5).astype(BF16)
    own_ok = (lax.broadcasted_iota(jnp.int32, (tq, tq), 1)
              <= lax.broadcasted_iota(jnp.int32, (tq, tq), 0))
    s_own = jnp.where(own_ok, _dot_nt(qs, kn_ref[0, 0].astype(BF16)), MASK_VALUE)
    vn = vn_ref[0, 0].astype(BF16)
    for t in range(tq):
        pages = slice(t * n_pg, (t + 1) * n_pg)
        kt = jnp.concatenate([r[0, 0, 0] for r in k_refs[pages]], axis=1).astype(BF16)
        vt = jnp.concatenate([r[0, 0, 0] for r in v_refs[pages]], axis=1).astype(BF16)
        s_sel = _dot(qs, kt)
        m = jnp.maximum(jnp.max(s_sel, axis=1, keepdims=True), jnp.max(s_own, axis=1, keepdims=True))
        p_sel = jnp.exp(s_sel - m)
        p_own = jnp.exp(s_own - m)
        denom = jnp.sum(p_sel, axis=1, keepdims=True) + jnp.sum(p_own, axis=1, keepdims=True)
        o = (_dot_nt(p_sel.astype(BF16), vt) + _dot(p_own.astype(BF16), vn)) / denom
        o_ref[0, 0, t:t + 1, :] = o[t:t + 1]


def _sample_attn(q, k_new, v_new, cache_kt, cache_vt, layer, phys):
    n, heads, t, hd = q.shape
    page = cache_kt.shape[4]
    n_pg = phys.shape[-1]

    def page_map(b, h, ph, j):
        return (layer, ph[(b * heads + h) * (t * n_pg) + j], h, 0, 0)

    pages = [pl.BlockSpec((1, 1, 1, hd, page), functools.partial(page_map, j=j))
             for j in range(t * n_pg)]
    own = pl.BlockSpec((1, 1, t, hd), lambda b, h, ph: (b, h, 0, 0))
    return pl.pallas_call(
        functools.partial(_sample_attn_kernel, n_pg=n_pg),
        grid_spec=pltpu.PrefetchScalarGridSpec(
            num_scalar_prefetch=1, grid=(n, heads),
            in_specs=[own, own, own] + pages * 2, out_specs=own),
        out_shape=jax.ShapeDtypeStruct((n, heads, t, hd), F32),
        compiler_params=_cparams("parallel", "parallel"),
        name="sample_attn",
    )(phys.reshape(-1), q, k_new, v_new, *([cache_kt] * (t * n_pg)), *([cache_vt] * (t * n_pg)))


def _gla_kernel(q_ref, k_ref, v_ref, la_ref, gb_ref, gn_ref, o_ref, sfin_ref, st_sc, qk_sc, b_sc,
                la_sc, attn_sc, *, chunk, sub, n_chunk, n_seq, heads, dk, dv):
    step = pl.program_id(1)

    @pl.when(step == 0)
    def _():
        st_sc[...] = jnp.zeros(st_sc.shape, F32)

    r_i = lax.broadcasted_iota(jnp.int32, (chunk, chunk), 0)
    c_i = lax.broadcasted_iota(jnp.int32, (chunk, chunk), 1)
    tri = r_i >= c_i
    tri_b = tri.astype(BF16)
    row_k = lax.broadcasted_iota(jnp.int32, (chunk, dk), 0)
    q_scale = dk ** -0.5

    def intra_fast(q, k, b):
        rows = []
        for i in range(chunk // sub):
            lo, hi = i * sub, (i + 1) * sub
            ref = b[lo - 1:lo, :] if i > 0 else jnp.zeros((1, dk), F32)
            qt = (q[lo:hi] * jnp.exp(b[lo:hi] - ref)).astype(BF16)
            kt = (k * jnp.exp(jnp.where(row_k < hi, ref - b, -jnp.inf))).astype(BF16)
            rows.append(_dot_nt(qt, kt))
        return jnp.where(tri, jnp.concatenate(rows, axis=0), 0.0)

    def intra_exact(q, k, b, la):
        qk_sc[...] = q
        b_sc[...] = b
        la_sc[...] = la
        col = lax.broadcasted_iota(jnp.int32, (1, chunk), 1)

        def row(t, carry):
            q_t = qk_sc[pl.ds(t, 1), :]
            la_t = la_sc[pl.ds(t, 1), :]
            before = b_sc[pl.ds(t, 1), :] - la_t
            kt = (k * jnp.exp(jnp.where(row_k < t, before - b, -jnp.inf))).astype(BF16)
            a_row = _dot_nt((q_t * jnp.exp(la_t)).astype(BF16), kt)
            k_t = jnp.sum(jnp.where(row_k == t, k, 0.0), axis=0, keepdims=True)
            diag = jnp.sum(q_t * k_t, axis=1, keepdims=True)
            attn_sc[pl.ds(t, 1), :] = jnp.where(col == t, diag, a_row)
            return carry

        lax.fori_loop(0, chunk, row, 0)
        return attn_sc[...]

    def one_chunk(ci, carry, exact):
        off = pl.multiple_of(ci * chunk, chunk)
        rows_ds = pl.ds(off, chunk)
        for s in range(n_seq):
            la = la_ref[s, rows_ds, :]
            la_hi = la.astype(BF16)
            la_mid = (la - la_hi.astype(F32)).astype(BF16)
            la_lo = (la - la_hi.astype(F32) - la_mid.astype(F32)).astype(BF16)
            b_all = _dot(tri_b, la_hi) + _dot(tri_b, la_mid) + _dot(tri_b, la_lo)
            for h in range(heads):
                kc = slice(h * dk, (h + 1) * dk)
                vc = slice(h * dv, (h + 1) * dv)
                q = q_ref[s, rows_ds, kc] * q_scale
                k = k_ref[s, rows_ds, kc]
                v = v_ref[s, rows_ds, vc]
                b = b_all[:, kc]
                st = st_sc[s, h]
                o = _dot_nt((q * jnp.exp(b)).astype(BF16), st.astype(BF16))
                attn = intra_exact(q, k, b, la[:, kc]) if exact else intra_fast(q, k, b)
                o = o + _dot(attn.astype(BF16), v)
                b_last = b[chunk - 1:chunk, :]
                k_dec = (k * jnp.exp(b_last - b)).astype(BF16)
                st_sc[s, h] = st * jnp.exp(b_last) + _dot_tn(v, k_dec)
                gb = gb_ref[s, rows_ds, vc].astype(F32)
                o_ref[s, rows_ds, vc] = (_rms(o) * gn_ref[...] * (gb * jax.nn.sigmoid(gb))).astype(BF16)
        return carry

    safe = jnp.min(la_ref[...]) * sub > GLA_FAST_MIN_LOG_DECAY

    @pl.when(safe)
    def _():
        lax.fori_loop(0, n_chunk, functools.partial(one_chunk, exact=False), 0)

    @pl.when(jnp.logical_not(safe))
    def _():
        lax.fori_loop(0, n_chunk, functools.partial(one_chunk, exact=True), 0)

    @pl.when(step == pl.num_programs(1) - 1)
    def _():
        sfin_ref[...] = st_sc[...]


def _gla_prompt(qb, kb, vb, la, gb, gla_norm_g, b0, n, *, heads, dk, dv):
    _, t, _ = qb.shape
    rows = min(GLA_STEP_ROWS, t)
    chunk = min(GLA_CHUNK, rows)
    n_seq = GLA_SEQS if n % GLA_SEQS == 0 and b0 % GLA_SEQS == 0 else 1
    assert t % rows == 0 and rows % chunk == 0 and chunk % GLA_SUB == 0
    kspec = pl.BlockSpec((n_seq, rows, heads * dk), lambda b, s: (b + b0 // n_seq, s, 0))
    vspec = pl.BlockSpec((n_seq, rows, heads * dv), lambda b, s: (b + b0 // n_seq, s, 0))
    out_spec = pl.BlockSpec((n_seq, rows, heads * dv), lambda b, s: (b, s, 0))
    return pl.pallas_call(
        functools.partial(_gla_kernel, chunk=chunk, sub=GLA_SUB, n_chunk=rows // chunk,
                          n_seq=n_seq, heads=heads, dk=dk, dv=dv),
        grid=(n // n_seq, t // rows),
        in_specs=[kspec, kspec, vspec, kspec, vspec, pl.BlockSpec((1, dv), lambda b, s: (0, 0))],
        out_specs=[out_spec, pl.BlockSpec((n_seq, heads, dv, dk), lambda b, s: (b, 0, 0, 0))],
        out_shape=[jax.ShapeDtypeStruct((n, t, heads * dv), BF16),
                   jax.ShapeDtypeStruct((n, heads, dv, dk), F32)],
        scratch_shapes=[pltpu.VMEM((n_seq, heads, dv, dk), F32), pltpu.VMEM((chunk, dk), F32),
                        pltpu.VMEM((chunk, dk), F32), pltpu.VMEM((chunk, dk), F32),
                        pltpu.VMEM((chunk, chunk), F32)],
        compiler_params=_cparams("parallel", "arbitrary"),
        name="gla_prompt",
    )(qb, kb, vb, la, gb, gla_norm_g)


def _gla_sample_kernel(qt_ref, kt_ref, lat_ref, v_ref, gb_ref, gn_ref, s0_ref, o_ref, sfin_ref,
                       *, heads, dk, dv):
    for h in range(heads):
        vc = slice(h * dv, (h + 1) * dv)
        s = s0_ref[0, 0, h]
        qt = qt_ref[0, h] * dk ** -0.5
        kt = kt_ref[0, h]
        at = jnp.exp(lat_ref[0, h])
        v = v_ref[0, :, vc].astype(F32)
        outs = []
        for t in range(v.shape[0]):
            s = at[:, t:t + 1] * s + kt[:, t:t + 1] * v[t:t + 1, :]
            outs.append(jnp.sum(qt[:, t:t + 1] * s, axis=0, keepdims=True))
        o = jnp.concatenate(outs, axis=0)
        gb = gb_ref[0, :, vc].astype(F32)
        o_ref[0, :, vc] = (_rms(o) * gn_ref[...] * (gb * jax.nn.sigmoid(gb))).astype(BF16)
        sfin_ref[0, h] = s


def _gla_sample(qt, kt, lat, vb, gb, gla_norm_g, state, layer, *, dk, dv):
    n, heads, _, t = qt.shape
    tspec = pl.BlockSpec((1, heads, dk, t), lambda b: (b, 0, 0, 0))
    vspec = pl.BlockSpec((1, t, heads * dv), lambda b: (b, 0, 0))
    return pl.pallas_call(
        functools.partial(_gla_sample_kernel, heads=heads, dk=dk, dv=dv),
        grid=(n,),
        in_specs=[tspec, tspec, tspec, vspec, vspec, pl.BlockSpec((1, dv), lambda b: (0, 0)),
                  pl.BlockSpec((1, 1, heads, dk, dv), lambda b: (layer, b, 0, 0, 0))],
        out_specs=[vspec, pl.BlockSpec((1, heads, dk, dv), lambda b: (b, 0, 0, 0))],
        out_shape=[jax.ShapeDtypeStruct((n, t, heads * dv), BF16),
                   jax.ShapeDtypeStruct((n, heads, dk, dv), F32)],
        compiler_params=_cparams("parallel"),
        name="gla_sample",
    )(qt, kt, lat, vb, gb, gla_norm_g, state)


def _merge_kernel(x_ref, oa_ref, ob_ref, g1_ref, wa_ref, wb_ref, wmg_ref, wo_ref, g2_ref, wrh_ref,
                  wrl_ref, br_ref, x2_ref, h2_ref, idx_ref, gate_ref, *, d, n_exp):
    x = x_ref[...]
    hb = (_rms(x) * g1_ref[...]).astype(BF16)
    y_a = _dot(oa_ref[...], wa_ref[...])
    y_b = _dot(ob_ref[...], wb_ref[...])
    mixed = (jax.nn.sigmoid(_dot(hb, wmg_ref[:, :d])) * y_a
             + jax.nn.sigmoid(_dot(hb, wmg_ref[:, d:])) * y_b)
    x2 = x + _dot(mixed.astype(BF16), wo_ref[...])
    x2_ref[...] = x2
    h2 = _rms(x2) * g2_ref[...]
    h2_hi = h2.astype(BF16)
    h2_ref[...] = h2_hi
    h2_lo = (h2 - h2_hi.astype(F32)).astype(BF16)
    logits = (_dot(h2_hi, wrh_ref[...]) + _dot(h2_hi, wrl_ref[...]) + _dot(h2_lo, wrh_ref[...])
              + br_ref[...])
    col = lax.broadcasted_iota(jnp.int32, logits.shape, 1)
    _, picks = _top_mask(logits, col, MOE_TOP_K, n_exp, axis=1)
    top_val = jnp.concatenate([p[1] for p in picks], axis=1)
    e = jnp.exp(top_val - top_val[:, :1])
    idx_ref[...] = jnp.concatenate([p[0] for p in picks], axis=1)
    gate_ref[...] = e / jnp.sum(e, axis=1, keepdims=True)


def _merge(x, row0, o_a, o_b, norm1_g, w_a, w_b, w_mg, w_o, norm2_g, w_router, b_router):
    n_tok, d = o_a.shape[0], x.shape[1]
    n_exp = w_router.shape[1]
    w_router_hi = w_router.astype(BF16)
    w_router_lo = (w_router - w_router_hi.astype(F32)).astype(BF16)
    tm = min(PROJ_ROWS, n_tok)
    assert n_tok % tm == 0 and row0 % tm == 0
    full = lambda a: pl.BlockSpec(a.shape, lambda i: (0,) * a.ndim)
    rows = lambda w: pl.BlockSpec((tm, w), lambda i: (i, 0))
    return pl.pallas_call(
        functools.partial(_merge_kernel, d=d, n_exp=n_exp),
        grid=(n_tok // tm,),
        in_specs=[pl.BlockSpec((tm, d), lambda i: (i + row0 // tm, 0)),
                  rows(o_a.shape[1]), rows(o_b.shape[1]), full(norm1_g), full(w_a),
                  full(w_b), full(w_mg), full(w_o), full(norm2_g), full(w_router_hi),
                  full(w_router_lo), full(b_router)],
        out_specs=[rows(d), rows(d), rows(MOE_TOP_K), rows(MOE_TOP_K)],
        out_shape=[jax.ShapeDtypeStruct((n_tok, d), F32), jax.ShapeDtypeStruct((n_tok, d), BF16),
                   jax.ShapeDtypeStruct((n_tok, MOE_TOP_K), jnp.int32),
                   jax.ShapeDtypeStruct((n_tok, MOE_TOP_K), F32)],
        compiler_params=_cparams("parallel"),
        name="merge",
    )(x, o_a, o_b, norm1_g, w_a, w_b, w_mg, w_o, norm2_g, w_router_hi, w_router_lo, b_router)


def _expert_kernel(te_ref, nt_ref, x_ref, wup_ref, bg_ref, bl_ref, wdn_ref, bdn_ref, y_ref,
                   wg_sc, wl_sc, wd_sc, t_sc, *, slab):
    i = pl.program_id(0)

    @pl.when((i == 0) | (te_ref[i] != te_ref[jnp.maximum(i - 1, 0)]))
    def _():
        for c in range(0, wup_ref.shape[2], slab):
            rows = slice(c // 2, (c + slab) // 2)
            for j in range(t_sc.shape[0]):
                lanes = slice(j * LANES, (j + 1) * LANES)
                t_sc[j] = wup_ref[0, lanes, c:c + slab].T
                wg_sc[rows, lanes] = t_sc[j, pl.ds(0, slab // 2, stride=2), :].astype(BF16)
                wl_sc[rows, lanes] = t_sc[j, pl.ds(1, slab // 2, stride=2), :].astype(BF16)
        wd_sc[...] = wdn_ref[0].astype(BF16)

    @pl.when(i < nt_ref[0])
    def _():
        x = x_ref[...]
        x_glu = jnp.minimum(_dot_nt(x, wg_sc[...]) + bg_ref[0], SWIGLU_LIMIT)
        x_lin = jnp.clip(_dot_nt(x, wl_sc[...]) + bl_ref[0], -SWIGLU_LIMIT, SWIGLU_LIMIT)
        a = x_glu * jax.nn.sigmoid(SWIGLU_ALPHA * x_glu) * (x_lin + 1.0)
        y_ref[...] = (_dot(a.astype(BF16), wd_sc[...]) + bdn_ref[0]).astype(y_ref.dtype)

    @pl.when(i >= nt_ref[0])
    def _():
        y_ref[...] = jnp.zeros(y_ref.shape, y_ref.dtype)


def _experts(buf, tile_e, n_used, w_up, b_glu, b_lin, w_dn, b_dn, *, tile_rows):
    r, d = buf.shape
    d_ff = w_dn.shape[1]
    slab = min(2 * LANES, 2 * d_ff)
    by_expert = lambda *blk: pl.BlockSpec((1,) + blk, lambda i, te, nt: (te[i], 0, 0))
    return pl.pallas_call(
        functools.partial(_expert_kernel, slab=slab),
        grid_spec=pltpu.PrefetchScalarGridSpec(
            num_scalar_prefetch=2, grid=(r // tile_rows,),
            in_specs=[pl.BlockSpec((tile_rows, d), lambda i, te, nt: (i, 0)),
                      by_expert(d, 2 * d_ff), by_expert(1, d_ff), by_expert(1, d_ff),
                      by_expert(d_ff, d), by_expert(1, d)],
            out_specs=pl.BlockSpec((tile_rows, d), lambda i, te, nt: (i, 0)),
            scratch_shapes=[pltpu.VMEM((d_ff, d), BF16), pltpu.VMEM((d_ff, d), BF16),
                            pltpu.VMEM((d_ff, d), BF16), pltpu.VMEM((d // LANES, slab, LANES), F32)]),
        out_shape=jax.ShapeDtypeStruct((r, d), BF16),
        compiler_params=_cparams("arbitrary"),
        name="experts",
    )(tile_e, n_used, buf, w_up, b_glu, b_lin, w_dn, b_dn)


def _rank_kernel(idx_ref, rank_ref, cnt_ref, seen_sc):
    @pl.when(pl.program_id(0) == 0)
    def _():
        seen_sc[...] = jnp.zeros(seen_sc.shape, F32)

    idx = idx_ref[...]
    tr, top_k = idx.shape
    lane = lax.broadcasted_iota(jnp.int32, (tr, LANES), 1)
    hits = [idx[:, k:k + 1] == lane for k in range(top_k)]
    routed = hits[0].astype(F32)
    for h in hits[1:]:
        routed = routed + h.astype(F32)
    earlier = (lax.broadcasted_iota(jnp.int32, (tr, tr), 1)
               < lax.broadcasted_iota(jnp.int32, (tr, tr), 0)).astype(BF16)
    before = _dot(earlier, routed.astype(BF16)) + seen_sc[...]
    rank_ref[...] = jnp.concatenate(
        [jnp.sum(jnp.where(h, before, 0.0), axis=1, keepdims=True) for h in hits],
        axis=1).astype(jnp.int32)
    seen_sc[...] = seen_sc[...] + jnp.sum(routed, axis=0, keepdims=True)
    cnt_ref[...] = seen_sc[...]


def _rank(top_idx):
    n_tok, top_k = top_idx.shape
    tr = min(ROUTE_ROWS, n_tok)
    assert n_tok % tr == 0
    return pl.pallas_call(
        _rank_kernel,
        grid=(n_tok // tr,),
        in_specs=[pl.BlockSpec((tr, top_k), lambda i: (i, 0))],
        out_specs=[pl.BlockSpec((tr, top_k), lambda i: (i, 0)),
                   pl.BlockSpec((1, LANES), lambda i: (0, 0))],
        out_shape=[jax.ShapeDtypeStruct((n_tok, top_k), jnp.int32),
                   jax.ShapeDtypeStruct((1, LANES), F32)],
        scratch_shapes=[pltpu.VMEM((1, LANES), F32)],
        compiler_params=_cparams("arbitrary"),
        name="moe_rank",
    )(top_idx)


def _combine_kernel(x_ref, y_ref, g_ref, o_ref):
    acc = x_ref[...]
    g = g_ref[...]
    for k in range(y_ref.shape[0]):
        acc = acc + g[:, k:k + 1] * y_ref[k].astype(F32)
    o_ref[...] = acc


def _combine(x2, y_kt, gate):
    top_k, n_tok, d = y_kt.shape
    tm = min(ROUTE_ROWS, n_tok)
    assert n_tok % tm == 0
    return pl.pallas_call(
        _combine_kernel,
        grid=(n_tok // tm,),
        in_specs=[pl.BlockSpec((tm, d), lambda i: (i, 0)),
                  pl.BlockSpec((top_k, tm, d), lambda i: (0, i, 0)),
                  pl.BlockSpec((tm, top_k), lambda i: (i, 0))],
        out_specs=pl.BlockSpec((tm, d), lambda i: (i, 0)),
        out_shape=jax.ShapeDtypeStruct((n_tok, d), F32),
        compiler_params=_cparams("parallel"),
        name="moe_combine",
    )(x2, y_kt, gate)


def _moe(x2, h2, top_idx, gate, expert_weights, tile_rows):
    n_tok, d = h2.shape
    n_exp = expert_weights[0].shape[0]
    assert n_exp <= LANES
    n_assign = n_tok * MOE_TOP_K
    rank, cnt = _rank(top_idx)
    counts = cnt[0, :n_exp].astype(jnp.int32)
    padded = (counts + tile_rows - 1) // tile_rows * tile_rows
    pend = jnp.cumsum(padded)
    experts = jnp.arange(n_exp, dtype=jnp.int32)
    first_row = jnp.sum(jnp.where(top_idx[:, :, None] == experts, pend - padded, 0), axis=-1)
    dest = (first_row + rank).reshape(-1)
    n_tiles = -(-n_assign // tile_rows) + n_exp
    tile_row0 = jnp.arange(n_tiles, dtype=jnp.int32) * tile_rows
    tile_e = jnp.minimum(jnp.sum((pend[None, :] <= tile_row0[:, None]).astype(jnp.int32), axis=1),
                         n_exp - 1)
    n_used = (pend[-1:] // tile_rows).astype(jnp.int32)
    src = jnp.zeros((n_tiles * tile_rows,), jnp.int32).at[dest].set(
        jnp.arange(n_assign, dtype=jnp.int32) // MOE_TOP_K,
        unique_indices=True, mode='promise_in_bounds')
    y = _experts(h2[src], tile_e, n_used, *expert_weights, tile_rows=tile_rows)
    y_kt = y[dest.reshape(n_tok, MOE_TOP_K).T]
    return _combine(x2, y_kt, gate)


def _rope_tables_t(pos, hd):
    half = hd // 2
    inv = ROPE_THETA ** (-jnp.arange(half, dtype=F32) / half)
    ang = inv[:, None] * pos.astype(F32)[None, :]
    cos, sin = jnp.cos(ang), jnp.sin(ang)
    return jnp.concatenate([cos, cos], axis=0), jnp.concatenate([-sin, sin], axis=0)


def kernel(x_prompt, x_sample, cache_k, cache_v, state_gla, page_table, norm1_g, w_in, q_norm_g,
           k_norm_g, w_gate_up, b_gate_up, gla_norm_g, w_branch_a, w_branch_b, w_merge_gate, w_out,
           norm2_g, w_router, b_router, w_up, b_up, w_down, b_down):
    depth = norm1_g.shape[0]
    assert depth == 1, "sample/prompt streams are only chained through one layer here"
    layer = 0
    n_p, t_p, d = x_prompt.shape
    n_s, t_s, _ = x_sample.shape
    heads, page, hd = cache_k.shape[2:]
    b_heads, dk, dv = state_gla.shape[2:]
    bk, bv = b_heads * dk, b_heads * dv
    aqk = heads * hd
    rank = w_gate_up.shape[1]
    past_len = page_table.shape[1] * page
    ppb = MOBA_BLOCK // page
    n_full = past_len // MOBA_BLOCK
    assert past_len % MOBA_BLOCK == 0 and n_full >= MOBA_TOPK and rank <= LANES
    assert w_in.shape[2] == 3 * aqk + 2 * bk + 2 * bv + rank

    w_qkv_t = w_in[layer][:, :3 * aqk].T.astype(BF16)
    w_rest = jnp.pad(w_in[layer][:, 3 * aqk:], ((0, 0), (0, LANES - rank))).astype(BF16)
    w_gu_p = jnp.pad(w_gate_up[layer], ((0, LANES - rank), (0, 0))).astype(BF16)
    g1 = norm1_g[layer][None, :]
    g2 = norm2_g[layer][None, :]
    qg = q_norm_g[layer][:, None]
    kg = k_norm_g[layer][:, None]
    bgu = b_gate_up[layer][None, :]
    gn = gla_norm_g[layer][None, :]
    w_a = w_branch_a[layer].astype(BF16)
    w_b = w_branch_b[layer].astype(BF16)
    w_mg = w_merge_gate[layer].astype(BF16)
    w_o = w_out[layer].astype(BF16)
    w_r = w_router[layer]
    b_r = b_router[layer][None, :]
    expert_weights = (w_up[layer], b_up[layer][:, None, 0::2], b_up[layer][:, None, 1::2],
                      w_down[layer], b_down[layer][:, None, :])
    proj = functools.partial(_project, norm1_g=g1, w_qkv_t=w_qkv_t, w_rest=w_rest, q_norm_g=qg,
                             k_norm_g=kg, w_gu_p=w_gu_p, b_gu=bgu, heads=heads, hd=hd, bk=bk, bv=bv)
    merge = functools.partial(_merge, norm1_g=g1, w_a=w_a, w_b=w_b, w_mg=w_mg, w_o=w_o,
                              norm2_g=g2, w_router=w_r, b_router=b_r)

    cos_p, sin_p = _rope_tables_t(jnp.arange(t_p), hd)
    qt, kt, vt, qb, kb, vb, la, gb = proj(x_prompt, cos_p, sin_p)
    n_part = n_p // PROMPT_SPLITS if n_p % PROMPT_SPLITS == 0 else n_p
    x_rows = x_prompt.reshape(n_p * t_p, d)
    y_parts, st_parts = [], []
    for b0 in range(0, n_p, n_part):
        o_a = _moba_prompt(qt, kt, vt, b0, n_part)
        o_b, st = _gla_prompt(qb, kb, vb, la, gb, gn, b0, n_part, heads=b_heads, dk=dk, dv=dv)
        x2, h2, idx, gate = merge(x_rows, b0 * t_p, o_a.reshape(n_part * t_p, -1),
                                  o_b.reshape(n_part * t_p, -1))
        y_parts.append(_moe(x2, h2, idx, gate, expert_weights, MOE_ROWS))
        st_parts.append(st)
    y_prompt = jnp.concatenate(y_parts).reshape(n_p, t_p, d)
    s_prompt = jnp.swapaxes(jnp.concatenate(st_parts), 2, 3)

    n_tok_s = n_s * t_s
    cos_s, sin_s = _rope_tables_t(past_len + jnp.arange(t_s), hd)
    qt_s, kt_s, vt_s, qb_s, kb_s, vb_s, la_s, gb_s = proj(
        x_sample.reshape(1, n_tok_s, d), jnp.tile(cos_s, (1, n_s)), jnp.tile(sin_s, (1, n_s)))
    per_seq = lambda a: a.reshape(heads, hd, n_s, t_s).transpose(2, 0, 3, 1)
    qa_s, ka_s, va_s = per_seq(qt_s), per_seq(kt_s), per_seq(vt_s)
    cache_kt = jnp.swapaxes(cache_k, 3, 4)
    cache_vt = jnp.swapaxes(cache_v, 3, 4)
    kmean = _sample_kmean(cache_kt, layer, page_table, n_full, ppb).transpose(0, 2, 1, 3)
    blk_idx = _sample_pick(qa_s, kmean, MOBA_TOPK)
    lpages = blk_idx[..., None] * ppb + jnp.arange(ppb, dtype=jnp.int32)
    phys = jnp.take_along_axis(page_table[:, None, None, :],
                               lpages.reshape(n_s, heads, t_s, MOBA_TOPK * ppb), axis=-1)
    o_a_s = _sample_attn(qa_s, ka_s, va_s, cache_kt, cache_vt, layer, phys)
    o_a_s = o_a_s.transpose(0, 2, 1, 3).reshape(n_tok_s, aqk).astype(BF16)
    tr = lambda a: a.reshape(n_s, t_s, b_heads, dk).transpose(0, 2, 3, 1)
    o_b_s, s_sample = _gla_sample(tr(qb_s), tr(kb_s), tr(la_s), vb_s.reshape(n_s, t_s, bv),
                                  gb_s.reshape(n_s, t_s, bv), gn, state_gla, layer, dk=dk, dv=dv)
    x2_s, h2_s, idx_s, gate_s = merge(x_sample.reshape(n_tok_s, d), 0, o_a_s,
                                      o_b_s.reshape(n_tok_s, bv))
    y_sample = _moe(x2_s, h2_s, idx_s, gate_s, expert_weights,
                    min(MOE_ROWS, n_tok_s)).reshape(n_s, t_s, d)
    k_prompt = jnp.swapaxes(kt, 2, 3)
    v_prompt = jnp.swapaxes(vt, 2, 3)
    return (y_prompt, y_sample, k_prompt[None], v_prompt[None], ka_s[None], va_s[None],
            s_prompt[None], s_sample[None])
```

```python
import functools

import jax
import jax.numpy as jnp
from jax import lax
from jax.experimental import pallas as pl
from jax.experimental.pallas import tpu as pltpu

F32 = jnp.float32
BF16 = jnp.bfloat16
HIGHEST = lax.Precision.HIGHEST

MOBA_BLOCK = 256
MOBA_TOPK = 3
ROPE_THETA = 10000.0
GLA_GATE_NORM = 16.0
NORM_EPS = 1e-6
MOE_TOP_K = 4
SWIGLU_ALPHA = 1.702
SWIGLU_LIMIT = 7.0

LANES = 128
PAD_ROWS = 16
VMEM_LIMIT_BYTES = 56 * 1024 * 1024

MASK_VALUE = -1e30
LOG2_E = 1.4426950408889634
PROJ_ROWS = 512
GLA_CHUNK = 256
GLA_SUB = 64
GLA_STEP_ROWS = 512
GLA_SEQS = 2
GLA_FAST_MIN_LOG_DECAY = -80.0
MOE_ROWS = 512
ROUTE_ROWS = 1024
PROMPT_SPLITS = 1
MOBA_GROUP = 4
SAMPLE_KMEAN_PAGES = 16


def _cparams(*sem):
    return pltpu.CompilerParams(dimension_semantics=sem, vmem_limit_bytes=VMEM_LIMIT_BYTES)


def _rms(x):
    return x * lax.rsqrt(jnp.mean(x * x, axis=-1, keepdims=True) + NORM_EPS)


def _dot(a, b):
    return jnp.dot(a, b, preferred_element_type=F32)


def _dot_nt(a, b, precision=None):
    return lax.dot_general(a, b, (((1,), (1,)), ((), ())), precision=precision,
                           preferred_element_type=F32)


def _dot_tn(a, b):
    return lax.dot_general(a, b, (((0,), (0,)), ((), ())), preferred_element_type=F32)


def _top_mask(g, pos, n_pick, n_pos, axis):
    sel = jnp.zeros(g.shape, F32)
    picks = []
    for _ in range(n_pick):
        mx = jnp.max(g, axis=axis, keepdims=True)
        cand = jnp.where((g == mx) & (mx > -jnp.inf), pos, n_pos)
        idx = jnp.min(cand, axis=axis, keepdims=True)
        pick = pos == idx
        sel = jnp.where(pick, 1.0, sel)
        g = jnp.where(pick, -jnp.inf, g)
        picks.append((idx, mx))
    return sel, picks


def _proj_kernel(x_ref, g1_ref, wqkv_ref, wrest_ref, qg_ref, kg_ref, wgu_ref, bgu_ref, cos_ref,
                 sin_ref, qt_ref, kt_ref, vt_ref, qb_ref, kb_ref, vb_ref, la_ref, gb_ref,
                 *, heads, hd, bk, bv):
    x = x_ref[0]
    hb = (_rms(x) * g1_ref[...]).astype(BF16)
    cos = cos_ref[...]
    sin = sin_ref[...]
    aqk = heads * hd

    def mm_t(lo, hi):
        return _dot_nt(wqkv_ref[lo:hi, :], hb)

    def norm_rope_t(z, g_ref, out_ref):
        for h in range(heads):
            y = z[h * hd:(h + 1) * hd, :]
            y = y * lax.rsqrt(jnp.mean(y * y, axis=0, keepdims=True) + NORM_EPS) * g_ref[...]
            rot = jnp.concatenate([y[hd // 2:], y[:hd // 2]], axis=0)
            out_ref[0, h] = y * cos + rot * sin

    norm_rope_t(mm_t(0, aqk), qg_ref, qt_ref)
    norm_rope_t(mm_t(aqk, 2 * aqk), kg_ref, kt_ref)
    zv = mm_t(2 * aqk, 3 * aqk)
    for h in range(heads):
        vt_ref[0, h] = zv[h * hd:(h + 1) * hd, :]

    def mm(lo, hi):
        return _dot(hb, wrest_ref[:, lo:hi])

    qb_ref[0] = mm(0, bk)
    kb_ref[0] = mm(bk, 2 * bk)
    vb_ref[0] = mm(2 * bk, 2 * bk + bv).astype(BF16)
    gb_ref[0] = mm(2 * bk + bv, 2 * bk + 2 * bv).astype(BF16)
    lr = mm(2 * bk + 2 * bv, 2 * bk + 2 * bv + LANES)
    u = _dot(lr.astype(BF16), wgu_ref[...]) + bgu_ref[...]
    log_sig = jnp.minimum(u, 0.0) - jnp.log(1.0 + jnp.exp(-jnp.abs(u)))
    la_ref[0] = log_sig / GLA_GATE_NORM


def _project(x, cos_t, sin_t, norm1_g, w_qkv_t, w_rest, q_norm_g, k_norm_g, w_gu_p, b_gu,
             *, heads, hd, bk, bv):
    n, t, d = x.shape
    tm = min(PROJ_ROWS, t)
    assert t % tm == 0
    full = lambda a: pl.BlockSpec(a.shape, lambda b, i: (0,) * a.ndim)
    head_out = pl.BlockSpec((1, heads, hd, tm), lambda b, i: (b, 0, 0, i))
    row_out = lambda w: pl.BlockSpec((1, tm, w), lambda b, i: (b, i, 0))
    table = pl.BlockSpec((hd, tm), lambda b, i: (0, i))
    head_shape = jax.ShapeDtypeStruct((n, heads, hd, t), F32)
    return pl.pallas_call(
        functools.partial(_proj_kernel, heads=heads, hd=hd, bk=bk, bv=bv),
        grid=(n, t // tm),
        in_specs=[pl.BlockSpec((1, tm, d), lambda b, i: (b, i, 0)),
                  full(norm1_g), full(w_qkv_t), full(w_rest), full(q_norm_g), full(k_norm_g),
                  full(w_gu_p), full(b_gu), table, table],
        out_specs=[head_out, head_out, head_out, row_out(bk), row_out(bk), row_out(bv),
                   row_out(bk), row_out(bv)],
        out_shape=[head_shape, head_shape, head_shape,
                   jax.ShapeDtypeStruct((n, t, bk), F32),
                   jax.ShapeDtypeStruct((n, t, bk), F32),
                   jax.ShapeDtypeStruct((n, t, bv), BF16),
                   jax.ShapeDtypeStruct((n, t, bk), F32),
                   jax.ShapeDtypeStruct((n, t, bv), BF16)],
        compiler_params=_cparams("parallel", "parallel"),
        name="proj",
    )(x, norm1_g, w_qkv_t, w_rest, q_norm_g, k_norm_g, w_gu_p, b_gu, cos_t, sin_t)


def _moba_kernel(qt_ref, kt_ref, vt_ref, o_ref, k_sc, vt_sc, kmean_sc, qs_sc, bias_sc, m_sc, acc_sc,
                 sa_sc, sb_sc, ma_sc, mb_sc, *, nb, blk, hd, hp, grp):
    c = pl.program_id(2)
    nq = hp * blk
    vr = hd + PAD_ROWS
    n_groups = nb // grp
    cols = [slice(i * blk, (i + 1) * blk) for i in range(hp)]

    @pl.when(c == 0)
    def _():
        row = lax.broadcasted_iota(jnp.int32, (LANES - hd, blk), 0)
        ones_row = (lax.broadcasted_iota(jnp.int32, (PAD_ROWS, blk), 0) == 0).astype(BF16)

        def stage_kv(j, carry):
            off = pl.multiple_of(j * blk, blk)
            place = (row == j % grp).astype(F32)
            for i in range(hp):
                k_nat = jnp.concatenate([kt_ref[0, i, :, pl.ds(off, blk)], place], axis=0).T
                k_sc[i, pl.ds(off, blk), :] = k_nat.astype(BF16)
                kmean_sc[i, pl.ds(j, 1), :] = jnp.mean(k_nat, axis=0, keepdims=True)
                vt_sc[i * vr:i * vr + hd, pl.ds(off, blk)] = vt_ref[0, i, :, pl.ds(off, blk)].astype(BF16)
                vt_sc[i * vr + hd:(i + 1) * vr, pl.ds(off, blk)] = ones_row
            return carry

        lax.fori_loop(0, nb, stage_kv, 0)

    zero_rows = jnp.zeros((LANES - hd, blk), F32)
    q_pad = [jnp.concatenate([qt_ref[0, i], zero_rows], axis=0) for i in range(hp)]
    gate = jnp.concatenate(
        [jnp.dot(kmean_sc[i], q_pad[i], precision=HIGHEST, preferred_element_type=F32)
         for i in range(hp)], axis=1)
    blk_row = lax.broadcasted_iota(jnp.int32, (nb, nq), 0)
    sel, _ = _top_mask(jnp.where(blk_row < c, gate, -jnp.inf), blk_row, min(MOBA_TOPK, nb - 1),
                       nb, axis=0)
    bias = (sel - 1.0) * -MASK_VALUE
    pad = jnp.zeros((PAD_ROWS - grp, blk), F32)
    for g in range(n_groups):
        for i in range(hp):
            bias_sc[g, i] = jnp.concatenate([bias[g * grp:(g + 1) * grp, cols[i]], pad],
                                            axis=0).astype(BF16)
    for i in range(hp):
        qs_sc[i] = (q_pad[i] * (hd ** -0.5 * LOG2_E)).astype(BF16)

    def scores(off, n_keys):
        return jnp.concatenate([_dot(k_sc[i, pl.ds(off, n_keys), :], qs_sc[i]) for i in range(hp)],
                               axis=1)

    def update(s, m_blk, off, n_keys, first):
        m_new = m_blk if first else jnp.maximum(m_sc[...], m_blk)
        pb = jnp.exp2(s - m_new).astype(BF16)
        if not first:
            alpha = jnp.exp2(m_sc[...] - m_new)
        m_sc[...] = m_new
        for i in range(hp):
            pv = _dot(vt_sc[i * vr:(i + 1) * vr, pl.ds(off, n_keys)], pb[:, cols[i]])
            acc_sc[:, cols[i]] = pv if first else alpha[:, cols[i]] * acc_sc[:, cols[i]] + pv

    own_off = pl.multiple_of(c * blk, blk)
    causal = (lax.broadcasted_iota(jnp.int32, (blk, nq), 0)
              <= lax.broadcasted_iota(jnp.int32, (blk, nq), 1) % blk)
    s_own = jnp.where(causal, scores(own_off, blk), MASK_VALUE)

    n_grp = (c + grp - 1) // grp

    def stage(g, s_buf, mx_buf):
        g = jnp.minimum(g, n_groups - 1)
        for i in range(hp):
            qs_sc[i, hd:hd + PAD_ROWS, :] = bias_sc[g, i]
        s = scores(pl.multiple_of(g * (grp * blk), grp * blk), grp * blk)
        s_buf[...] = s
        mx_buf[...] = jnp.max(s, axis=0, keepdims=True)

    def consume(g, s_buf, mx_buf):
        update(s_buf[...], mx_buf[...], pl.multiple_of(g * (grp * blk), grp * blk), grp * blk, False)

    stage(0, sa_sc, ma_sc)
    update(s_own, jnp.max(s_own, axis=0, keepdims=True), own_off, blk, True)

    def body(u, carry):
        stage(2 * u + 1, sb_sc, mb_sc)
        consume(2 * u, sa_sc, ma_sc)
        stage(2 * u + 2, sa_sc, ma_sc)
        consume(2 * u + 1, sb_sc, mb_sc)
        return carry

    lax.fori_loop(0, n_grp // 2, body, 0)

    @pl.when(n_grp % 2 == 1)
    def _():
        consume(n_grp - 1, sa_sc, ma_sc)

    out_t = acc_sc[:hd, :] / acc_sc[hd:hd + 1, :]
    out_t = jnp.concatenate([out_t[:, cols[i]] for i in range(hp)], axis=0)
    o_ref[0] = out_t.T.astype(BF16)


def _moba_prompt(qt, kt, vt, b0, n):
    _, heads, hd, t = qt.shape
    blk = MOBA_BLOCK
    assert t % blk == 0 and LANES % hd == 0
    hp = LANES // hd
    assert heads % hp == 0
    nb = t // blk
    grp = MOBA_GROUP if nb % MOBA_GROUP == 0 else 1
    nq = hp * blk
    kv_spec = pl.BlockSpec((1, hp, hd, t), lambda b, g, c: (b + b0, g, 0, 0))
    return pl.pallas_call(
        functools.partial(_moba_kernel, nb=nb, blk=blk, hd=hd, hp=hp, grp=grp),
        grid=(n, heads // hp, nb),
        in_specs=[pl.BlockSpec((1, hp, hd, blk), lambda b, g, c: (b + b0, g, 0, c)), kv_spec, kv_spec],
        out_specs=pl.BlockSpec((1, blk, hp * hd), lambda b, g, c: (b, c, g)),
        out_shape=jax.ShapeDtypeStruct((n, t, heads * hd), BF16),
        scratch_shapes=[pltpu.VMEM((hp, t, LANES), BF16),
                        pltpu.VMEM((hp * (hd + PAD_ROWS), t), BF16),
                        pltpu.VMEM((hp, nb, LANES), F32),
                        pltpu.VMEM((hp, LANES, blk), BF16),
                        pltpu.VMEM((nb // grp, hp, PAD_ROWS, blk), BF16),
                        pltpu.VMEM((1, nq), F32),
                        pltpu.VMEM((hd + PAD_ROWS, nq), F32),
                        pltpu.VMEM((grp * blk, nq), F32), pltpu.VMEM((grp * blk, nq), F32),
                        pltpu.VMEM((1, nq), F32), pltpu.VMEM((1, nq), F32)],
        compiler_params=_cparams("parallel", "parallel", "arbitrary"),
        name="moba_prompt",
    )(qt, kt, vt)


def _kmean_pages_kernel(pt_ref, *refs, n_pg, ppb, rows):
    del pt_ref
    out_ref = refs[n_pg]
    for b in range(n_pg // ppb):
        acc = refs[b * ppb][0, 0]
        for j in range(1, ppb):
            acc = acc + refs[b * ppb + j][0, 0]
        out_ref[0, b] = jnp.sum(acc, axis=-1) / rows


def _sample_kmean(cache_kt, layer, page_table, n_full, ppb):
    _, _, heads, hd, page = cache_kt.shape
    n, n_pages = page_table.shape
    n_pg = SAMPLE_KMEAN_PAGES
    assert (n_full * ppb) % n_pg == 0 and n_pg % ppb == 0

    def page_map(b, g, pt, j):
        return (layer, pt[b * n_pages + g * n_pg + j], 0, 0, 0)

    specs = [pl.BlockSpec((1, 1, heads, hd, page), functools.partial(page_map, j=j))
             for j in range(n_pg)]
    return pl.pallas_call(
        functools.partial(_kmean_pages_kernel, n_pg=n_pg, ppb=ppb, rows=float(ppb * page)),
        grid_spec=pltpu.PrefetchScalarGridSpec(
            num_scalar_prefetch=1, grid=(n, n_full * ppb // n_pg), in_specs=specs,
            out_specs=pl.BlockSpec((1, n_pg // ppb, heads, hd), lambda b, g, pt: (b, g, 0, 0))),
        out_shape=jax.ShapeDtypeStruct((n, n_full, heads, hd), F32),
        compiler_params=_cparams("parallel", "arbitrary"),
        name="sample_kmean",
    )(page_table.reshape(-1), *([cache_kt] * n_pg))


def _sample_pick_kernel(q_ref, km_ref, idx_ref, *, heads, n_full, n_sel):
    col = lax.broadcasted_iota(jnp.int32, (q_ref.shape[2], n_full), 1)
    for h in range(heads):
        gate = _dot_nt(q_ref[0, h], km_ref[0, h], precision=HIGHEST)
        _, picks = _top_mask(gate, col, n_sel, n_full, axis=1)
        idx_ref[0, h] = jnp.concatenate([p[0] for p in picks], axis=1)


def _sample_pick(q, kmean, n_sel):
    n, heads, t, hd = q.shape
    n_full = kmean.shape[2]
    return pl.pallas_call(
        functools.partial(_sample_pick_kernel, heads=heads, n_full=n_full, n_sel=n_sel),
        grid=(n,),
        in_specs=[pl.BlockSpec((1, heads, t, hd), lambda b: (b, 0, 0, 0)),
                  pl.BlockSpec((1, heads, n_full, hd), lambda b: (b, 0, 0, 0))],
        out_specs=pl.BlockSpec((1, heads, t, n_sel), lambda b: (b, 0, 0, 0)),
        out_shape=jax.ShapeDtypeStruct((n, heads, t, n_sel), jnp.int32),
        compiler_params=_cparams("parallel"),
        name="sample_pick",
    )(q, kmean)


def _sample_attn_kernel(ph_ref, q_ref, kn_ref, vn_ref, *refs, n_pg):
    del ph_ref
    tq, hd = q_ref.shape[2], q_ref.shape[3]
    k_refs, v_refs, o_ref = refs[:tq * n_pg], refs[tq * n_pg:2 * tq * n_pg], refs[2 * tq * n_pg]
    qs = (q_ref[0, 0] * hd ** -0.5).astype(BF16)
    own_ok = (lax.broadcasted_iota(jnp.int32, (tq, tq), 1)
              <= lax.broadcasted_iota(jnp.int32, (tq, tq), 0))
    s_own = jnp.where(own_ok, _dot_nt(qs, kn_ref[0, 0].astype(BF16)), MASK_VALUE)
    vn = vn_ref[0, 0].astype(BF16)
    for t in range(tq):
        pages = slice(t * n_pg, (t + 1) * n_pg)
        kt = jnp.concatenate([r[0, 0, 0] for r in k_refs[pages]], axis=1).astype(BF16)
        vt = jnp.concatenate([r[0, 0, 0] for r in v_refs[pages]], axis=1).astype(BF16)
        s_sel = _dot(qs, kt)
        m = jnp.maximum(jnp.max(s_sel, axis=1, keepdims=True), jnp.max(s_own, axis=1, keepdims=True))
        p_sel = jnp.exp(s_sel - m)
        p_own = jnp.exp(s_own - m)
        denom = jnp.sum(p_sel, axis=1, keepdims=True) + jnp.sum(p_own, axis=1, keepdims=True)
        o = (_dot_nt(p_sel.astype(BF16), vt) + _dot(p_own.astype(BF16), vn)) / denom
        o_ref[0, 0, t:t + 1, :] = o[t:t + 1]


def _sample_attn(q, k_new, v_new, cache_kt, cache_vt, layer, phys):
    n, heads, t, hd = q.shape
    page = cache_kt.shape[4]
    n_pg = phys.shape[-1]

    def page_map(b, h, ph, j):
        return (layer, ph[(b * heads + h) * (t * n_pg) + j], h, 0, 0)

    pages = [pl.BlockSpec((1, 1, 1, hd, page), functools.partial(page_map, j=j))
             for j in range(t * n_pg)]
    own = pl.BlockSpec((1, 1, t, hd), lambda b, h, ph: (b, h, 0, 0))
    return pl.pallas_call(
        functools.partial(_sample_attn_kernel, n_pg=n_pg),
        grid_spec=pltpu.PrefetchScalarGridSpec(
            num_scalar_prefetch=1, grid=(n, heads),
            in_specs=[own, own, own] + pages * 2, out_specs=own),
        out_shape=jax.ShapeDtypeStruct((n, heads, t, hd), F32),
        compiler_params=_cparams("parallel", "parallel"),
        name="sample_attn",
    )(phys.reshape(-1), q, k_new, v_new, *([cache_kt] * (t * n_pg)), *([cache_vt] * (t * n_pg)))


def _gla_kernel(q_ref, k_ref, v_ref, la_ref, gb_ref, gn_ref, o_ref, sfin_ref, st_sc, qk_sc, b_sc,
                la_sc, attn_sc, *, chunk, sub, n_chunk, n_seq, heads, dk, dv):
    step = pl.program_id(1)

    @pl.when(step == 0)
    def _():
        st_sc[...] = jnp.zeros(st_sc.shape, F32)

    r_i = lax.broadcasted_iota(jnp.int32, (chunk, chunk), 0)
    c_i = lax.broadcasted_iota(jnp.int32, (chunk, chunk), 1)
    tri = r_i >= c_i
    tri_b = tri.astype(BF16)
    row_k = lax.broadcasted_iota(jnp.int32, (chunk, dk), 0)
    q_scale = dk ** -0.5

    def intra_fast(q, k, b):
        rows = []
        for i in range(chunk // sub):
            lo, hi = i * sub, (i + 1) * sub
            ref = b[lo - 1:lo, :] if i > 0 else jnp.zeros((1, dk), F32)
            qt = (q[lo:hi] * jnp.exp(b[lo:hi] - ref)).astype(BF16)
            kt = (k * jnp.exp(jnp.where(row_k < hi, ref - b, -jnp.inf))).astype(BF16)
            rows.append(_dot_nt(qt, kt))
        return jnp.where(tri, jnp.concatenate(rows, axis=0), 0.0)

    def intra_exact(q, k, b, la):
        qk_sc[...] = q
        b_sc[...] = b
        la_sc[...] = la
        col = lax.broadcasted_iota(jnp.int32, (1, chunk), 1)

        def row(t, carry):
            q_t = qk_sc[pl.ds(t, 1), :]
            la_t = la_sc[pl.ds(t, 1), :]
            before = b_sc[pl.ds(t, 1), :] - la_t
            kt = (k * jnp.exp(jnp.where(row_k < t, before - b, -jnp.inf))).astype(BF16)
            a_row = _dot_nt((q_t * jnp.exp(la_t)).astype(BF16), kt)
            k_t = jnp.sum(jnp.where(row_k == t, k, 0.0), axis=0, keepdims=True)
            diag = jnp.sum(q_t * k_t, axis=1, keepdims=True)
            attn_sc[pl.ds(t, 1), :] = jnp.where(col == t, diag, a_row)
            return carry

        lax.fori_loop(0, chunk, row, 0)
        return attn_sc[...]

    def one_chunk(ci, carry, exact):
        off = pl.multiple_of(ci * chunk, chunk)
        rows_ds = pl.ds(off, chunk)
        for s in range(n_seq):
            la = la_ref[s, rows_ds, :]
            la_hi = la.astype(BF16)
            la_mid = (la - la_hi.astype(F32)).astype(BF16)
            la_lo = (la - la_hi.astype(F32) - la_mid.astype(F32)).astype(BF16)
            b_all = _dot(tri_b, la_hi) + _dot(tri_b, la_mid) + _dot(tri_b, la_lo)
            for h in range(heads):
                kc = slice(h * dk, (h + 1) * dk)
                vc = slice(h * dv, (h + 1) * dv)
                q = q_ref[s, rows_ds, kc] * q_scale
                k = k_ref[s, rows_ds, kc]
                v = v_ref[s, rows_ds, vc]
                b = b_all[:, kc]
                st = st_sc[s, h]
                o = _dot_nt((q * jnp.exp(b)).astype(BF16), st.astype(BF16))
                attn = intra_exact(q, k, b, la[:, kc]) if exact else intra_fast(q, k, b)
                o = o + _dot(attn.astype(BF16), v)
                b_last = b[chunk - 1:chunk, :]
                k_dec = (k * jnp.exp(b_last - b)).astype(BF16)
                st_sc[s, h] = st * jnp.exp(b_last) + _dot_tn(v, k_dec)
                gb = gb_ref[s, rows_ds, vc].astype(F32)
                o_ref[s, rows_ds, vc] = (_rms(o) * gn_ref[...] * (gb * jax.nn.sigmoid(gb))).astype(BF16)
        return carry

    safe = jnp.min(la_ref[...]) * sub > GLA_FAST_MIN_LOG_DECAY

    @pl.when(safe)
    def _():
        lax.fori_loop(0, n_chunk, functools.partial(one_chunk, exact=False), 0)

    @pl.when(jnp.logical_not(safe))
    def _():
        lax.fori_loop(0, n_chunk, functools.partial(one_chunk, exact=True), 0)

    @pl.when(step == pl.num_programs(1) - 1)
    def _():
        sfin_ref[...] = st_sc[...]


def _gla_prompt(qb, kb, vb, la, gb, gla_norm_g, b0, n, *, heads, dk, dv):
    _, t, _ = qb.shape
    rows = min(GLA_STEP_ROWS, t)
    chunk = min(GLA_CHUNK, rows)
    n_seq = GLA_SEQS if n % GLA_SEQS == 0 and b0 % GLA_SEQS == 0 else 1
    assert t % rows == 0 and rows % chunk == 0 and chunk % GLA_SUB == 0
    kspec = pl.BlockSpec((n_seq, rows, heads * dk), lambda b, s: (b + b0 // n_seq, s, 0))
    vspec = pl.BlockSpec((n_seq, rows, heads * dv), lambda b, s: (b + b0 // n_seq, s, 0))
    out_spec = pl.BlockSpec((n_seq, rows, heads * dv), lambda b, s: (b, s, 0))
    return pl.pallas_call(
        functools.partial(_gla_kernel, chunk=chunk, sub=GLA_SUB, n_chunk=rows // chunk,
                          n_seq=n_seq, heads=heads, dk=dk, dv=dv),
        grid=(n // n_seq, t // rows),
        in_specs=[kspec, kspec, vspec, kspec, vspec, pl.BlockSpec((1, dv), lambda b, s: (0, 0))],
        out_specs=[out_spec, pl.BlockSpec((n_seq, heads, dv, dk), lambda b, s: (b, 0, 0, 0))],
        out_shape=[jax.ShapeDtypeStruct((n, t, heads * dv), BF16),
                   jax.ShapeDtypeStruct((n, heads, dv, dk), F32)],
        scratch_shapes=[pltpu.VMEM((n_seq, heads, dv, dk), F32), pltpu.VMEM((chunk, dk), F32),
                        pltpu.VMEM((chunk, dk), F32), pltpu.VMEM((chunk, dk), F32),
                        pltpu.VMEM((chunk, chunk), F32)],
        compiler_params=_cparams("parallel", "arbitrary"),
        name="gla_prompt",
    )(qb, kb, vb, la, gb, gla_norm_g)


def _gla_sample_kernel(qt_ref, kt_ref, lat_ref, v_ref, gb_ref, gn_ref, s0_ref, o_ref, sfin_ref,
                       *, heads, dk, dv):
    for h in range(heads):
        vc = slice(h * dv, (h + 1) * dv)
        s = s0_ref[0, 0, h]
        qt = qt_ref[0, h] * dk ** -0.5
        kt = kt_ref[0, h]
        at = jnp.exp(lat_ref[0, h])
        v = v_ref[0, :, vc].astype(F32)
        outs = []
        for t in range(v.shape[0]):
            s = at[:, t:t + 1] * s + kt[:, t:t + 1] * v[t:t + 1, :]
            outs.append(jnp.sum(qt[:, t:t + 1] * s, axis=0, keepdims=True))
        o = jnp.concatenate(outs, axis=0)
        gb = gb_ref[0, :, vc].astype(F32)
        o_ref[0, :, vc] = (_rms(o) * gn_ref[...] * (gb * jax.nn.sigmoid(gb))).astype(BF16)
        sfin_ref[0, h] = s


def _gla_sample(qt, kt, lat, vb, gb, gla_norm_g, state, layer, *, dk, dv):
    n, heads, _, t = qt.shape
    tspec = pl.BlockSpec((1, heads, dk, t), lambda b: (b, 0, 0, 0))
    vspec = pl.BlockSpec((1, t, heads * dv), lambda b: (b, 0, 0))
    return pl.pallas_call(
        functools.partial(_gla_sample_kernel, heads=heads, dk=dk, dv=dv),
        grid=(n,),
        in_specs=[tspec, tspec, tspec, vspec, vspec, pl.BlockSpec((1, dv), lambda b: (0, 0)),
                  pl.BlockSpec((1, 1, heads, dk, dv), lambda b: (layer, b, 0, 0, 0))],
        out_specs=[vspec, pl.BlockSpec((1, heads, dk, dv), lambda b: (b, 0, 0, 0))],
        out_shape=[jax.ShapeDtypeStruct((n, t, heads * dv), BF16),
                   jax.ShapeDtypeStruct((n, heads, dk, dv), F32)],
        compiler_params=_cparams("parallel"),
        name="gla_sample",
    )(qt, kt, lat, vb, gb, gla_norm_g, state)


def _merge_kernel(x_ref, oa_ref, ob_ref, g1_ref, wa_ref, wb_ref, wmg_ref, wo_ref, g2_ref, wrh_ref,
                  wrl_ref, br_ref, x2_ref, h2_ref, idx_ref, gate_ref, *, d, n_exp):
    x = x_ref[...]
    hb = (_rms(x) * g1_ref[...]).astype(BF16)
    y_a = _dot(oa_ref[...], wa_ref[...])
    y_b = _dot(ob_ref[...], wb_ref[...])
    mixed = (jax.nn.sigmoid(_dot(hb, wmg_ref[:, :d])) * y_a
             + jax.nn.sigmoid(_dot(hb, wmg_ref[:, d:])) * y_b)
    x2 = x + _dot(mixed.astype(BF16), wo_ref[...])
    x2_ref[...] = x2
    h2 = _rms(x2) * g2_ref[...]
    h2_hi = h2.astype(BF16)
    h2_ref[...] = h2_hi
    h2_lo = (h2 - h2_hi.astype(F32)).astype(BF16)
    logits = (_dot(h2_hi, wrh_ref[...]) + _dot(h2_hi, wrl_ref[...]) + _dot(h2_lo, wrh_ref[...])
              + br_ref[...])
    col = lax.broadcasted_iota(jnp.int32, logits.shape, 1)
    _, picks = _top_mask(logits, col, MOE_TOP_K, n_exp, axis=1)
    top_val = jnp.concatenate([p[1] for p in picks], axis=1)
    e = jnp.exp(top_val - top_val[:, :1])
    idx_ref[...] = jnp.concatenate([p[0] for p in picks], axis=1)
    gate_ref[...] = e / jnp.sum(e, axis=1, keepdims=True)


def _merge(x, row0, o_a, o_b, norm1_g, w_a, w_b, w_mg, w_o, norm2_g, w_router, b_router):
    n_tok, d = o_a.shape[0], x.shape[1]
    n_exp = w_router.shape[1]
    w_router_hi = w_router.astype(BF16)
    w_router_lo = (w_router - w_router_hi.astype(F32)).astype(BF16)
    tm = min(PROJ_ROWS, n_tok)
    assert n_tok % tm == 0 and row0 % tm == 0
    full = lambda a: pl.BlockSpec(a.shape, lambda i: (0,) * a.ndim)
    rows = lambda w: pl.BlockSpec((tm, w), lambda i: (i, 0))
    return pl.pallas_call(
        functools.partial(_merge_kernel, d=d, n_exp=n_exp),
        grid=(n_tok // tm,),
        in_specs=[pl.BlockSpec((tm, d), lambda i: (i + row0 // tm, 0)),
                  rows(o_a.shape[1]), rows(o_b.shape[1]), full(norm1_g), full(w_a),
                  full(w_b), full(w_mg), full(w_o), full(norm2_g), full(w_router_hi),
                  full(w_router_lo), full(b_router)],
        out_specs=[rows(d), rows(d), rows(MOE_TOP_K), rows(MOE_TOP_K)],
        out_shape=[jax.ShapeDtypeStruct((n_tok, d), F32), jax.ShapeDtypeStruct((n_tok, d), BF16),
                   jax.ShapeDtypeStruct((n_tok, MOE_TOP_K), jnp.int32),
                   jax.ShapeDtypeStruct((n_tok, MOE_TOP_K), F32)],
        compiler_params=_cparams("parallel"),
        name="merge",
    )(x, o_a, o_b, norm1_g, w_a, w_b, w_mg, w_o, norm2_g, w_router_hi, w_router_lo, b_router)


def _expert_kernel(te_ref, nt_ref, x_ref, wup_ref, bg_ref, bl_ref, wdn_ref, bdn_ref, y_ref,
                   wg_sc, wl_sc, wd_sc, t_sc, *, slab):
    i = pl.program_id(0)

    @pl.when((i == 0) | (te_ref[i] != te_ref[jnp.maximum(i - 1, 0)]))
    def _():
        for c in range(0, wup_ref.shape[2], slab):
            rows = slice(c // 2, (c + slab) // 2)
            for j in range(t_sc.shape[0]):
                lanes = slice(j * LANES, (j + 1) * LANES)
                t_sc[j] = wup_ref[0, lanes, c:c + slab].T
                wg_sc[rows, lanes] = t_sc[j, pl.ds(0, slab // 2, stride=2), :].astype(BF16)
                wl_sc[rows, lanes] = t_sc[j, pl.ds(1, slab // 2, stride=2), :].astype(BF16)
        wd_sc[...] = wdn_ref[0].astype(BF16)

    @pl.when(i < nt_ref[0])
    def _():
        x = x_ref[...]
        x_glu = jnp.minimum(_dot_nt(x, wg_sc[...]) + bg_ref[0], SWIGLU_LIMIT)
        x_lin = jnp.clip(_dot_nt(x, wl_sc[...]) + bl_ref[0], -SWIGLU_LIMIT, SWIGLU_LIMIT)
        a = x_glu * jax.nn.sigmoid(SWIGLU_ALPHA * x_glu) * (x_lin + 1.0)
        y_ref[...] = (_dot(a.astype(BF16), wd_sc[...]) + bdn_ref[0]).astype(y_ref.dtype)

    @pl.when(i >= nt_ref[0])
    def _():
        y_ref[...] = jnp.zeros(y_ref.shape, y_ref.dtype)


def _experts(buf, tile_e, n_used, w_up, b_glu, b_lin, w_dn, b_dn, *, tile_rows):
    r, d = buf.shape
    d_ff = w_dn.shape[1]
    slab = min(2 * LANES, 2 * d_ff)
    by_expert = lambda *blk: pl.BlockSpec((1,) + blk, lambda i, te, nt: (te[i], 0, 0))
    return pl.pallas_call(
        functools.partial(_expert_kernel, slab=slab),
        grid_spec=pltpu.PrefetchScalarGridSpec(
            num_scalar_prefetch=2, grid=(r // tile_rows,),
            in_specs=[pl.BlockSpec((tile_rows, d), lambda i, te, nt: (i, 0)),
                      by_expert(d, 2 * d_ff), by_expert(1, d_ff), by_expert(1, d_ff),
                      by_expert(d_ff, d), by_expert(1, d)],
            out_specs=pl.BlockSpec((tile_rows, d), lambda i, te, nt: (i, 0)),
            scratch_shapes=[pltpu.VMEM((d_ff, d), BF16), pltpu.VMEM((d_ff, d), BF16),
                            pltpu.VMEM((d_ff, d), BF16), pltpu.VMEM((d // LANES, slab, LANES), F32)]),
        out_shape=jax.ShapeDtypeStruct((r, d), BF16),
        compiler_params=_cparams("arbitrary"),
        name="experts",
    )(tile_e, n_used, buf, w_up, b_glu, b_lin, w_dn, b_dn)


def _rank_kernel(idx_ref, rank_ref, cnt_ref, seen_sc):
    @pl.when(pl.program_id(0) == 0)
    def _():
        seen_sc[...] = jnp.zeros(seen_sc.shape, F32)

    idx = idx_ref[...]
    tr, top_k = idx.shape
    lane = lax.broadcasted_iota(jnp.int32, (tr, LANES), 1)
    hits = [idx[:, k:k + 1] == lane for k in range(top_k)]
    routed = hits[0].astype(F32)
    for h in hits[1:]:
        routed = routed + h.astype(F32)
    earlier = (lax.broadcasted_iota(jnp.int32, (tr, tr), 1)
               < lax.broadcasted_iota(jnp.int32, (tr, tr), 0)).astype(BF16)
    before = _dot(earlier, routed.astype(BF16)) + seen_sc[...]
    rank_ref[...] = jnp.concatenate(
        [jnp.sum(jnp.where(h, before, 0.0), axis=1, keepdims=True) for h in hits],
        axis=1).astype(jnp.int32)
    seen_sc[...] = seen_sc[...] + jnp.sum(routed, axis=0, keepdims=True)
    cnt_ref[...] = seen_sc[...]


def _rank(top_idx):
    n_tok, top_k = top_idx.shape
    tr = min(ROUTE_ROWS, n_tok)
    assert n_tok % tr == 0
    return pl.pallas_call(
        _rank_kernel,
        grid=(n_tok // tr,),
        in_specs=[pl.BlockSpec((tr, top_k), lambda i: (i, 0))],
        out_specs=[pl.BlockSpec((tr, top_k), lambda i: (i, 0)),
                   pl.BlockSpec((1, LANES), lambda i: (0, 0))],
        out_shape=[jax.ShapeDtypeStruct((n_tok, top_k), jnp.int32),
                   jax.ShapeDtypeStruct((1, LANES), F32)],
        scratch_shapes=[pltpu.VMEM((1, LANES), F32)],
        compiler_params=_cparams("arbitrary"),
        name="moe_rank",
    )(top_idx)


def _combine_kernel(x_ref, y_ref, g_ref, o_ref):
    acc = x_ref[...]
    g = g_ref[...]
    for k in range(y_ref.shape[0]):
        acc = acc + g[:, k:k + 1] * y_ref[k].astype(F32)
    o_ref[...] = acc


def _combine(x2, y_kt, gate):
    top_k, n_tok, d = y_kt.shape
    tm = min(ROUTE_ROWS, n_tok)
    assert n_tok % tm == 0
    return pl.pallas_call(
        _combine_kernel,
        grid=(n_tok // tm,),
        in_specs=[pl.BlockSpec((tm, d), lambda i: (i, 0)),
                  pl.BlockSpec((top_k, tm, d), lambda i: (0, i, 0)),
                  pl.BlockSpec((tm, top_k), lambda i: (i, 0))],
        out_specs=pl.BlockSpec((tm, d), lambda i: (i, 0)),
        out_shape=jax.ShapeDtypeStruct((n_tok, d), F32),
        compiler_params=_cparams("parallel"),
        name="moe_combine",
    )(x2, y_kt, gate)


def _moe(x2, h2, top_idx, gate, expert_weights, tile_rows):
    n_tok, d = h2.shape
    n_exp = expert_weights[0].shape[0]
    assert n_exp <= LANES
    n_assign = n_tok * MOE_TOP_K
    rank, cnt = _rank(top_idx)
    counts = cnt[0, :n_exp].astype(jnp.int32)
    padded = (counts + tile_rows - 1) // tile_rows * tile_rows
    pend = jnp.cumsum(padded)
    experts = jnp.arange(n_exp, dtype=jnp.int32)
    first_row = jnp.sum(jnp.where(top_idx[:, :, None] == experts, pend - padded, 0), axis=-1)
    dest = (first_row + rank).reshape(-1)
    n_tiles = -(-n_assign // tile_rows) + n_exp
    tile_row0 = jnp.arange(n_tiles, dtype=jnp.int32) * tile_rows
    tile_e = jnp.minimum(jnp.sum((pend[None, :] <= tile_row0[:, None]).astype(jnp.int32), axis=1),
                         n_exp - 1)
    n_used = (pend[-1:] // tile_rows).astype(jnp.int32)
    src = jnp.zeros((n_tiles * tile_rows,), jnp.int32).at[dest].set(
        jnp.arange(n_assign, dtype=jnp.int32) // MOE_TOP_K,
        unique_indices=True, mode='promise_in_bounds')
    y = _experts(h2[src], tile_e, n_used, *expert_weights, tile_rows=tile_rows)
    y_kt = y[dest.reshape(n_tok, MOE_TOP_K).T]
    return _combine(x2, y_kt, gate)


def _rope_tables_t(pos, hd):
    half = hd // 2
    inv = ROPE_THETA ** (-jnp.arange(half, dtype=F32) / half)
    ang = inv[:, None] * pos.astype(F32)[None, :]
    cos, sin = jnp.cos(ang), jnp.sin(ang)
    return jnp.concatenate([cos, cos], axis=0), jnp.concatenate([-sin, sin], axis=0)


def kernel(x_prompt, x_sample, cache_k, cache_v, state_gla, page_table, norm1_g, w_in, q_norm_g,
           k_norm_g, w_gate_up, b_gate_up, gla_norm_g, w_branch_a, w_branch_b, w_merge_gate, w_out,
           norm2_g, w_router, b_router, w_up, b_up, w_down, b_down):
    depth = norm1_g.shape[0]
    assert depth == 1, "sample/prompt streams are only chained through one layer here"
    layer = 0
    n_p, t_p, d = x_prompt.shape
    n_s, t_s, _ = x_sample.shape
    heads, page, hd = cache_k.shape[2:]
    b_heads, dk, dv = state_gla.shape[2:]
    bk, bv = b_heads * dk, b_heads * dv
    aqk = heads * hd
    rank = w_gate_up.shape[1]
    past_len = page_table.shape[1] * page
    ppb = MOBA_BLOCK // page
    n_full = past_len // MOBA_BLOCK
    assert past_len % MOBA_BLOCK == 0 and n_full >= MOBA_TOPK and rank <= LANES
    assert w_in.shape[2] == 3 * aqk + 2 * bk + 2 * bv + rank

    w_qkv_t = w_in[layer][:, :3 * aqk].T.astype(BF16)
    w_rest = jnp.pad(w_in[layer][:, 3 * aqk:], ((0, 0), (0, LANES - rank))).astype(BF16)
    w_gu_p = jnp.pad(w_gate_up[layer], ((0, LANES - rank), (0, 0))).astype(BF16)
    g1 = norm1_g[layer][None, :]
    g2 = norm2_g[layer][None, :]
    qg = q_norm_g[layer][:, None]
    kg = k_norm_g[layer][:, None]
    bgu = b_gate_up[layer][None, :]
    gn = gla_norm_g[layer][None, :]
    w_a = w_branch_a[layer].astype(BF16)
    w_b = w_branch_b[layer].astype(BF16)
    w_mg = w_merge_gate[layer].astype(BF16)
    w_o = w_out[layer].astype(BF16)
    w_r = w_router[layer]
    b_r = b_router[layer][None, :]
    expert_weights = (w_up[layer], b_up[layer][:, None, 0::2], b_up[layer][:, None, 1::2],
                      w_down[layer], b_down[layer][:, None, :])
    proj = functools.partial(_project, norm1_g=g1, w_qkv_t=w_qkv_t, w_rest=w_rest, q_norm_g=qg,
                             k_norm_g=kg, w_gu_p=w_gu_p, b_gu=bgu, heads=heads, hd=hd, bk=bk, bv=bv)
    merge = functools.partial(_merge, norm1_g=g1, w_a=w_a, w_b=w_b, w_mg=w_mg, w_o=w_o,
                              norm2_g=g2, w_router=w_r, b_router=b_r)

    cos_p, sin_p = _rope_tables_t(jnp.arange(t_p), hd)
    qt, kt, vt, qb, kb, vb, la, gb = proj(x_prompt, cos_p, sin_p)
    n_part = n_p // PROMPT_SPLITS if n_p % PROMPT_SPLITS == 0 else n_p
    x_rows = x_prompt.reshape(n_p * t_p, d)
    y_parts, st_parts = [], []
    for b0 in range(0, n_p, n_part):
        o_a = _moba_prompt(qt, kt, vt, b0, n_part)
        o_b, st = _gla_prompt(qb, kb, vb, la, gb, gn, b0, n_part, heads=b_heads, dk=dk, dv=dv)
        x2, h2, idx, gate = merge(x_rows, b0 * t_p, o_a.reshape(n_part * t_p, -1),
                                  o_b.reshape(n_part * t_p, -1))
        y_parts.append(_moe(x2, h2, idx, gate, expert_weights, MOE_ROWS))
        st_parts.append(st)
    y_prompt = jnp.concatenate(y_parts).reshape(n_p, t_p, d)
    s_prompt = jnp.swapaxes(jnp.concatenate(st_parts), 2, 3)

    n_tok_s = n_s * t_s
    cos_s, sin_s = _rope_tables_t(past_len + jnp.arange(t_s), hd)
    qt_s, kt_s, vt_s, qb_s, kb_s, vb_s, la_s, gb_s = proj(
        x_sample.reshape(1, n_tok_s, d), jnp.tile(cos_s, (1, n_s)), jnp.tile(sin_s, (1, n_s)))
    per_seq = lambda a: a.reshape(heads, hd, n_s, t_s).transpose(2, 0, 3, 1)
    qa_s, ka_s, va_s = per_seq(qt_s), per_seq(kt_s), per_seq(vt_s)
    cache_kt = jnp.swapaxes(cache_k, 3, 4)
    cache_vt = jnp.swapaxes(cache_v, 3, 4)
    kmean = _sample_kmean(cache_kt, layer, page_table, n_full, ppb).transpose(0, 2, 1, 3)
    blk_idx = _sample_pick(qa_s, kmean, MOBA_TOPK)
    lpages = blk_idx[..., None] * ppb + jnp.arange(ppb, dtype=jnp.int32)
    phys = jnp.take_along_axis(page_table[:, None, None, :],
                               lpages.reshape(n_s, heads, t_s, MOBA_TOPK * ppb), axis=-1)
    o_a_s = _sample_attn(qa_s, ka_s, va_s, cache_kt, cache_vt, layer, phys)
    o_a_s = o_a_s.transpose(0, 2, 1, 3).reshape(n_tok_s, aqk).astype(BF16)
    tr = lambda a: a.reshape(n_s, t_s, b_heads, dk).transpose(0, 2, 3, 1)
    o_b_s, s_sample = _gla_sample(tr(qb_s), tr(kb_s), tr(la_s), vb_s.reshape(n_s, t_s, bv),
                                  gb_s.reshape(n_s, t_s, bv), gn, state_gla, layer, dk=dk, dv=dv)
    x2_s, h2_s, idx_s, gate_s = merge(x_sample.reshape(n_tok_s, d), 0, o_a_s,
                                      o_b_s.reshape(n_tok_s, bv))
    y_sample = _moe(x2_s, h2_s, idx_s, gate_s, expert_weights,
                    min(MOE_ROWS, n_tok_s)).reshape(n_s, t_s, d)
    k_prompt = jnp.swapaxes(kt, 2, 3)
    v_prompt = jnp.swapaxes(vt, 2, 3)
    return (y_prompt, y_sample, k_prompt[None], v_prompt[None], ka_s[None], va_s[None],
            s_prompt[None], s_sample[None])
```

```python
import functools

import jax
import jax.numpy as jnp
from jax import lax
from jax.experimental import pallas as pl
from jax.experimental.pallas import tpu as pltpu

F32 = jnp.float32
BF16 = jnp.bfloat16
HIGHEST = lax.Precision.HIGHEST

MOBA_BLOCK = 256
MOBA_TOPK = 3
ROPE_THETA = 10000.0
GLA_GATE_NORM = 16.0
NORM_EPS = 1e-6
MOE_TOP_K = 4
SWIGLU_ALPHA = 1.702
SWIGLU_LIMIT = 7.0

LANES = 128
PAD_ROWS = 16
VMEM_LIMIT_BYTES = 56 * 1024 * 1024

MASK_VALUE = -1e30
LOG2_E = 1.4426950408889634
PROJ_ROWS = 512
GLA_CHUNK = 256
GLA_SUB = 64
GLA_STEP_ROWS = 512
GLA_SEQS = 2
GLA_FAST_MIN_LOG_DECAY = -80.0
MOE_ROWS = 512
ROUTE_ROWS = 1024
PROMPT_SPLITS = 1
MOBA_GROUP = 4
SAMPLE_KMEAN_PAGES = 16


def _cparams(*sem):
    return pltpu.CompilerParams(dimension_semantics=sem, vmem_limit_bytes=VMEM_LIMIT_BYTES)


def _rms(x):
    return x * lax.rsqrt(jnp.mean(x * x, axis=-1, keepdims=True) + NORM_EPS)


def _dot(a, b):
    return jnp.dot(a, b, preferred_element_type=F32)


def _dot_nt(a, b, precision=None):
    return lax.dot_general(a, b, (((1,), (1,)), ((), ())), precision=precision,
                           preferred_element_type=F32)


def _dot_tn(a, b):
    return lax.dot_general(a, b, (((0,), (0,)), ((), ())), preferred_element_type=F32)


def _dot_parts(a, b, dot=_dot):
    a_hi = a.astype(BF16)
    b_hi = b.astype(BF16)
    a_lo = (a - a_hi.astype(F32)).astype(BF16)
    b_lo = (b - b_hi.astype(F32)).astype(BF16)
    return dot(a_hi, b_hi) + dot(a_hi, b_lo) + dot(a_lo, b_hi)


def _top_mask(g, pos, n_pick, n_pos, axis):
    sel = jnp.zeros(g.shape, F32)
    picks = []
    for _ in range(n_pick):
        mx = jnp.max(g, axis=axis, keepdims=True)
        cand = jnp.where((g == mx) & (mx > -jnp.inf), pos, n_pos)
        idx = jnp.min(cand, axis=axis, keepdims=True)
        pick = pos == idx
        sel = jnp.where(pick, 1.0, sel)
        g = jnp.where(pick, -jnp.inf, g)
        picks.append((idx, mx))
    return sel, picks


def _proj_kernel(x_ref, g1_ref, wqkv_ref, wrest_ref, qg_ref, kg_ref, wgu_ref, bgu_ref, cos_ref,
                 sin_ref, qt_ref, kt_ref, vt_ref, qb_ref, kb_ref, vb_ref, la_ref, gb_ref,
                 *, heads, hd, bk, bv):
    x = x_ref[0]
    hb = (_rms(x) * g1_ref[...]).astype(BF16)
    cos = cos_ref[...]
    sin = sin_ref[...]
    aqk = heads * hd

    def mm_t(lo, hi):
        return _dot_nt(wqkv_ref[lo:hi, :], hb)

    def norm_rope_t(z, g_ref, out_ref):
        for h in range(heads):
            y = z[h * hd:(h + 1) * hd, :]
            y = y * lax.rsqrt(jnp.mean(y * y, axis=0, keepdims=True) + NORM_EPS) * g_ref[...]
            rot = jnp.concatenate([y[hd // 2:], y[:hd // 2]], axis=0)
            out_ref[0, h] = y * cos + rot * sin

    norm_rope_t(mm_t(0, aqk), qg_ref, qt_ref)
    norm_rope_t(mm_t(aqk, 2 * aqk), kg_ref, kt_ref)
    zv = mm_t(2 * aqk, 3 * aqk)
    for h in range(heads):
        vt_ref[0, h] = zv[h * hd:(h + 1) * hd, :]

    def mm(lo, hi):
        return _dot(hb, wrest_ref[:, lo:hi])

    qb_ref[0] = mm(0, bk)
    kb_ref[0] = mm(bk, 2 * bk)
    vb_ref[0] = mm(2 * bk, 2 * bk + bv).astype(BF16)
    gb_ref[0] = mm(2 * bk + bv, 2 * bk + 2 * bv).astype(BF16)
    lr = mm(2 * bk + 2 * bv, 2 * bk + 2 * bv + LANES)
    u = _dot(lr.astype(BF16), wgu_ref[...]) + bgu_ref[...]
    log_sig = jnp.minimum(u, 0.0) - jnp.log(1.0 + jnp.exp(-jnp.abs(u)))
    la_ref[0] = log_sig / GLA_GATE_NORM


def _project(x, cos_t, sin_t, norm1_g, w_qkv_t, w_rest, q_norm_g, k_norm_g, w_gu_p, b_gu,
             *, heads, hd, bk, bv):
    n, t, d = x.shape
    tm = min(PROJ_ROWS, t)
    assert t % tm == 0
    full = lambda a: pl.BlockSpec(a.shape, lambda b, i: (0,) * a.ndim)
    head_out = pl.BlockSpec((1, heads, hd, tm), lambda b, i: (b, 0, 0, i))
    row_out = lambda w: pl.BlockSpec((1, tm, w), lambda b, i: (b, i, 0))
    table = pl.BlockSpec((hd, tm), lambda b, i: (0, i))
    head_shape = jax.ShapeDtypeStruct((n, heads, hd, t), F32)
    return pl.pallas_call(
        functools.partial(_proj_kernel, heads=heads, hd=hd, bk=bk, bv=bv),
        grid=(n, t // tm),
        in_specs=[pl.BlockSpec((1, tm, d), lambda b, i: (b, i, 0)),
                  full(norm1_g), full(w_qkv_t), full(w_rest), full(q_norm_g), full(k_norm_g),
                  full(w_gu_p), full(b_gu), table, table],
        out_specs=[head_out, head_out, head_out, row_out(bk), row_out(bk), row_out(bv),
                   row_out(bk), row_out(bv)],
        out_shape=[head_shape, head_shape, head_shape,
                   jax.ShapeDtypeStruct((n, t, bk), F32),
                   jax.ShapeDtypeStruct((n, t, bk), F32),
                   jax.ShapeDtypeStruct((n, t, bv), BF16),
                   jax.ShapeDtypeStruct((n, t, bk), F32),
                   jax.ShapeDtypeStruct((n, t, bv), BF16)],
        compiler_params=_cparams("parallel", "parallel"),
        name="proj",
    )(x, norm1_g, w_qkv_t, w_rest, q_norm_g, k_norm_g, w_gu_p, b_gu, cos_t, sin_t)


def _moba_kernel(qt_ref, kt_ref, vt_ref, o_ref, k_sc, vt_sc, kmean_sc, qs_sc, bias_sc, m_sc, acc_sc,
                 sa_sc, sb_sc, ma_sc, mb_sc, *, nb, blk, hd, hp, grp):
    c = pl.program_id(2)
    nq = hp * blk
    vr = hd + PAD_ROWS
    n_groups = nb // grp
    cols = [slice(i * blk, (i + 1) * blk) for i in range(hp)]

    @pl.when(c == 0)
    def _():
        row = lax.broadcasted_iota(jnp.int32, (LANES - hd, blk), 0)
        ones_row = (lax.broadcasted_iota(jnp.int32, (PAD_ROWS, blk), 0) == 0).astype(BF16)

        def stage_kv(j, carry):
            off = pl.multiple_of(j * blk, blk)
            place = (row == j % grp).astype(F32)
            for i in range(hp):
                k_nat = jnp.concatenate([kt_ref[0, i, :, pl.ds(off, blk)], place], axis=0).T
                k_sc[i, pl.ds(off, blk), :] = k_nat.astype(BF16)
                kmean_sc[i, pl.ds(j, 1), :] = jnp.mean(k_nat, axis=0, keepdims=True)
                vt_sc[i * vr:i * vr + hd, pl.ds(off, blk)] = vt_ref[0, i, :, pl.ds(off, blk)].astype(BF16)
                vt_sc[i * vr + hd:(i + 1) * vr, pl.ds(off, blk)] = ones_row
            return carry

        lax.fori_loop(0, nb, stage_kv, 0)

    zero_rows = jnp.zeros((LANES - hd, blk), F32)
    q_pad = [jnp.concatenate([qt_ref[0, i], zero_rows], axis=0) for i in range(hp)]
    gate = jnp.concatenate(
        [_dot_parts(kmean_sc[i], q_pad[i]) for i in range(hp)], axis=1)
    blk_row = lax.broadcasted_iota(jnp.int32, (nb, nq), 0)
    sel, _ = _top_mask(jnp.where(blk_row < c, gate, -jnp.inf), blk_row, min(MOBA_TOPK, nb - 1),
                       nb, axis=0)
    bias = (sel - 1.0) * -MASK_VALUE
    pad = jnp.zeros((PAD_ROWS - grp, blk), F32)
    for g in range(n_groups):
        for i in range(hp):
            bias_sc[g, i] = jnp.concatenate([bias[g * grp:(g + 1) * grp, cols[i]], pad],
                                            axis=0).astype(BF16)
    for i in range(hp):
        qs_sc[i] = (q_pad[i] * (hd ** -0.5 * LOG2_E)).astype(BF16)

    def scores(off, n_keys):
        return jnp.concatenate([_dot(k_sc[i, pl.ds(off, n_keys), :], qs_sc[i]) for i in range(hp)],
                               axis=1)

    def update(s, m_blk, off, n_keys, first):
        m_new = m_blk if first else jnp.maximum(m_sc[...], m_blk)
        pb = jnp.exp2(s - m_new).astype(BF16)
        if not first:
            alpha = jnp.exp2(m_sc[...] - m_new)
        m_sc[...] = m_new
        for i in range(hp):
            pv = _dot(vt_sc[i * vr:(i + 1) * vr, pl.ds(off, n_keys)], pb[:, cols[i]])
            acc_sc[:, cols[i]] = pv if first else alpha[:, cols[i]] * acc_sc[:, cols[i]] + pv

    own_off = pl.multiple_of(c * blk, blk)
    causal = (lax.broadcasted_iota(jnp.int32, (blk, nq), 0)
              <= lax.broadcasted_iota(jnp.int32, (blk, nq), 1) % blk)
    s_own = jnp.where(causal, scores(own_off, blk), MASK_VALUE)

    n_grp = (c + grp - 1) // grp

    def stage(g, s_buf, mx_buf):
        g = jnp.minimum(g, n_groups - 1)
        for i in range(hp):
            qs_sc[i, hd:hd + PAD_ROWS, :] = bias_sc[g, i]
        s = scores(pl.multiple_of(g * (grp * blk), grp * blk), grp * blk)
        s_buf[...] = s
        mx_buf[...] = jnp.max(s, axis=0, keepdims=True)

    def consume(g, s_buf, mx_buf):
        update(s_buf[...], mx_buf[...], pl.multiple_of(g * (grp * blk), grp * blk), grp * blk, False)

    stage(0, sa_sc, ma_sc)
    update(s_own, jnp.max(s_own, axis=0, keepdims=True), own_off, blk, True)

    def body(u, carry):
        stage(2 * u + 1, sb_sc, mb_sc)
        consume(2 * u, sa_sc, ma_sc)
        stage(2 * u + 2, sa_sc, ma_sc)
        consume(2 * u + 1, sb_sc, mb_sc)
        return carry

    lax.fori_loop(0, n_grp // 2, body, 0)

    @pl.when(n_grp % 2 == 1)
    def _():
        consume(n_grp - 1, sa_sc, ma_sc)

    out_t = acc_sc[:hd, :] / acc_sc[hd:hd + 1, :]
    out_t = jnp.concatenate([out_t[:, cols[i]] for i in range(hp)], axis=0)
    o_ref[0] = out_t.T.astype(BF16)


def _moba_prompt(qt, kt, vt, b0, n):
    _, heads, hd, t = qt.shape
    blk = MOBA_BLOCK
    assert t % blk == 0 and LANES % hd == 0
    hp = LANES // hd
    assert heads % hp == 0
    nb = t // blk
    grp = MOBA_GROUP if nb % MOBA_GROUP == 0 else 1
    nq = hp * blk
    kv_spec = pl.BlockSpec((1, hp, hd, t), lambda b, g, c: (b + b0, g, 0, 0))
    return pl.pallas_call(
        functools.partial(_moba_kernel, nb=nb, blk=blk, hd=hd, hp=hp, grp=grp),
        grid=(n, heads // hp, nb),
        in_specs=[pl.BlockSpec((1, hp, hd, blk), lambda b, g, c: (b + b0, g, 0, c)), kv_spec, kv_spec],
        out_specs=pl.BlockSpec((1, blk, hp * hd), lambda b, g, c: (b, c, g)),
        out_shape=jax.ShapeDtypeStruct((n, t, heads * hd), BF16),
        scratch_shapes=[pltpu.VMEM((hp, t, LANES), BF16),
                        pltpu.VMEM((hp * (hd + PAD_ROWS), t), BF16),
                        pltpu.VMEM((hp, nb, LANES), F32),
                        pltpu.VMEM((hp, LANES, blk), BF16),
                        pltpu.VMEM((nb // grp, hp, PAD_ROWS, blk), BF16),
                        pltpu.VMEM((1, nq), F32),
                        pltpu.VMEM((hd + PAD_ROWS, nq), F32),
                        pltpu.VMEM((grp * blk, nq), F32), pltpu.VMEM((grp * blk, nq), F32),
                        pltpu.VMEM((1, nq), F32), pltpu.VMEM((1, nq), F32)],
        compiler_params=_cparams("parallel", "parallel", "arbitrary"),
        name="moba_prompt",
    )(qt, kt, vt)


def _kmean_pages_kernel(pt_ref, *refs, n_pg, ppb, rows):
    del pt_ref
    out_ref = refs[n_pg]
    for b in range(n_pg // ppb):
        acc = refs[b * ppb][0, 0]
        for j in range(1, ppb):
            acc = acc + refs[b * ppb + j][0, 0]
        out_ref[0, b] = jnp.sum(acc, axis=-1) / rows


def _sample_kmean(cache_kt, layer, page_table, n_full, ppb):
    _, _, heads, hd, page = cache_kt.shape
    n, n_pages = page_table.shape
    n_pg = SAMPLE_KMEAN_PAGES
    assert (n_full * ppb) % n_pg == 0 and n_pg % ppb == 0

    def page_map(b, g, pt, j):
        return (layer, pt[b * n_pages + g * n_pg + j], 0, 0, 0)

    specs = [pl.BlockSpec((1, 1, heads, hd, page), functools.partial(page_map, j=j))
             for j in range(n_pg)]
    return pl.pallas_call(
        functools.partial(_kmean_pages_kernel, n_pg=n_pg, ppb=ppb, rows=float(ppb * page)),
        grid_spec=pltpu.PrefetchScalarGridSpec(
            num_scalar_prefetch=1, grid=(n, n_full * ppb // n_pg), in_specs=specs,
            out_specs=pl.BlockSpec((1, n_pg // ppb, heads, hd), lambda b, g, pt: (b, g, 0, 0))),
        out_shape=jax.ShapeDtypeStruct((n, n_full, heads, hd), F32),
        compiler_params=_cparams("parallel", "arbitrary"),
        name="sample_kmean",
    )(page_table.reshape(-1), *([cache_kt] * n_pg))


def _sample_pick_kernel(q_ref, km_ref, idx_ref, *, heads, n_full, n_sel):
    col = lax.broadcasted_iota(jnp.int32, (q_ref.shape[2], n_full), 1)
    for h in range(heads):
        gate = _dot_parts(q_ref[0, h], km_ref[0, h], dot=_dot_nt)
        _, picks = _top_mask(gate, col, n_sel, n_full, axis=1)
        idx_ref[0, h] = jnp.concatenate([p[0] for p in picks], axis=1)


def _sample_pick(q, kmean, n_sel):
    n, heads, t, hd = q.shape
    n_full = kmean.shape[2]
    return pl.pallas_call(
        functools.partial(_sample_pick_kernel, heads=heads, n_full=n_full, n_sel=n_sel),
        grid=(n,),
        in_specs=[pl.BlockSpec((1, heads, t, hd), lambda b: (b, 0, 0, 0)),
                  pl.BlockSpec((1, heads, n_full, hd), lambda b: (b, 0, 0, 0))],
        out_specs=pl.BlockSpec((1, heads, t, n_sel), lambda b: (b, 0, 0, 0)),
        out_shape=jax.ShapeDtypeStruct((n, heads, t, n_sel), jnp.int32),
        compiler_params=_cparams("parallel"),
        name="sample_pick",
    )(q, kmean)


def _sample_attn_kernel(ph_ref, q_ref, kn_ref, vn_ref, *refs, n_pg):
    del ph_ref
    tq, hd = q_ref.shape[2], q_ref.shape[3]
    k_refs, v_refs, o_ref = refs[:tq * n_pg], refs[tq * n_pg:2 * tq * n_pg], refs[2 * tq * n_pg]
    qs = (q_ref[0, 0] * hd ** -0.5).astype(BF16)
    own_ok = (lax.broadcasted_iota(jnp.int32, (tq, tq), 1)
              <= lax.broadcasted_iota(jnp.int32, (tq, tq), 0))
    s_own = jnp.where(own_ok, _dot_nt(qs, kn_ref[0, 0].astype(BF16)), MASK_VALUE)
    vn = vn_ref[0, 0].astype(BF16)
    for t in range(tq):
        pages = slice(t * n_pg, (t + 1) * n_pg)
        kt = jnp.concatenate([r[0, 0, 0] for r in k_refs[pages]], axis=1).astype(BF16)
        vt = jnp.concatenate([r[0, 0, 0] for r in v_refs[pages]], axis=1).astype(BF16)
        s_sel = _dot(qs, kt)
        m = jnp.maximum(jnp.max(s_sel, axis=1, keepdims=True), jnp.max(s_own, axis=1, keepdims=True))
        p_sel = jnp.exp(s_sel - m)
        p_own = jnp.exp(s_own - m)
        denom = jnp.sum(p_sel, axis=1, keepdims=True) + jnp.sum(p_own, axis=1, keepdims=True)
        o = (_dot_nt(p_sel.astype(BF16), vt) + _dot(p_own.astype(BF16), vn)) / denom
        o_ref[0, 0, t:t + 1, :] = o[t:t + 1]


def _sample_attn(q, k_new, v_new, cache_kt, cache_vt, layer, phys):
    n, heads, t, hd = q.shape
    page = cache_kt.shape[4]
    n_pg = phys.shape[-1]

    def page_map(b, h, ph, j):
        return (layer, ph[(b * heads + h) * (t * n_pg) + j], h, 0, 0)

    pages = [pl.BlockSpec((1, 1, 1, hd, page), functools.partial(page_map, j=j))
             for j in range(t * n_pg)]
    own = pl.BlockSpec((1, 1, t, hd), lambda b, h, ph: (b, h, 0, 0))
    return pl.pallas_call(
        functools.partial(_sample_attn_kernel, n_pg=n_pg),
        grid_spec=pltpu.PrefetchScalarGridSpec(
            num_scalar_prefetch=1, grid=(n, heads),
            in_specs=[own, own, own] + pages * 2, out_specs=own),
        out_shape=jax.ShapeDtypeStruct((n, heads, t, hd), F32),
        compiler_params=_cparams("parallel", "parallel"),
        name="sample_attn",
    )(phys.reshape(-1), q, k_new, v_new, *([cache_kt] * (t * n_pg)), *([cache_vt] * (t * n_pg)))


def _gla_kernel(q_ref, k_ref, v_ref, la_ref, gb_ref, gn_ref, o_ref, sfin_ref, st_sc, qk_sc, b_sc,
                la_sc, attn_sc, *, chunk, sub, n_chunk, n_seq, heads, dk, dv):
    step = pl.program_id(1)

    @pl.when(step == 0)
    def _():
        st_sc[...] = jnp.zeros(st_sc.shape, F32)

    r_i = lax.broadcasted_iota(jnp.int32, (chunk, chunk), 0)
    c_i = lax.broadcasted_iota(jnp.int32, (chunk, chunk), 1)
    tri = r_i >= c_i
    tri_b = tri.astype(BF16)
    row_k = lax.broadcasted_iota(jnp.int32, (chunk, dk), 0)
    q_scale = dk ** -0.5

    def intra_fast(q, k, b):
        rows = []
        for i in range(chunk // sub):
            lo, hi = i * sub, (i + 1) * sub
            ref = b[lo - 1:lo, :] if i > 0 else jnp.zeros((1, dk), F32)
            qt = (q[lo:hi] * jnp.exp(b[lo:hi] - ref)).astype(BF16)
            kt = (k * jnp.exp(jnp.where(row_k < hi, ref - b, -jnp.inf))).astype(BF16)
            rows.append(_dot_nt(qt, kt))
        return jnp.where(tri, jnp.concatenate(rows, axis=0), 0.0)

    def intra_exact(q, k, b, la):
        qk_sc[...] = q
        b_sc[...] = b
        la_sc[...] = la
        col = lax.broadcasted_iota(jnp.int32, (1, chunk), 1)

        def row(t, carry):
            q_t = qk_sc[pl.ds(t, 1), :]
            la_t = la_sc[pl.ds(t, 1), :]
            before = b_sc[pl.ds(t, 1), :] - la_t
            kt = (k * jnp.exp(jnp.where(row_k < t, before - b, -jnp.inf))).astype(BF16)
            a_row = _dot_nt((q_t * jnp.exp(la_t)).astype(BF16), kt)
            k_t = jnp.sum(jnp.where(row_k == t, k, 0.0), axis=0, keepdims=True)
            diag = jnp.sum(q_t * k_t, axis=1, keepdims=True)
            attn_sc[pl.ds(t, 1), :] = jnp.where(col == t, diag, a_row)
            return carry

        lax.fori_loop(0, chunk, row, 0)
        return attn_sc[...]

    def one_chunk(ci, carry, exact):
        off = pl.multiple_of(ci * chunk, chunk)
        rows_ds = pl.ds(off, chunk)
        for s in range(n_seq):
            la = la_ref[s, rows_ds, :]
            la_hi = la.astype(BF16)
            la_mid = (la - la_hi.astype(F32)).astype(BF16)
            la_lo = (la - la_hi.astype(F32) - la_mid.astype(F32)).astype(BF16)
            b_all = _dot(tri_b, la_hi) + _dot(tri_b, la_mid) + _dot(tri_b, la_lo)
            for h in range(heads):
                kc = slice(h * dk, (h + 1) * dk)
                vc = slice(h * dv, (h + 1) * dv)
                q = q_ref[s, rows_ds, kc] * q_scale
                k = k_ref[s, rows_ds, kc]
                v = v_ref[s, rows_ds, vc]
                b = b_all[:, kc]
                st = st_sc[s, h]
                o = _dot_nt((q * jnp.exp(b)).astype(BF16), st.astype(BF16))
                attn = intra_exact(q, k, b, la[:, kc]) if exact else intra_fast(q, k, b)
                o = o + _dot(attn.astype(BF16), v)
                b_last = b[chunk - 1:chunk, :]
                k_dec = (k * jnp.exp(b_last - b)).astype(BF16)
                st_sc[s, h] = st * jnp.exp(b_last) + _dot_tn(v, k_dec)
                gb = gb_ref[s, rows_ds, vc].astype(F32)
                o_ref[s, rows_ds, vc] = (_rms(o) * gn_ref[...] * (gb * jax.nn.sigmoid(gb))).astype(BF16)
        return carry

    safe = jnp.min(la_ref[...]) * sub > GLA_FAST_MIN_LOG_DECAY

    @pl.when(safe)
    def _():
        lax.fori_loop(0, n_chunk, functools.partial(one_chunk, exact=False), 0)

    @pl.when(jnp.logical_not(safe))
    def _():
        lax.fori_loop(0, n_chunk, functools.partial(one_chunk, exact=True), 0)

    @pl.when(step == pl.num_programs(1) - 1)
    def _():
        sfin_ref[...] = st_sc[...]


def _gla_prompt(qb, kb, vb, la, gb, gla_norm_g, b0, n, *, heads, dk, dv):
    _, t, _ = qb.shape
    rows = min(GLA_STEP_ROWS, t)
    chunk = min(GLA_CHUNK, rows)
    n_seq = GLA_SEQS if n % GLA_SEQS == 0 and b0 % GLA_SEQS == 0 else 1
    assert t % rows == 0 and rows % chunk == 0 and chunk % GLA_SUB == 0
    kspec = pl.BlockSpec((n_seq, rows, heads * dk), lambda b, s: (b + b0 // n_seq, s, 0))
    vspec = pl.BlockSpec((n_seq, rows, heads * dv), lambda b, s: (b + b0 // n_seq, s, 0))
    out_spec = pl.BlockSpec((n_seq, rows, heads * dv), lambda b, s: (b, s, 0))
    return pl.pallas_call(
        functools.partial(_gla_kernel, chunk=chunk, sub=GLA_SUB, n_chunk=rows // chunk,
                          n_seq=n_seq, heads=heads, dk=dk, dv=dv),
        grid=(n // n_seq, t // rows),
        in_specs=[kspec, kspec, vspec, kspec, vspec, pl.BlockSpec((1, dv), lambda b, s: (0, 0))],
        out_specs=[out_spec, pl.BlockSpec((n_seq, heads, dv, dk), lambda b, s: (b, 0, 0, 0))],
        out_shape=[jax.ShapeDtypeStruct((n, t, heads * dv), BF16),
                   jax.ShapeDtypeStruct((n, heads, dv, dk), F32)],
        scratch_shapes=[pltpu.VMEM((n_seq, heads, dv, dk), F32), pltpu.VMEM((chunk, dk), F32),
                        pltpu.VMEM((chunk, dk), F32), pltpu.VMEM((chunk, dk), F32),
                        pltpu.VMEM((chunk, chunk), F32)],
        compiler_params=_cparams("parallel", "arbitrary"),
        name="gla_prompt",
    )(qb, kb, vb, la, gb, gla_norm_g)


def _gla_sample_kernel(qt_ref, kt_ref, lat_ref, v_ref, gb_ref, gn_ref, s0_ref, o_ref, sfin_ref,
                       *, heads, dk, dv):
    for h in range(heads):
        vc = slice(h * dv, (h + 1) * dv)
        s = s0_ref[0, 0, h]
        qt = qt_ref[0, h] * dk ** -0.5
        kt = kt_ref[0, h]
        at = jnp.exp(lat_ref[0, h])
        v = v_ref[0, :, vc].astype(F32)
        outs = []
        for t in range(v.shape[0]):
            s = at[:, t:t + 1] * s + kt[:, t:t + 1] * v[t:t + 1, :]
            outs.append(jnp.sum(qt[:, t:t + 1] * s, axis=0, keepdims=True))
        o = jnp.concatenate(outs, axis=0)
        gb = gb_ref[0, :, vc].astype(F32)
        o_ref[0, :, vc] = (_rms(o) * gn_ref[...] * (gb * jax.nn.sigmoid(gb))).astype(BF16)
        sfin_ref[0, h] = s


def _gla_sample(qt, kt, lat, vb, gb, gla_norm_g, state, layer, *, dk, dv):
    n, heads, _, t = qt.shape
    tspec = pl.BlockSpec((1, heads, dk, t), lambda b: (b, 0, 0, 0))
    vspec = pl.BlockSpec((1, t, heads * dv), lambda b: (b, 0, 0))
    return pl.pallas_call(
        functools.partial(_gla_sample_kernel, heads=heads, dk=dk, dv=dv),
        grid=(n,),
        in_specs=[tspec, tspec, tspec, vspec, vspec, pl.BlockSpec((1, dv), lambda b: (0, 0)),
                  pl.BlockSpec((1, 1, heads, dk, dv), lambda b: (layer, b, 0, 0, 0))],
        out_specs=[vspec, pl.BlockSpec((1, heads, dk, dv), lambda b: (b, 0, 0, 0))],
        out_shape=[jax.ShapeDtypeStruct((n, t, heads * dv), BF16),
                   jax.ShapeDtypeStruct((n, heads, dk, dv), F32)],
        compiler_params=_cparams("parallel"),
        name="gla_sample",
    )(qt, kt, lat, vb, gb, gla_norm_g, state)


def _merge_kernel(x_ref, oa_ref, ob_ref, g1_ref, wa_ref, wb_ref, wmg_ref, wo_ref, g2_ref, wrh_ref,
                  wrl_ref, br_ref, x2_ref, h2_ref, idx_ref, gate_ref, *, d, n_exp):
    x = x_ref[...]
    hb = (_rms(x) * g1_ref[...]).astype(BF16)
    y_a = _dot(oa_ref[...], wa_ref[...])
    y_b = _dot(ob_ref[...], wb_ref[...])
    mixed = (jax.nn.sigmoid(_dot(hb, wmg_ref[:, :d])) * y_a
             + jax.nn.sigmoid(_dot(hb, wmg_ref[:, d:])) * y_b)
    x2 = x + _dot(mixed.astype(BF16), wo_ref[...])
    x2_ref[...] = x2
    h2 = _rms(x2) * g2_ref[...]
    h2_hi = h2.astype(BF16)
    h2_ref[...] = h2_hi
    h2_lo = (h2 - h2_hi.astype(F32)).astype(BF16)
    logits = (_dot(h2_hi, wrh_ref[...]) + _dot(h2_hi, wrl_ref[...]) + _dot(h2_lo, wrh_ref[...])
              + br_ref[...])
    col = lax.broadcasted_iota(jnp.int32, logits.shape, 1)
    _, picks = _top_mask(logits, col, MOE_TOP_K, n_exp, axis=1)
    top_val = jnp.concatenate([p[1] for p in picks], axis=1)
    e = jnp.exp(top_val - top_val[:, :1])
    idx_ref[...] = jnp.concatenate([p[0] for p in picks], axis=1)
    gate_ref[...] = e / jnp.sum(e, axis=1, keepdims=True)


def _merge(x, row0, o_a, o_b, norm1_g, w_a, w_b, w_mg, w_o, norm2_g, w_router, b_router):
    n_tok, d = o_a.shape[0], x.shape[1]
    n_exp = w_router.shape[1]
    w_router_hi = w_router.astype(BF16)
    w_router_lo = (w_router - w_router_hi.astype(F32)).astype(BF16)
    tm = min(PROJ_ROWS, n_tok)
    assert n_tok % tm == 0 and row0 % tm == 0
    full = lambda a: pl.BlockSpec(a.shape, lambda i: (0,) * a.ndim)
    rows = lambda w: pl.BlockSpec((tm, w), lambda i: (i, 0))
    return pl.pallas_call(
        functools.partial(_merge_kernel, d=d, n_exp=n_exp),
        grid=(n_tok // tm,),
        in_specs=[pl.BlockSpec((tm, d), lambda i: (i + row0 // tm, 0)),
                  rows(o_a.shape[1]), rows(o_b.shape[1]), full(norm1_g), full(w_a),
                  full(w_b), full(w_mg), full(w_o), full(norm2_g), full(w_router_hi),
                  full(w_router_lo), full(b_router)],
        out_specs=[rows(d), rows(d), rows(MOE_TOP_K), rows(MOE_TOP_K)],
        out_shape=[jax.ShapeDtypeStruct((n_tok, d), F32), jax.ShapeDtypeStruct((n_tok, d), BF16),
                   jax.ShapeDtypeStruct((n_tok, MOE_TOP_K), jnp.int32),
                   jax.ShapeDtypeStruct((n_tok, MOE_TOP_K), F32)],
        compiler_params=_cparams("parallel"),
        name="merge",
    )(x, o_a, o_b, norm1_g, w_a, w_b, w_mg, w_o, norm2_g, w_router_hi, w_router_lo, b_router)


def _expert_kernel(te_ref, nt_ref, x_ref, wup_ref, bg_ref, bl_ref, wdn_ref, bdn_ref, y_ref,
                   wg_sc, wl_sc, wd_sc, t_sc, *, slab):
    i = pl.program_id(0)

    @pl.when((i == 0) | (te_ref[i] != te_ref[jnp.maximum(i - 1, 0)]))
    def _():
        for c in range(0, wup_ref.shape[2], slab):
            rows = slice(c // 2, (c + slab) // 2)
            for j in range(t_sc.shape[0]):
                lanes = slice(j * LANES, (j + 1) * LANES)
                t_sc[j] = wup_ref[0, lanes, c:c + slab].T
                wg_sc[rows, lanes] = t_sc[j, pl.ds(0, slab // 2, stride=2), :].astype(BF16)
                wl_sc[rows, lanes] = t_sc[j, pl.ds(1, slab // 2, stride=2), :].astype(BF16)
        wd_sc[...] = wdn_ref[0].astype(BF16)

    @pl.when(i < nt_ref[0])
    def _():
        x = x_ref[...]
        x_glu = jnp.minimum(_dot_nt(x, wg_sc[...]) + bg_ref[0], SWIGLU_LIMIT)
        x_lin = jnp.clip(_dot_nt(x, wl_sc[...]) + bl_ref[0], -SWIGLU_LIMIT, SWIGLU_LIMIT)
        a = x_glu * jax.nn.sigmoid(SWIGLU_ALPHA * x_glu) * (x_lin + 1.0)
        y_ref[...] = (_dot(a.astype(BF16), wd_sc[...]) + bdn_ref[0]).astype(y_ref.dtype)

    @pl.when(i >= nt_ref[0])
    def _():
        y_ref[...] = jnp.zeros(y_ref.shape, y_ref.dtype)


def _experts(buf, tile_e, n_used, w_up, b_glu, b_lin, w_dn, b_dn, *, tile_rows):
    r, d = buf.shape
    d_ff = w_dn.shape[1]
    slab = min(2 * LANES, 2 * d_ff)
    by_expert = lambda *blk: pl.BlockSpec((1,) + blk, lambda i, te, nt: (te[i], 0, 0))
    return pl.pallas_call(
        functools.partial(_expert_kernel, slab=slab),
        grid_spec=pltpu.PrefetchScalarGridSpec(
            num_scalar_prefetch=2, grid=(r // tile_rows,),
            in_specs=[pl.BlockSpec((tile_rows, d), lambda i, te, nt: (i, 0)),
                      by_expert(d, 2 * d_ff), by_expert(1, d_ff), by_expert(1, d_ff),
                      by_expert(d_ff, d), by_expert(1, d)],
            out_specs=pl.BlockSpec((tile_rows, d), lambda i, te, nt: (i, 0)),
            scratch_shapes=[pltpu.VMEM((d_ff, d), BF16), pltpu.VMEM((d_ff, d), BF16),
                            pltpu.VMEM((d_ff, d), BF16), pltpu.VMEM((d // LANES, slab, LANES), F32)]),
        out_shape=jax.ShapeDtypeStruct((r, d), BF16),
        compiler_params=_cparams("arbitrary"),
        name="experts",
    )(tile_e, n_used, buf, w_up, b_glu, b_lin, w_dn, b_dn)


def _rank_kernel(idx_ref, rank_ref, cnt_ref, seen_sc):
    @pl.when(pl.program_id(0) == 0)
    def _():
        seen_sc[...] = jnp.zeros(seen_sc.shape, F32)

    idx = idx_ref[...]
    tr, top_k = idx.shape
    lane = lax.broadcasted_iota(jnp.int32, (tr, LANES), 1)
    hits = [idx[:, k:k + 1] == lane for k in range(top_k)]
    routed = hits[0].astype(F32)
    for h in hits[1:]:
        routed = routed + h.astype(F32)
    earlier = (lax.broadcasted_iota(jnp.int32, (tr, tr), 1)
               < lax.broadcasted_iota(jnp.int32, (tr, tr), 0)).astype(BF16)
    before = _dot(earlier, routed.astype(BF16)) + seen_sc[...]
    rank_ref[...] = jnp.concatenate(
        [jnp.sum(jnp.where(h, before, 0.0), axis=1, keepdims=True) for h in hits],
        axis=1).astype(jnp.int32)
    seen_sc[...] = seen_sc[...] + jnp.sum(routed, axis=0, keepdims=True)
    cnt_ref[...] = seen_sc[...]


def _rank(top_idx):
    n_tok, top_k = top_idx.shape
    tr = min(ROUTE_ROWS, n_tok)
    assert n_tok % tr == 0
    return pl.pallas_call(
        _rank_kernel,
        grid=(n_tok // tr,),
        in_specs=[pl.BlockSpec((tr, top_k), lambda i: (i, 0))],
        out_specs=[pl.BlockSpec((tr, top_k), lambda i: (i, 0)),
                   pl.BlockSpec((1, LANES), lambda i: (0, 0))],
        out_shape=[jax.ShapeDtypeStruct((n_tok, top_k), jnp.int32),
                   jax.ShapeDtypeStruct((1, LANES), F32)],
        scratch_shapes=[pltpu.VMEM((1, LANES), F32)],
        compiler_params=_cparams("arbitrary"),
        name="moe_rank",
    )(top_idx)


def _combine_kernel(x_ref, y_ref, g_ref, o_ref):
    acc = x_ref[...]
    g = g_ref[...]
    for k in range(y_ref.shape[0]):
        acc = acc + g[:, k:k + 1] * y_ref[k].astype(F32)
    o_ref[...] = acc


def _combine(x2, y_kt, gate):
    top_k, n_tok, d = y_kt.shape
    tm = min(ROUTE_ROWS, n_tok)
    assert n_tok % tm == 0
    return pl.pallas_call(
        _combine_kernel,
        grid=(n_tok // tm,),
        in_specs=[pl.BlockSpec((tm, d), lambda i: (i, 0)),
                  pl.BlockSpec((top_k, tm, d), lambda i: (0, i, 0)),
                  pl.BlockSpec((tm, top_k), lambda i: (i, 0))],
        out_specs=pl.BlockSpec((tm, d), lambda i: (i, 0)),
        out_shape=jax.ShapeDtypeStruct((n_tok, d), F32),
        compiler_params=_cparams("parallel"),
        name="moe_combine",
    )(x2, y_kt, gate)


def _moe(x2, h2, top_idx, gate, expert_weights, tile_rows):
    n_tok, d = h2.shape
    n_exp = expert_weights[0].shape[0]
    assert n_exp <= LANES
    n_assign = n_tok * MOE_TOP_K
    rank, cnt = _rank(top_idx)
    counts = cnt[0, :n_exp].astype(jnp.int32)
    padded = (counts + tile_rows - 1) // tile_rows * tile_rows
    pend = jnp.cumsum(padded)
    experts = jnp.arange(n_exp, dtype=jnp.int32)
    first_row = jnp.sum(jnp.where(top_idx[:, :, None] == experts, pend - padded, 0), axis=-1)
    dest = (first_row + rank).reshape(-1)
    n_tiles = -(-n_assign // tile_rows) + n_exp
    tile_row0 = jnp.arange(n_tiles, dtype=jnp.int32) * tile_rows
    tile_e = jnp.minimum(jnp.sum((pend[None, :] <= tile_row0[:, None]).astype(jnp.int32), axis=1),
                         n_exp - 1)
    n_used = (pend[-1:] // tile_rows).astype(jnp.int32)
    src = jnp.zeros((n_tiles * tile_rows,), jnp.int32).at[dest].set(
        jnp.arange(n_assign, dtype=jnp.int32) // MOE_TOP_K,
        unique_indices=True, mode='promise_in_bounds')
    y = _experts(h2[src], tile_e, n_used, *expert_weights, tile_rows=tile_rows)
    y_kt = y[dest.reshape(n_tok, MOE_TOP_K).T]
    return _combine(x2, y_kt, gate)


def _rope_tables_t(pos, hd):
    half = hd // 2
    inv = ROPE_THETA ** (-jnp.arange(half, dtype=F32) / half)
    ang = inv[:, None] * pos.astype(F32)[None, :]
    cos, sin = jnp.cos(ang), jnp.sin(ang)
    return jnp.concatenate([cos, cos], axis=0), jnp.concatenate([-sin, sin], axis=0)


def kernel(x_prompt, x_sample, cache_k, cache_v, state_gla, page_table, norm1_g, w_in, q_norm_g,
           k_norm_g, w_gate_up, b_gate_up, gla_norm_g, w_branch_a, w_branch_b, w_merge_gate, w_out,
           norm2_g, w_router, b_router, w_up, b_up, w_down, b_down):
    depth = norm1_g.shape[0]
    assert depth == 1, "sample/prompt streams are only chained through one layer here"
    layer = 0
    n_p, t_p, d = x_prompt.shape
    n_s, t_s, _ = x_sample.shape
    heads, page, hd = cache_k.shape[2:]
    b_heads, dk, dv = state_gla.shape[2:]
    bk, bv = b_heads * dk, b_heads * dv
    aqk = heads * hd
    rank = w_gate_up.shape[1]
    past_len = page_table.shape[1] * page
    ppb = MOBA_BLOCK // page
    n_full = past_len // MOBA_BLOCK
    assert past_len % MOBA_BLOCK == 0 and n_full >= MOBA_TOPK and rank <= LANES
    assert w_in.shape[2] == 3 * aqk + 2 * bk + 2 * bv + rank

    w_qkv_t = w_in[layer][:, :3 * aqk].T.astype(BF16)
    w_rest = jnp.pad(w_in[layer][:, 3 * aqk:], ((0, 0), (0, LANES - rank))).astype(BF16)
    w_gu_p = jnp.pad(w_gate_up[layer], ((0, LANES - rank), (0, 0))).astype(BF16)
    g1 = norm1_g[layer][None, :]
    g2 = norm2_g[layer][None, :]
    qg = q_norm_g[layer][:, None]
    kg = k_norm_g[layer][:, None]
    bgu = b_gate_up[layer][None, :]
    gn = gla_norm_g[layer][None, :]
    w_a = w_branch_a[layer].astype(BF16)
    w_b = w_branch_b[layer].astype(BF16)
    w_mg = w_merge_gate[layer].astype(BF16)
    w_o = w_out[layer].astype(BF16)
    w_r = w_router[layer]
    b_r = b_router[layer][None, :]
    expert_weights = (w_up[layer], b_up[layer][:, None, 0::2], b_up[layer][:, None, 1::2],
                      w_down[layer], b_down[layer][:, None, :])
    proj = functools.partial(_project, norm1_g=g1, w_qkv_t=w_qkv_t, w_rest=w_rest, q_norm_g=qg,
                             k_norm_g=kg, w_gu_p=w_gu_p, b_gu=bgu, heads=heads, hd=hd, bk=bk, bv=bv)
    merge = functools.partial(_merge, norm1_g=g1, w_a=w_a, w_b=w_b, w_mg=w_mg, w_o=w_o,
                              norm2_g=g2, w_router=w_r, b_router=b_r)

    cos_p, sin_p = _rope_tables_t(jnp.arange(t_p), hd)
    qt, kt, vt, qb, kb, vb, la, gb = proj(x_prompt, cos_p, sin_p)
    n_part = n_p // PROMPT_SPLITS if n_p % PROMPT_SPLITS == 0 else n_p
    x_rows = x_prompt.reshape(n_p * t_p, d)
    y_parts, st_parts = [], []
    for b0 in range(0, n_p, n_part):
        o_a = _moba_prompt(qt, kt, vt, b0, n_part)
        o_b, st = _gla_prompt(qb, kb, vb, la, gb, gn, b0, n_part, heads=b_heads, dk=dk, dv=dv)
        x2, h2, idx, gate = merge(x_rows, b0 * t_p, o_a.reshape(n_part * t_p, -1),
                                  o_b.reshape(n_part * t_p, -1))
        y_parts.append(_moe(x2, h2, idx, gate, expert_weights, MOE_ROWS))
        st_parts.append(st)
    y_prompt = jnp.concatenate(y_parts).reshape(n_p, t_p, d)
    s_prompt = jnp.swapaxes(jnp.concatenate(st_parts), 2, 3)

    n_tok_s = n_s * t_s
    cos_s, sin_s = _rope_tables_t(past_len + jnp.arange(t_s), hd)
    qt_s, kt_s, vt_s, qb_s, kb_s, vb_s, la_s, gb_s = proj(
        x_sample.reshape(1, n_tok_s, d), jnp.tile(cos_s, (1, n_s)), jnp.tile(sin_s, (1, n_s)))
    per_seq = lambda a: a.reshape(heads, hd, n_s, t_s).transpose(2, 0, 3, 1)
    qa_s, ka_s, va_s = per_seq(qt_s), per_seq(kt_s), per_seq(vt_s)
    cache_kt = jnp.swapaxes(cache_k, 3, 4)
    cache_vt = jnp.swapaxes(cache_v, 3, 4)
    kmean = _sample_kmean(cache_kt, layer, page_table, n_full, ppb).transpose(0, 2, 1, 3)
    blk_idx = _sample_pick(qa_s, kmean, MOBA_TOPK)
    lpages = blk_idx[..., None] * ppb + jnp.arange(ppb, dtype=jnp.int32)
    phys = jnp.take_along_axis(page_table[:, None, None, :],
                               lpages.reshape(n_s, heads, t_s, MOBA_TOPK * ppb), axis=-1)
    o_a_s = _sample_attn(qa_s, ka_s, va_s, cache_kt, cache_vt, layer, phys)
    o_a_s = o_a_s.transpose(0, 2, 1, 3).reshape(n_tok_s, aqk).astype(BF16)
    tr = lambda a: a.reshape(n_s, t_s, b_heads, dk).transpose(0, 2, 3, 1)
    o_b_s, s_sample = _gla_sample(tr(qb_s), tr(kb_s), tr(la_s), vb_s.reshape(n_s, t_s, bv),
                                  gb_s.reshape(n_s, t_s, bv), gn, state_gla, layer, dk=dk, dv=dv)
    x2_s, h2_s, idx_s, gate_s = merge(x_sample.reshape(n_tok_s, d), 0, o_a_s,
                                      o_b_s.reshape(n_tok_s, bv))
    y_sample = _moe(x2_s, h2_s, idx_s, gate_s, expert_weights,
                    min(MOE_ROWS, n_tok_s)).reshape(n_s, t_s, d)
    k_prompt = jnp.swapaxes(kt, 2, 3)
    v_prompt = jnp.swapaxes(vt, 2, 3)
    return (y_prompt, y_sample, k_prompt[None], v_prompt[None], ka_s[None], va_s[None],
            s_prompt[None], s_sample[None])
```
